```python
import functools
import math
import jax
import jax.numpy as jnp
from jax import lax
import numpy as np

D_MODEL = 1024
BATCH = 8
SEQ = 4096
DEPTH = 2

GRID_W = 64
CTX_LEN = 256
EPS = 1e-6

MIX_WIDTH = D_MODEL
HEAD_DIM = 64
NA_WIDTH = MIX_WIDTH // 2
NA_HEADS = NA_WIDTH // HEAD_DIM
MAX_KR = 8
KC = 16
Q_COL_BLOCK = KC
KEY_COL_BLOCK = 2 * KC
FOURIER_WIDTH = MIX_WIDTH - NA_WIDTH
FOURIER_DIM = 64
FOURIER_GROUPS = FOURIER_WIDTH // FOURIER_DIM
EVEN_IN_WIDTH = 3 * NA_WIDTH + FOURIER_WIDTH

SSM_WIDTH = MIX_WIDTH
SSM_GROUP = 16
SSM_GROUPS = SSM_WIDTH // SSM_GROUP
SSM_STATE = 64
DT_MIN = 1e-3
DT_MAX = 1e-1
LAMBDA_RE_MAX = -1e-4

N_EXPERTS = 32
TOP_K = 4
D_FF = D_MODEL
SWIGLU_ALPHA = 1.702
SWIGLU_LIMIT = 7.0
EXPERT_BLOCK = 512

kernel_name = 'hybrid_na_fnet_s5_moe_dit'


def rmsnorm(x, g):
    x32 = x.astype(jnp.float32)
    y = x32 * lax.rsqrt(jnp.mean(x32 * x32, axis=-1, keepdims=True) + EPS)
    return (y * g.astype(jnp.float32)).astype(x.dtype)


def modulate(x, g, shift, scale):
    return rmsnorm(x, g) * (1 + scale) + shift


def _na_tables():
    n_cb = GRID_W // Q_COL_BLOCK
    q_cols = np.arange(GRID_W).reshape(n_cb, Q_COL_BLOCK)
    kb = np.clip(np.arange(n_cb) * Q_COL_BLOCK - KC // 2, 0, GRID_W - KEY_COL_BLOCK)
    key_cols = kb[:, None] + np.arange(KEY_COL_BLOCK)
    win0 = np.clip(q_cols - KC // 2, 0, GRID_W - KC)
    kcol = key_cols[:, None, :]
    valid = (kcol >= win0[..., None]) & (kcol < win0[..., None] + KC)
    col_off = np.clip(kcol - q_cols[..., None], -(KC - 1), KC - 1) + KC - 1
    return key_cols, valid, col_off


def neighbourhood_attention(q, k, v, k_ctx, v_ctx, rpb):
    bsz, seq_len, n_heads, hd = q.shape
    rows = seq_len // GRID_W
    kr = min(MAX_KR, rows)
    n_cb = GRID_W // Q_COL_BLOCK
    n_win = kr * KEY_COL_BLOCK
    key_cols, col_valid, col_off = _na_tables()
    mask = jnp.asarray(np.broadcast_to(col_valid[:, :, None, :], (n_cb, Q_COL_BLOCK, kr, KEY_COL_BLOCK)).reshape(n_cb, Q_COL_BLOCK, n_win))
    scale = HEAD_DIM ** -0.5
    kg = k.reshape(bsz, rows, GRID_W, n_heads, hd)
    vg = v.reshape(bsz, rows, GRID_W, n_heads, hd)
    qg = q.reshape(bsz, rows, GRID_W, n_heads, hd).transpose(1, 0, 2, 3, 4)
    rpb32 = rpb.astype(jnp.float32)

    def row_block(args):
        r, q_row = args
        r0 = jnp.clip(r - kr // 2, 0, rows - kr)
        k_band = lax.dynamic_slice_in_dim(kg, r0, kr, axis=1)
        v_band = lax.dynamic_slice_in_dim(vg, r0, kr, axis=1)
        k_blk = k_band[:, :, key_cols].transpose(0, 2, 1, 3, 4, 5).reshape(bsz, n_cb, n_win, n_heads, hd)
        v_blk = v_band[:, :, key_cols].transpose(0, 2, 1, 3, 4, 5).reshape(bsz, n_cb, n_win, n_heads, hd)
        qb = q_row.reshape(bsz, n_cb, Q_COL_BLOCK, n_heads, hd)
        s_win = jnp.einsum('bnqhd,bnkhd->bhnqk', qb, k_blk).astype(jnp.float32) * scale
        row_idx = r0 + jnp.arange(kr) - r + MAX_KR - 1
        rpb_rows = jnp.take(rpb32, row_idx, axis=1)
        bias = rpb_rows[:, :, col_off].transpose(0, 2, 3, 1, 4).reshape(n_heads, n_cb, Q_COL_BLOCK, n_win)
        s_win = jnp.where(mask, s_win + bias, -jnp.inf)
        s_ctx = jnp.einsum('bnqhd,bkhd->bhnqk', qb, k_ctx).astype(jnp.float32) * scale
        p = jax.nn.softmax(jnp.concatenate([s_win, s_ctx], axis=-1), axis=-1).astype(v.dtype)
        o = (jnp.einsum('bhnqk,bnkhd->bnqhd', p[..., :n_win], v_blk)
             + jnp.einsum('bhnqk,bkhd->bnqhd', p[..., n_win:], v_ctx))
        return o.reshape(bsz, GRID_W, n_heads, hd)

    out = lax.map(row_block, (jnp.arange(rows), qg))
    return out.transpose(1, 0, 2, 3, 4).reshape(bsz, seq_len, n_heads, hd)


def context_attention(q, k, v):
    s = jnp.einsum('bqhd,bkhd->bhqk', q, k).astype(jnp.float32) * HEAD_DIM ** -0.5
    p = jax.nn.softmax(s, axis=-1).astype(v.dtype)
    return jnp.einsum('bhqk,bkhd->bqhd', p, v)


def fourier_mix(u, w_fourier):
    f = jnp.fft.fft2(u.astype(jnp.float32), axes=(1, 3), norm='ortho').real.astype(u.dtype)
    return jnp.einsum('bngc,gce->bnge', f, w_fourier)


def even_mixer(w_in, rpb, w_fourier, w_out, h_lat, h_ctx, need_ctx):
    def project(h):
        lead = h.shape[:-1]
        q, k, v, f = jnp.split(h @ w_in, [NA_WIDTH, 2 * NA_WIDTH, 3 * NA_WIDTH], axis=-1)
        return (q.reshape(*lead, NA_HEADS, HEAD_DIM), k.reshape(*lead, NA_HEADS, HEAD_DIM),
                v.reshape(*lead, NA_HEADS, HEAD_DIM), f.reshape(*lead, FOURIER_GROUPS, FOURIER_DIM))

    def merge(a, f):
        lead = a.shape[:2]
        return jnp.concatenate([a.reshape(*lead, NA_WIDTH), f.reshape(*lead, FOURIER_WIDTH)], axis=-1) @ w_out

    q_l, k_l, v_l, f_l = project(h_lat)
    q_c, k_c, v_c, f_c = project(h_ctx)
    y_lat = merge(neighbourhood_attention(q_l, k_l, v_l, k_c, v_c, rpb), fourier_mix(f_l, w_fourier))
    if need_ctx:
        y_ctx = merge(context_attention(q_c, k_c, v_c), fourier_mix(f_c, w_fourier))
        return y_lat, y_ctx
    return y_lat, None


def s5_discretise(lam_re, lam_im, log_dt, b_re, b_im):
    lam = lax.complex(jnp.minimum(lam_re.astype(jnp.float32), LAMBDA_RE_MAX), lam_im.astype(jnp.float32))
    dt = jnp.exp(log_dt.astype(jnp.float32))[:, None]
    a_bar = jnp.exp(lam * dt)
    b = lax.complex(b_re.astype(jnp.float32), b_im.astype(jnp.float32))
    b_bar = ((a_bar - 1) / lam)[..., None] * b
    return a_bar, b_bar


def linear_recurrence(a_bar, bu, h0, reverse):
    a = jnp.broadcast_to(a_bar, bu.shape)

    def combine(e1, e2):
        a1, b1 = e1
        a2, b2 = e2
        return a1 * a2, a2 * b1 + b2

    a_cum, h = lax.associative_scan(combine, (a, bu), reverse=reverse, axis=0)
    if h0 is not None:
        h = h + a_cum * h0
    return h


def s5_drive(u, b_bar):
    return lax.complex(jnp.einsum('ngc,gpc->ngp', u, b_bar.real), jnp.einsum('ngc,gpc->ngp', u, b_bar.imag))


def s5_readout(h, c_re, c_im):
    return jnp.einsum('ngp,gcp->ngc', h.real, c_re) - jnp.einsum('ngp,gcp->ngc', h.imag, c_im)


def odd_mixer(w_in, lam_re, lam_im, log_dt, b_re, b_im, c_re, c_im, d_skip, w_glu, b_glu, w_out, h_lat, h_ctx, need_ctx):
    bsz, seq_len, _ = h_lat.shape
    ctx_len = h_ctx.shape[1]
    u_lat = (h_lat @ w_in).astype(jnp.float32).reshape(bsz, seq_len, SSM_GROUPS, SSM_GROUP)
    u_ctx = (h_ctx @ w_in).astype(jnp.float32).reshape(bsz, ctx_len, SSM_GROUPS, SSM_GROUP)
    dirs = []
    for i in range(2):
        a_bar, b_bar = s5_discretise(lam_re[i], lam_im[i], log_dt[i], b_re[i], b_im[i])
        dirs.append((a_bar, b_bar, c_re[i].astype(jnp.float32), c_im[i].astype(jnp.float32), i == 1))
    d32 = d_skip.astype(jnp.float32)

    def one_sample(args):
        ul, uc = args
        y_lat = d32 * ul
        y_ctx = d32 * uc if need_ctx else None
        for a_bar, b_bar, cr, ci, rev in dirs:
            h_c = linear_recurrence(a_bar, s5_drive(uc, b_bar), None, rev)
            h_end = h_c[0] if rev else h_c[-1]
            h_l = linear_recurrence(a_bar, s5_drive(ul, b_bar), h_end, rev)
            y_lat = y_lat + s5_readout(h_l, cr, ci)
            if need_ctx:
                y_ctx = y_ctx + s5_readout(h_c, cr, ci)
        if need_ctx:
            return y_lat, y_ctx
        return y_lat

    def glu_out(y):
        g = jax.nn.gelu(y.reshape(*y.shape[:2], SSM_WIDTH).astype(h_lat.dtype))
        return (g * jax.nn.sigmoid(g @ w_glu + b_glu)) @ w_out

    if need_ctx:
        y_lat, y_ctx = lax.map(one_sample, (u_lat, u_ctx))
        return glu_out(y_lat), glu_out(y_ctx)
    y_lat = lax.map(one_sample, (u_lat, u_ctx))
    return glu_out(y_lat), None


def moe(h, router_w, router_b, w_gate_up, b_gate_up, w_down, b_down):
    n_tok, d_model = h.shape
    logits = (h @ router_w).astype(jnp.float32) + router_b.astype(jnp.float32)
    top_logit, top_idx = lax.top_k(logits, TOP_K)
    gates = jax.nn.softmax(top_logit, axis=-1)
    n_pairs = n_tok * TOP_K
    flat_e = top_idx.reshape(-1)
    flat_tok = jnp.repeat(jnp.arange(n_tok, dtype=jnp.int32), TOP_K)
    flat_gate = gates.reshape(-1)
    order = jnp.argsort(flat_e)
    sorted_e = flat_e[order]
    counts = jnp.bincount(flat_e, length=N_EXPERTS)
    start = jnp.cumsum(counts) - counts
    rank = jnp.arange(n_pairs) - start[sorted_e]
    padded = (counts + EXPERT_BLOCK - 1) // EXPERT_BLOCK * EXPERT_BLOCK
    pend = jnp.cumsum(padded)
    dest = (pend - padded)[sorted_e] + rank
    n_blocks = -(-n_pairs // EXPERT_BLOCK) + N_EXPERTS
    cap = n_blocks * EXPERT_BLOCK
    row_tok = jnp.full((cap,), n_tok, jnp.int32).at[dest].set(flat_tok[order])
    row_gate = jnp.zeros((cap,), jnp.float32).at[dest].set(flat_gate[order])
    block_expert = jnp.minimum(jnp.searchsorted(pend, jnp.arange(n_blocks) * EXPERT_BLOCK, side='right'), N_EXPERTS - 1)
    h_pad = jnp.concatenate([h, jnp.zeros((1, d_model), h.dtype)], axis=0)

    def expert_block(args):
        tok, e = args
        gu = h_pad[tok] @ w_gate_up[e] + b_gate_up[e]
        gate, up = jnp.split(gu, 2, axis=-1)
        gate = jnp.minimum(gate, SWIGLU_LIMIT)
        up = jnp.clip(up, -SWIGLU_LIMIT, SWIGLU_LIMIT)
        act = (up + 1) * (gate * jax.nn.sigmoid(SWIGLU_ALPHA * gate))
        return act @ w_down[e] + b_down[e]

    y_rows = lax.map(expert_block, (row_tok.reshape(n_blocks, EXPERT_BLOCK), block_expert))
    y_rows = y_rows.reshape(cap, d_model) * row_gate[:, None].astype(h.dtype)
    return jax.ops.segment_sum(y_rows, row_tok, num_segments=n_tok + 1)[:n_tok]


def trunk_layer(x, x_ctx, c, c_ctx, ada_w, ada_b, norm_mix, norm_ffn, moe_params, mixer, need_ctx):
    bsz, seq_len, d_model = x.shape
    mod = (jax.nn.silu(c) @ ada_w + ada_b)[:, None, :]
    mod_c = jax.nn.silu(c_ctx) @ ada_w + ada_b
    sh1, sc1, g1, sh2, sc2, g2 = jnp.split(mod, 6, axis=-1)
    sh1c, sc1c, g1c, sh2c, sc2c, g2c = jnp.split(mod_c, 6, axis=-1)
    y, y_ctx = mixer(modulate(x, norm_mix, sh1, sc1), modulate(x_ctx, norm_mix, sh1c, sc1c), need_ctx)
    x = x + g1 * y
    h2 = modulate(x, norm_ffn, sh2, sc2)
    if need_ctx:
        x_ctx = x_ctx + g1c * y_ctx
        h2c = modulate(x_ctx, norm_ffn, sh2c, sc2c)
        n_lat = bsz * seq_len
        f = moe(jnp.concatenate([h2.reshape(-1, d_model), h2c.reshape(-1, d_model)], axis=0), *moe_params)
        x = x + g2 * f[:n_lat].reshape(x.shape)
        x_ctx = x_ctx + g2c * f[n_lat:].reshape(x_ctx.shape)
        return x, x_ctx
    x = x + g2 * moe(h2.reshape(-1, d_model), *moe_params).reshape(x.shape)
    return x, None


def setup_inputs(seed: int = 0) -> dict:
    key = jax.random.key(seed)
    keys = iter(jax.random.split(key, 64))

    def nrm(shape, s):
        return s * jax.random.normal(next(keys), shape, jnp.float32)

    def gain():
        return 1.0 + nrm((D_MODEL,), 0.1)

    inp = {}
    inp['x'] = nrm((BATCH, SEQ, D_MODEL), 1.0)
    inp['c'] = nrm((BATCH, D_MODEL), 1.0)
    inp['ctx'] = nrm((BATCH, CTX_LEN, D_MODEL), 1.0)
    inp['c_ctx'] = nrm((D_MODEL,), 1.0)

    def moe_params(p):
        inp[p + 'router_w'] = nrm((D_MODEL, N_EXPERTS), D_MODEL ** -0.5)
        inp[p + 'router_b'] = nrm((N_EXPERTS,), 0.01)
        inp[p + 'w_gate_up'] = nrm((N_EXPERTS, D_MODEL, 2 * D_FF), D_MODEL ** -0.5)
        inp[p + 'b_gate_up'] = nrm((N_EXPERTS, 2 * D_FF), 0.01)
        inp[p + 'w_down'] = nrm((N_EXPERTS, D_FF, D_MODEL), D_FF ** -0.5)
        inp[p + 'b_down'] = nrm((N_EXPERTS, D_MODEL), 0.01)

    inp['l0_ada_w'] = nrm((D_MODEL, 6 * D_MODEL), D_MODEL ** -0.5)
    inp['l0_ada_b'] = nrm((6 * D_MODEL,), 0.01)
    inp['l0_norm_mix'] = gain()
    inp['l0_w_in'] = nrm((D_MODEL, EVEN_IN_WIDTH), D_MODEL ** -0.5)
    inp['l0_rpb'] = nrm((NA_HEADS, 2 * MAX_KR - 1, 2 * KC - 1), 0.1)
    inp['l0_w_fourier'] = nrm((FOURIER_GROUPS, FOURIER_DIM, FOURIER_DIM), FOURIER_DIM ** -0.5)
    inp['l0_w_out'] = nrm((MIX_WIDTH, D_MODEL), MIX_WIDTH ** -0.5)
    inp['l0_norm_ffn'] = gain()
    moe_params('l0_')
    inp['l1_ada_w'] = nrm((D_MODEL, 6 * D_MODEL), D_MODEL ** -0.5)
    inp['l1_ada_b'] = nrm((6 * D_MODEL,), 0.01)
    inp['l1_norm_mix'] = gain()
    inp['l1_w_in'] = nrm((D_MODEL, SSM_WIDTH), D_MODEL ** -0.5)
    inp['l1_lambda_re'] = -0.5 + nrm((2, SSM_GROUPS, SSM_STATE), 0.01)
    inp['l1_lambda_im'] = math.pi * jnp.arange(SSM_STATE, dtype=jnp.float32) + nrm((2, SSM_GROUPS, SSM_STATE), 0.01)
    inp['l1_log_dt'] = jax.random.uniform(next(keys), (2, SSM_GROUPS), jnp.float32, math.log(DT_MIN), math.log(DT_MAX))
    inp['l1_b_re'] = nrm((2, SSM_GROUPS, SSM_STATE, SSM_GROUP), (2 * SSM_GROUP) ** -0.5)
    inp['l1_b_im'] = nrm((2, SSM_GROUPS, SSM_STATE, SSM_GROUP), (2 * SSM_GROUP) ** -0.5)
    inp['l1_c_re'] = nrm((2, SSM_GROUPS, SSM_GROUP, SSM_STATE), (2 * SSM_STATE) ** -0.5)
    inp['l1_c_im'] = nrm((2, SSM_GROUPS, SSM_GROUP, SSM_STATE), (2 * SSM_STATE) ** -0.5)
    inp['l1_d_skip'] = nrm((SSM_GROUPS, SSM_GROUP), 1.0)
    inp['l1_w_glu'] = nrm((SSM_WIDTH, SSM_WIDTH), SSM_WIDTH ** -0.5)
    inp['l1_b_glu'] = nrm((SSM_WIDTH,), 0.01)
    inp['l1_w_out'] = nrm((SSM_WIDTH, D_MODEL), SSM_WIDTH ** -0.5)
    inp['l1_norm_ffn'] = gain()
    moe_params('l1_')
    inp['final_norm'] = gain()
    return inp


def reference(x, c, ctx, c_ctx,
              l0_ada_w, l0_ada_b, l0_norm_mix, l0_w_in, l0_rpb, l0_w_fourier, l0_w_out, l0_norm_ffn,
              l0_router_w, l0_router_b, l0_w_gate_up, l0_b_gate_up, l0_w_down, l0_b_down,
              l1_ada_w, l1_ada_b, l1_norm_mix, l1_w_in, l1_lambda_re, l1_lambda_im, l1_log_dt,
              l1_b_re, l1_b_im, l1_c_re, l1_c_im, l1_d_skip, l1_w_glu, l1_b_glu, l1_w_out, l1_norm_ffn,
              l1_router_w, l1_router_b, l1_w_gate_up, l1_b_gate_up, l1_w_down, l1_b_down,
              final_norm):
    even = functools.partial(even_mixer, l0_w_in, l0_rpb, l0_w_fourier, l0_w_out)
    odd = functools.partial(odd_mixer, l1_w_in, l1_lambda_re, l1_lambda_im, l1_log_dt, l1_b_re, l1_b_im,
                            l1_c_re, l1_c_im, l1_d_skip, l1_w_glu, l1_b_glu, l1_w_out)
    layers = [
        (l0_ada_w, l0_ada_b, l0_norm_mix, l0_norm_ffn,
         (l0_router_w, l0_router_b, l0_w_gate_up, l0_b_gate_up, l0_w_down, l0_b_down), even),
        (l1_ada_w, l1_ada_b, l1_norm_mix, l1_norm_ffn,
         (l1_router_w, l1_router_b, l1_w_gate_up, l1_b_gate_up, l1_w_down, l1_b_down), odd),
    ]
    x_ctx = ctx
    for i in range(DEPTH):
        ada_w, ada_b, norm_mix, norm_ffn, moe_p, mixer = layers[i]
        x, x_ctx = trunk_layer(x, x_ctx, c, c_ctx, ada_w, ada_b, norm_mix, norm_ffn, moe_p, mixer, i < DEPTH - 1)
    return rmsnorm(x, final_norm)
```

```python
import functools
import math

import numpy as np
import jax
import jax.numpy as jnp
from jax import lax
from jax.experimental import pallas as pl
from jax.experimental.pallas import tpu as pltpu

F32 = jnp.float32
BF16 = jnp.bfloat16

D_MODEL = 1024
GRID_W = 64
HEAD_DIM = 64
NA_WIDTH = 512
NA_HEADS = 8
MAX_KR = 8
KC = 16
FOURIER_DIM = 64
FOURIER_GROUPS = 8
FOURIER_WIDTH = 512
SSM_GROUP = 16
SSM_GROUPS = 64
SSM_STATE = 64
LAMBDA_RE_MAX = -1e-4
N_EXPERTS = 32
TOP_K = 4
SWIGLU_ALPHA = 1.702
SWIGLU_LIMIT = 7.0
EPS = 1e-6

TM = 512
EXPERT_ROWS = 512
COMBINE_TM = 256
S5_CHUNK = 16
VMEM_LIMIT = 48 * 1024 * 1024


def _cparams(sem):
    return pltpu.CompilerParams(dimension_semantics=sem, vmem_limit_bytes=VMEM_LIMIT)


def _split_bf16(a):
    hi = a.astype(BF16)
    lo = (a - hi.astype(F32)).astype(BF16)
    return hi, lo


def _dot(a, b):
    return jnp.dot(a, b, preferred_element_type=F32)


def _dot_nt(a, b):
    return lax.dot_general(a, b, (((1,), (1,)), ((), ())), preferred_element_type=F32)


def _ada_kernel(c_ref, w_ref, b_ref, o_ref):
    c = c_ref[...]
    s = c * jax.nn.sigmoid(c)
    s_hi, s_lo = _split_bf16(s)
    w_hi, w_lo = _split_bf16(w_ref[...])
    o_ref[...] = _dot(s_hi, w_hi) + _dot(s_lo, w_hi) + _dot(s_hi, w_lo) + b_ref[...]


def _ada(c_pad, ada_w, ada_b):
    n = ada_w.shape[1]
    tn = 1024
    return pl.pallas_call(
        _ada_kernel,
        grid=(n // tn,),
        in_specs=[pl.BlockSpec((c_pad.shape[0], D_MODEL), lambda j: (0, 0)),
                  pl.BlockSpec((D_MODEL, tn), lambda j: (0, j)),
                  pl.BlockSpec((1, tn), lambda j: (0, j))],
        out_specs=pl.BlockSpec((c_pad.shape[0], tn), lambda j: (0, j)),
        out_shape=jax.ShapeDtypeStruct((c_pad.shape[0], n), F32),
        compiler_params=_cparams(("arbitrary",)),
        name="ada_mod",
    )(c_pad, ada_w, ada_b.reshape(1, n))


def _mod_spec(which, tiles_per_batch):
    tiles, bsz = tiles_per_batch
    return pl.BlockSpec((1, 1, D_MODEL), lambda i: (jnp.minimum(i // tiles, bsz) * 6 + which, 0, 0))


def _modulate(x, g, sh, sc):
    ms = jnp.mean(x * x, axis=-1, keepdims=True)
    y = x * lax.rsqrt(ms + EPS) * g
    return y * (1.0 + sc) + sh


def _modproj_kernel(*refs, n_lat_tiles, two_src, widths, q_scale):
    if two_src:
        x_ref, c_ref, sh_ref, sc_ref, g_ref, w_ref = refs[:6]
        o_refs = refs[6:]
        i = pl.program_id(0)
        x = jnp.where(i < n_lat_tiles, x_ref[...], c_ref[...])
    else:
        x_ref, sh_ref, sc_ref, g_ref, w_ref = refs[:5]
        o_refs = refs[5:]
        x = x_ref[...]
    h = _modulate(x, g_ref[...], sh_ref[0], sc_ref[0]).astype(BF16)
    off = 0
    for j, (o_ref, wd) in enumerate(zip(o_refs, widths)):
        y = _dot(h, w_ref[:, off:off + wd])
        if j == 0 and q_scale != 1.0:
            y = y * q_scale
        o_ref[...] = y.astype(o_ref.dtype)
        off += wd


def _modproj(x_srcs, mod, norm_g, w_bf, widths, n_lat, n_tok, tiles_per_batch, q_scale=1.0):
    n_tiles = n_tok // TM
    n_lat_tiles = n_lat // TM
    two_src = len(x_srcs) == 2
    n_out = w_bf.shape[1]
    if two_src:
        x_specs = [pl.BlockSpec((TM, D_MODEL), lambda i: (jnp.minimum(i, n_lat_tiles - 1), 0)),
                   pl.BlockSpec((TM, D_MODEL), lambda i: (jnp.maximum(i - n_lat_tiles, 0), 0))]
    else:
        x_specs = [pl.BlockSpec((TM, D_MODEL), lambda i: (i, 0))]
    return pl.pallas_call(
        functools.partial(_modproj_kernel, n_lat_tiles=n_lat_tiles, two_src=two_src,
                          widths=tuple(widths), q_scale=q_scale),
        grid=(n_tiles,),
        in_specs=x_specs + [_mod_spec(0, tiles_per_batch), _mod_spec(1, tiles_per_batch),
                            pl.BlockSpec((1, D_MODEL), lambda i: (0, 0)),
                            pl.BlockSpec((D_MODEL, n_out), lambda i: (0, 0))],
        out_specs=[pl.BlockSpec((TM, wd), lambda i: (i, 0)) for wd in widths],
        out_shape=[jax.ShapeDtypeStruct((n_tok, wd), BF16) for wd in widths],
        compiler_params=_cparams(("arbitrary",)),
        name="modulate_in_proj",
    )(*x_srcs, mod, mod, norm_g.reshape(1, D_MODEL), w_bf)


def _na_bias_table(rpb):
    qc = np.arange(GRID_W)[:, None]
    kc = np.arange(GRID_W)[None, :]
    win0 = np.clip(qc - KC // 2, 0, GRID_W - KC)
    valid = (kc >= win0) & (kc < win0 + KC)
    col_off = np.clip(kc - qc, -(KC - 1), KC - 1) + KC - 1
    cb = rpb.astype(F32)[:, :, col_off]
    cb = jnp.where(jnp.asarray(valid)[None, None], cb, -jnp.inf)
    return jnp.concatenate([cb[:, :-1], cb[:, 1:]], axis=-1)


def _attn_kernel(q_ref, k_ref, v_ref, qc_ref, kc_ref, vc_ref, bias_ref, o_ref, oc_ref, *, rows):
    lane = lax.broadcasted_iota(jnp.int32, (1, 2 * HEAD_DIM), 1)
    first = lane < HEAD_DIM
    kc = kc_ref[...]
    vc = vc_ref[...]
    zero = jnp.zeros((), BF16)

    def softmax_pv(s_list, v_list):
        m = s_list[0].max(axis=-1, keepdims=True)
        for s in s_list[1:]:
            m = jnp.maximum(m, s.max(axis=-1, keepdims=True))
        den = None
        acc = None
        for s, vv in zip(s_list, v_list):
            p = jnp.exp(s - m)
            ps = p.sum(axis=-1, keepdims=True)
            den = ps if den is None else den + ps
            pv = _dot(p.astype(BF16), vv)
            acc = pv if acc is None else acc + pv
        return acc / den

    def row_body(r, carry):
        r0 = jnp.clip(r - MAX_KR // 2, 0, rows - MAX_KR)
        q_r = q_ref[pl.ds(pl.multiple_of(r * GRID_W, GRID_W), GRID_W), :]
        band = pl.ds(pl.multiple_of(r0 * GRID_W, GRID_W), MAX_KR * GRID_W)
        kb = k_ref[band, :]
        vb = v_ref[band, :]
        base = r0 - r + MAX_KR - 1
        outs = []
        for hh in range(2):
            q_h = jnp.where(first if hh == 0 else ~first, q_r, zero)
            bias = jnp.concatenate([bias_ref[hh, base + 2 * jj] for jj in range(MAX_KR // 2)], axis=-1)
            s_w = _dot_nt(q_h, kb) + bias
            s_c = _dot_nt(q_h, kc)
            outs.append(softmax_pv([s_w, s_c], [vb, vc]))
        o_ref[pl.ds(pl.multiple_of(r * GRID_W, GRID_W), GRID_W), :] = (
            jnp.where(first, outs[0], outs[1]).astype(o_ref.dtype))
        return carry

    lax.fori_loop(0, rows, row_body, 0)

    q_c = qc_ref[...]
    outs = []
    for hh in range(2):
        q_h = jnp.where(first if hh == 0 else ~first, q_c, zero)
        outs.append(softmax_pv([_dot_nt(q_h, kc)], [vc]))
    oc_ref[...] = jnp.where(first, outs[0], outs[1]).astype(oc_ref.dtype)


def _attention(q, k, v, bias2, bsz, seq, ctx_len):
    rows = seq // GRID_W
    assert rows >= MAX_KR and seq % ctx_len == 0
    cb0 = bsz * seq // ctx_len
    lat = pl.BlockSpec((seq, 2 * HEAD_DIM), lambda b, hp: (b, hp))
    ctx = pl.BlockSpec((ctx_len, 2 * HEAD_DIM), lambda b, hp: (cb0 + b, hp))
    return pl.pallas_call(
        functools.partial(_attn_kernel, rows=rows),
        grid=(bsz, NA_HEADS // 2),
        in_specs=[lat, lat, lat, ctx, ctx, ctx,
                  pl.BlockSpec((2, 2 * MAX_KR - 2, GRID_W, 2 * GRID_W), lambda b, hp: (hp, 0, 0, 0))],
        out_specs=[pl.BlockSpec((seq, 2 * HEAD_DIM), lambda b, hp: (b, hp)),
                   pl.BlockSpec((ctx_len, 2 * HEAD_DIM), lambda b, hp: (b, hp))],
        out_shape=[jax.ShapeDtypeStruct((bsz * seq, NA_WIDTH), BF16),
                   jax.ShapeDtypeStruct((bsz * ctx_len, NA_WIDTH), BF16)],
        compiler_params=_cparams(("arbitrary", "arbitrary")),
        name="na_attention",
    )(q, k, v, q, k, v, bias2)


@functools.lru_cache(maxsize=None)
def _dft_tables(n):
    j = np.arange(n, dtype=np.int64)
    ang = 2.0 * np.pi * ((j[:, None] * j[None, :]) % n).astype(np.float64) / n
    return np.cos(ang).astype(BF16), np.sin(ang).astype(BF16)


def _fourier_kernel(cos_ref, sin_ref, f_ref, bdc_ref, bds_ref, bdw_ref, o_ref, *, scale):
    fb = f_ref[...]
    zr = _dot(cos_ref[...], fb)
    zs = _dot(sin_ref[...], fb)
    fr = (_dot(zr.astype(BF16), bdc_ref[...]) - _dot(zs.astype(BF16), bds_ref[...])) * scale
    o_ref[...] = _dot(fr.astype(BF16), bdw_ref[...]).astype(o_ref.dtype)


def _fourier(f_all, bdw, bsz, n, row_block0, tm):
    cos_n, sin_n = _dft_tables(n)
    cos64, sin64 = _dft_tables(FOURIER_DIM)
    eye = np.eye(FOURIER_GROUPS)
    bdc = jnp.asarray(np.kron(eye, cos64.astype(np.float32)), BF16)
    bds = jnp.asarray(np.kron(eye, sin64.astype(np.float32)), BF16)
    mt = n // tm
    small = pl.BlockSpec((FOURIER_WIDTH, FOURIER_WIDTH), lambda b, m: (0, 0))
    return pl.pallas_call(
        functools.partial(_fourier_kernel, scale=1.0 / math.sqrt(n * FOURIER_DIM)),
        grid=(bsz, mt),
        in_specs=[pl.BlockSpec((tm, n), lambda b, m: (m, 0)),
                  pl.BlockSpec((tm, n), lambda b, m: (m, 0)),
                  pl.BlockSpec((n, FOURIER_WIDTH), lambda b, m: (row_block0 + b, 0)),
                  small, small, small],
        out_specs=pl.BlockSpec((tm, FOURIER_WIDTH), lambda b, m: (b * mt + m, 0)),
        out_shape=jax.ShapeDtypeStruct((bsz * n, FOURIER_WIDTH), BF16),
        compiler_params=_cparams(("arbitrary", "arbitrary")),
        name="fourier_mix",
    )(jnp.asarray(cos_n), jnp.asarray(sin_n), f_all, bdc, bds, bdw)


def _route(h2, rw_hi_ref, rw_lo_ref, rb_ref, m_ref, g_ref):
    h_hi, h_lo = _split_bf16(h2)
    logits = (_dot(h_hi, rw_hi_ref[...]) + _dot(h_lo, rw_hi_ref[...]) + _dot(h_hi, rw_lo_ref[...])
              + rb_ref[...])
    lane = lax.broadcasted_iota(jnp.int32, logits.shape, 1).astype(F32)
    work = logits
    sel_any = jnp.zeros(logits.shape, jnp.bool_)
    top = None
    for _ in range(TOP_K):
        m = work.max(axis=-1, keepdims=True)
        if top is None:
            top = m
        idx = jnp.where(work == m, lane, float(N_EXPERTS)).min(axis=-1, keepdims=True)
        sel = lane == idx
        sel_any = sel_any | sel
        work = jnp.where(sel, -jnp.inf, work)
    e = jnp.where(sel_any, jnp.exp(logits - top), 0.0)
    g_ref[...] = e / e.sum(axis=-1, keepdims=True)
    m_ref[...] = sel_any.astype(F32).astype(m_ref.dtype)


def _post_l0_kernel(a_ref, ac_ref, f_ref, fc_ref, x_ref, c_ref, g1_ref, sh_ref, sc_ref, ng_ref, w_ref,
                    rwh_ref, rwl_ref, rb_ref, xo_ref, h_ref, m_ref, g_ref, *, n_lat_tiles):
    lat = pl.program_id(0) < n_lat_tiles
    a = jnp.where(lat, a_ref[...], ac_ref[...])
    f = jnp.where(lat, f_ref[...], fc_ref[...])
    x = jnp.where(lat, x_ref[...], c_ref[...])
    y = _dot(a, w_ref[:NA_WIDTH, :]) + _dot(f, w_ref[NA_WIDTH:, :])
    x = x + g1_ref[0] * y
    xo_ref[...] = x
    h2 = _modulate(x, ng_ref[...], sh_ref[0], sc_ref[0])
    h_ref[...] = h2
    _route(h2, rwh_ref, rwl_ref, rb_ref, m_ref, g_ref)


def _post_l1_kernel(y_ref, x_ref, g1_ref, sh_ref, sc_ref, ng_ref, wg_ref, bg_ref, w_ref,
                    rwh_ref, rwl_ref, rb_ref, xo_ref, h_ref, m_ref, g_ref):
    gy = y_ref[...]
    z = _dot(gy, wg_ref[...]) + bg_ref[...]
    v = (gy.astype(F32) * jax.nn.sigmoid(z)).astype(BF16)
    x = x_ref[...] + g1_ref[0] * _dot(v, w_ref[...])
    xo_ref[...] = x
    h2 = _modulate(x, ng_ref[...], sh_ref[0], sc_ref[0])
    h_ref[...] = h2
    _route(h2, rwh_ref, rwl_ref, rb_ref, m_ref, g_ref)


def _router_args(router_w, router_b):
    rw_hi = router_w.astype(BF16)
    rw_lo = (router_w - rw_hi.astype(F32)).astype(BF16)
    return rw_hi, rw_lo, router_b.reshape(1, N_EXPERTS).astype(F32)


def _post_out(n_tok):
    specs = [pl.BlockSpec((TM, D_MODEL), lambda i: (i, 0)),
             pl.BlockSpec((TM, D_MODEL), lambda i: (i, 0)),
             pl.BlockSpec((TM, N_EXPERTS), lambda i: (i, 0)),
             pl.BlockSpec((TM, N_EXPERTS), lambda i: (i, 0))]
    shapes = [jax.ShapeDtypeStruct((n_tok, D_MODEL), F32),
              jax.ShapeDtypeStruct((n_tok, D_MODEL), F32),
              jax.ShapeDtypeStruct((n_tok, N_EXPERTS), BF16),
              jax.ShapeDtypeStruct((n_tok, N_EXPERTS), F32)]
    return specs, shapes


def _const_spec(shape):
    return pl.BlockSpec(shape, lambda i: tuple(0 for _ in shape))


def _post_l0(a_lat, a_ctx, f_lat, f_ctx, x, ctx, mod, norm_g, w_out_bf, router, n_lat, n_tok, tiles_per_batch):
    n_lat_tiles = n_lat // TM

    def lat(wd):
        return pl.BlockSpec((TM, wd), lambda i: (jnp.minimum(i, n_lat_tiles - 1), 0))

    def cx(wd):
        return pl.BlockSpec((TM, wd), lambda i: (jnp.maximum(i - n_lat_tiles, 0), 0))

    out_specs, out_shapes = _post_out(n_tok)
    return pl.pallas_call(
        functools.partial(_post_l0_kernel, n_lat_tiles=n_lat_tiles),
        grid=(n_tok // TM,),
        in_specs=[lat(NA_WIDTH), cx(NA_WIDTH), lat(FOURIER_WIDTH), cx(FOURIER_WIDTH),
                  lat(D_MODEL), cx(D_MODEL),
                  _mod_spec(2, tiles_per_batch), _mod_spec(3, tiles_per_batch), _mod_spec(4, tiles_per_batch),
                  _const_spec((1, D_MODEL)), _const_spec((D_MODEL, D_MODEL)),
                  _const_spec((D_MODEL, N_EXPERTS)), _const_spec((D_MODEL, N_EXPERTS)),
                  _const_spec((1, N_EXPERTS))],
        out_specs=out_specs, out_shape=out_shapes,
        compiler_params=_cparams(("arbitrary",)),
        name="post_mixer_l0",
    )(a_lat, a_ctx, f_lat, f_ctx, x, ctx, mod, mod, mod, norm_g.reshape(1, D_MODEL), w_out_bf, *router)


def _post_l1(gy, x_all, mod, norm_g, w_glu_bf, b_glu, w_out_bf, router, n_lat, tiles_per_batch):
    out_specs, out_shapes = _post_out(n_lat)
    tok = pl.BlockSpec((TM, D_MODEL), lambda i: (i, 0))
    return pl.pallas_call(
        _post_l1_kernel,
        grid=(n_lat // TM,),
        in_specs=[tok, tok,
                  _mod_spec(2, tiles_per_batch), _mod_spec(3, tiles_per_batch), _mod_spec(4, tiles_per_batch),
                  _const_spec((1, D_MODEL)), _const_spec((D_MODEL, D_MODEL)), _const_spec((1, D_MODEL)),
                  _const_spec((D_MODEL, D_MODEL)),
                  _const_spec((D_MODEL, N_EXPERTS)), _const_spec((D_MODEL, N_EXPERTS)),
                  _const_spec((1, N_EXPERTS))],
        out_specs=out_specs, out_shape=out_shapes,
        compiler_params=_cparams(("arbitrary",)),
        name="post_mixer_l1",
    )(gy, x_all, mod, mod, mod, norm_g.reshape(1, D_MODEL), w_glu_bf, b_glu.reshape(1, D_MODEL), w_out_bf,
      *router)


def _rank_kernel(m_ref, rank_ref, cnt_ref, base_ref):
    @pl.when(pl.program_id(0) == 0)
    def _():
        base_ref[...] = jnp.zeros_like(base_ref)

    m = m_ref[...]
    r = lax.broadcasted_iota(jnp.int32, (TM, TM), 0)
    c = lax.broadcasted_iota(jnp.int32, (TM, TM), 1)
    tri = (r > c).astype(F32).astype(BF16)
    rank_ref[...] = _dot(tri, m) + base_ref[...]
    base_ref[...] = base_ref[...] + m.astype(F32).sum(axis=0, keepdims=True)
    cnt_ref[...] = base_ref[...]


def _rank(m_sel):
    n_tok = m_sel.shape[0]
    return pl.pallas_call(
        _rank_kernel,
        grid=(n_tok // TM,),
        in_specs=[pl.BlockSpec((TM, N_EXPERTS), lambda i: (i, 0))],
        out_specs=[pl.BlockSpec((TM, N_EXPERTS), lambda i: (i, 0)),
                   pl.BlockSpec((1, N_EXPERTS), lambda i: (0, 0))],
        out_shape=[jax.ShapeDtypeStruct((n_tok, N_EXPERTS), F32),
                   jax.ShapeDtypeStruct((1, N_EXPERTS), F32)],
        scratch_shapes=[pltpu.VMEM((1, N_EXPERTS), F32)],
        compiler_params=_cparams(("arbitrary",)),
        name="moe_rank",
    )(m_sel)


def _dest_kernel(rank_ref, m_ref, g_ref, ps_ref, d_ref, gate_ref):
    m = m_ref[...]
    dest = rank_ref[...] + ps_ref[...]
    r = lax.broadcasted_iota(jnp.int32, (N_EXPERTS, N_EXPERTS), 0)
    c = lax.broadcasted_iota(jnp.int32, (N_EXPERTS, N_EXPERTS), 1)
    upper = (r < c).astype(F32).astype(BF16)
    slot = _dot(m, upper)
    chosen = m.astype(F32) > 0.5
    g = g_ref[...]
    lane = lax.broadcasted_iota(jnp.int32, d_ref.shape, 1)
    d_out = jnp.zeros(d_ref.shape, F32)
    g_out = jnp.zeros(gate_ref.shape, F32)
    for k in range(TOP_K):
        sel = chosen & (slot == float(k))
        dk = jnp.where(sel, dest, 0.0).sum(axis=-1, keepdims=True)
        gk = jnp.where(sel, g, 0.0).sum(axis=-1, keepdims=True)
        d_out = jnp.where(lane == k, dk, d_out)
        g_out = jnp.where(lane == k, gk, g_out)
    d_ref[...] = d_out.astype(jnp.int32)
    gate_ref[...] = g_out


def _dest(rank, m_sel, gates, pstart):
    n_tok = m_sel.shape[0]
    tok = pl.BlockSpec((TM, N_EXPERTS), lambda i: (i, 0))
    wide = pl.BlockSpec((TM, 128), lambda i: (i, 0))
    return pl.pallas_call(
        _dest_kernel,
        grid=(n_tok // TM,),
        in_specs=[tok, tok, tok, _const_spec((1, N_EXPERTS))],
        out_specs=[wide, wide],
        out_shape=[jax.ShapeDtypeStruct((n_tok, 128), jnp.int32),
                   jax.ShapeDtypeStruct((n_tok, 128), F32)],
        compiler_params=_cparams(("arbitrary",)),
        name="moe_dest",
    )(rank, m_sel, gates, pstart)


def _row_copy(src_ref, src_row, dst_ref, dst_row, sem):
    return pltpu.make_async_copy(src_ref.at[pl.ds(src_row, 1), :], dst_ref.at[pl.ds(dst_row, 1), :], sem)


def _dispatch_kernel(cnt_ref, ps_ref, d_ref, h_ref, xs_ref, zero_ref, sem):
    def issue(i, carry):
        for k in range(TOP_K):
            _row_copy(h_ref, i, xs_ref, d_ref[i * TOP_K + k], sem).start()
        return carry

    def drain(i, carry):
        for k in range(TOP_K):
            _row_copy(h_ref, 0, xs_ref, 0, sem).wait()
        return carry

    lax.fori_loop(0, TM, issue, 0)
    lax.fori_loop(0, TM, drain, 0)

    @pl.when(pl.program_id(0) == pl.num_programs(0) - 1)
    def _():
        zero_ref[...] = jnp.zeros_like(zero_ref)
        for e in range(N_EXPERTS):
            pad = (-cnt_ref[e]) % EXPERT_ROWS
            first = ps_ref[e] + cnt_ref[e]

            def fill(r, carry):
                _row_copy(zero_ref, 0, xs_ref, first + r, sem).start()
                return carry

            def fill_wait(r, carry):
                _row_copy(zero_ref, 0, xs_ref, 0, sem).wait()
                return carry

            lax.fori_loop(0, pad, fill, 0)
            lax.fori_loop(0, pad, fill_wait, 0)


def _dispatch(h2, dest_flat, cnt_i, pstart_i, cap):
    n_tok = h2.shape[0]
    grid_spec = pltpu.PrefetchScalarGridSpec(
        num_scalar_prefetch=2,
        grid=(n_tok // TM,),
        in_specs=[pl.BlockSpec((TM * TOP_K,), lambda i, *_: (i,), memory_space=pltpu.SMEM),
                  pl.BlockSpec((TM, D_MODEL), lambda i, *_: (i, 0))],
        out_specs=pl.BlockSpec(memory_space=pl.ANY),
        scratch_shapes=[pltpu.VMEM((8, D_MODEL), F32), pltpu.SemaphoreType.DMA(())],
    )
    return pl.pallas_call(
        _dispatch_kernel,
        grid_spec=grid_spec,
        out_shape=jax.ShapeDtypeStruct((cap, D_MODEL), F32),
        compiler_params=_cparams(("arbitrary",)),
        name="moe_dispatch",
    )(cnt_i, pstart_i, dest_flat, h2)


def _expert_kernel(be_ref, nu_ref, x_ref, wgu_ref, bgu_ref, wd_ref, bd_ref, o_ref):
    @pl.when(pl.program_id(0) < nu_ref[0])
    def _():
        xb = x_ref[...].astype(BF16)
        gu = _dot(xb, wgu_ref[0]) + bgu_ref[0]
        gate = jnp.minimum(gu[:, :D_MODEL], SWIGLU_LIMIT)
        up = jnp.clip(gu[:, D_MODEL:], -SWIGLU_LIMIT, SWIGLU_LIMIT)
        act = (up + 1.0) * (gate * jax.nn.sigmoid(SWIGLU_ALPHA * gate))
        o_ref[...] = _dot(act.astype(BF16), wd_ref[0]) + bd_ref[0]


def _experts(xs, block_expert, n_used, wgu_bf, bgu, wd_bf, bd):
    n_blocks = xs.shape[0] // EXPERT_ROWS

    def row_map(i, be, nu):
        return (jnp.minimum(i, nu[0] - 1), 0)

    def exp_map(i, be, nu):
        return (be[i], 0, 0)

    grid_spec = pltpu.PrefetchScalarGridSpec(
        num_scalar_prefetch=2,
        grid=(n_blocks,),
        in_specs=[pl.BlockSpec((EXPERT_ROWS, D_MODEL), row_map),
                  pl.BlockSpec((1, D_MODEL, 2 * D_MODEL), exp_map),
                  pl.BlockSpec((1, 1, 2 * D_MODEL), exp_map),
                  pl.BlockSpec((1, D_MODEL, D_MODEL), exp_map),
                  pl.BlockSpec((1, 1, D_MODEL), exp_map)],
        out_specs=pl.BlockSpec((EXPERT_ROWS, D_MODEL), row_map),
    )
    return pl.pallas_call(
        _expert_kernel,
        grid_spec=grid_spec,
        out_shape=jax.ShapeDtypeStruct(xs.shape, F32),
        compiler_params=_cparams(("arbitrary",)),
        name="moe_experts",
    )(block_expert, n_used, xs, wgu_bf, bgu.reshape(N_EXPERTS, 1, 2 * D_MODEL), wd_bf,
      bd.reshape(N_EXPERTS, 1, D_MODEL))


def _combine_kernel(d_ref, x_ref, gate_ref, g2_ref, ng_ref, ys_ref, o_ref, buf_ref, sem, *, final_norm):
    def issue(i, carry):
        for k in range(TOP_K):
            pltpu.make_async_copy(ys_ref.at[pl.ds(d_ref[i * TOP_K + k], 1), :],
                                  buf_ref.at[k, pl.ds(i, 1), :], sem).start()
        return carry

    def drain(i, carry):
        for k in range(TOP_K):
            pltpu.make_async_copy(ys_ref.at[pl.ds(0, 1), :], buf_ref.at[k, pl.ds(0, 1), :], sem).wait()
        return carry

    lax.fori_loop(0, COMBINE_TM, issue, 0)
    lax.fori_loop(0, COMBINE_TM, drain, 0)
    gate = gate_ref[...]
    acc = gate[:, 0:1] * buf_ref[0]
    for k in range(1, TOP_K):
        acc = acc + gate[:, k:k + 1] * buf_ref[k]
    x = x_ref[...] + g2_ref[0] * acc
    if final_norm:
        ms = jnp.mean(x * x, axis=-1, keepdims=True)
        x = x * lax.rsqrt(ms + EPS) * ng_ref[...]
    o_ref[...] = x


def _combine(ys, dest_flat, gate4, x_res, mod, norm_g, tiles_per_batch, final_norm):
    n_tok = x_res.shape[0]
    tok = pl.BlockSpec((COMBINE_TM, D_MODEL), lambda i: (i, 0))
    return pl.pallas_call(
        functools.partial(_combine_kernel, final_norm=final_norm),
        grid=(n_tok // COMBINE_TM,),
        in_specs=[pl.BlockSpec((COMBINE_TM * TOP_K,), lambda i: (i,), memory_space=pltpu.SMEM),
                  tok,
                  pl.BlockSpec((COMBINE_TM, 128), lambda i: (i, 0)),
                  _mod_spec(5, tiles_per_batch),
                  _const_spec((1, D_MODEL)),
                  pl.BlockSpec(memory_space=pl.ANY)],
        out_specs=tok,
        out_shape=jax.ShapeDtypeStruct((n_tok, D_MODEL), F32),
        scratch_shapes=[pltpu.VMEM((TOP_K, COMBINE_TM, D_MODEL), F32), pltpu.SemaphoreType.DMA(())],
        compiler_params=_cparams(("arbitrary",)),
        name="moe_combine",
    )(dest_flat, x_res, gate4, mod, norm_g.reshape(1, D_MODEL), ys)


def _moe(h2, m_sel, gates, x_res, mod, moe_w, combine_tiles_per_batch, norm_g, final_norm):
    wgu_bf, bgu, wd_bf, bd = moe_w
    n_tok = h2.shape[0]
    n_blocks = -(-n_tok * TOP_K // EXPERT_ROWS) + N_EXPERTS
    cap = n_blocks * EXPERT_ROWS
    rank, cnt = _rank(m_sel)
    cnt_i = cnt[0].astype(jnp.int32)
    nblk = (cnt_i + EXPERT_ROWS - 1) // EXPERT_ROWS
    bend = jnp.cumsum(nblk)
    pstart_i = (bend - nblk) * EXPERT_ROWS
    n_used = jnp.maximum(bend[-1], 1)
    blk = jnp.arange(n_blocks, dtype=jnp.int32)
    block_expert = jnp.searchsorted(bend, jnp.minimum(blk, n_used - 1), side='right').astype(jnp.int32)
    block_expert = jnp.minimum(block_expert, N_EXPERTS - 1)
    dest4, gate4 = _dest(rank, m_sel, gates, pstart_i.astype(F32).reshape(1, N_EXPERTS))
    dest_flat = dest4[:, :TOP_K].reshape(-1)
    xs = _dispatch(h2, dest_flat, cnt_i, pstart_i, cap)
    ys = _experts(xs, block_expert, n_used.reshape(1).astype(jnp.int32), wgu_bf, bgu, wd_bf, bd)
    return _combine(ys, dest_flat, gate4, x_res, mod, norm_g, combine_tiles_per_batch, final_norm)


def _s5_tables(lam_re, lam_im, log_dt, b_re, b_im, c_re, c_im, d_skip):
    q = S5_CHUNK
    lam = lax.complex(jnp.minimum(lam_re.astype(F32), LAMBDA_RE_MAX), lam_im.astype(F32))
    dt = jnp.exp(log_dt.astype(F32))[..., None]
    a = jnp.exp(lam * dt)
    bbar = ((a - 1) / lam)[..., None] * lax.complex(b_re.astype(F32), b_im.astype(F32))
    cc = lax.complex(c_re.astype(F32), c_im.astype(F32))
    steps = jnp.arange(q + 1, dtype=F32)
    pw = jnp.exp((lam * dt)[:, :, None, :] * steps[None, None, :, None])
    kern = jnp.einsum('dgop,dgkp,dgpc->dgkoc', cc, pw[:, :, :q], bbar).real
    s_idx = np.arange(q)[:, None]
    t_idx = np.arange(q)[None, :]
    toep = []
    for d in range(2):
        lag = (t_idx - s_idx) if d == 0 else (s_idx - t_idx)
        ok = jnp.asarray(lag >= 0)[None, :, None, :, None]
        t = kern[d][:, np.clip(lag, 0, q - 1)]
        t = jnp.transpose(t, (0, 1, 4, 2, 3))
        toep.append(jnp.where(ok, t, 0.0))
    eye = jnp.eye(q * SSM_GROUP, dtype=F32).reshape(q, SSM_GROUP, q, SSM_GROUP)
    toep[0] = toep[0] + eye[None] * d_skip.astype(F32)[:, None, :, None, None]
    toep = jnp.concatenate([t.reshape(SSM_GROUPS, q * SSM_GROUP, q * SSM_GROUP) for t in toep], axis=-1)
    ws = []
    for d in range(2):
        e = pw[d][:, ::-1][:, 1:] if d == 0 else pw[d][:, :q]
        w = e[:, :, None, :] * jnp.transpose(bbar[d], (0, 2, 1))[:, None]
        ws.append(w.reshape(SSM_GROUPS, q * SSM_GROUP, SSM_STATE))
    w_state = jnp.concatenate([ws[0].real, ws[1].real, ws[0].imag, ws[1].imag], axis=-1)
    wo = []
    for d in range(2):
        e = pw[d][:, 1:] if d == 0 else pw[d][:, ::-1][:, :q]
        w = cc[d][:, None] * e[:, :, None, :]
        wo.append(jnp.transpose(w, (0, 3, 1, 2)).reshape(SSM_GROUPS, SSM_STATE, q * SSM_GROUP))
    wo_re = jnp.concatenate([wo[0].real, wo[1].real], axis=1)
    wo_im = jnp.concatenate([-wo[0].imag, -wo[1].imag], axis=1)
    a_q = pw[:, :, q]
    a_re = jnp.concatenate([a_q[0].real, a_q[1].real], axis=-1).reshape(SSM_GROUPS, 1, 2 * SSM_STATE)
    a_im = jnp.concatenate([a_q[0].imag, a_q[1].imag], axis=-1).reshape(SSM_GROUPS, 1, 2 * SSM_STATE)
    return toep.astype(BF16), w_state.astype(BF16), wo_re.astype(BF16), wo_im.astype(BF16), a_re, a_im


def _s5_kernel(ul_ref, uc_ref, toep_ref, ws_ref, wor_ref, woi_ref, ar_ref, ai_ref, o_ref,
               sl_ref, sc_ref, hbr_ref, hbi_ref, *, bsz, n_lat_chunks, n_ctx_chunks):
    ul = ul_ref[0]
    sl_ref[...] = _dot(ul, ws_ref[0])
    sc_ref[...] = _dot(uc_ref[0], ws_ref[0])
    half = 2 * SSM_STATE
    lane = lax.broadcasted_iota(jnp.int32, (bsz, half), 1)
    fwd = lane < SSM_STATE
    ar = jnp.broadcast_to(ar_ref[0], (bsz, half))
    ai = jnp.broadcast_to(ai_ref[0], (bsz, half))

    def load(ref, j, n):
        jf = pl.multiple_of(j * bsz, bsz)
        jb = pl.multiple_of((n - 1 - j) * bsz, bsz)
        s_f = ref[pl.ds(jf, bsz), :]
        s_b = ref[pl.ds(jb, bsz), :]
        return (jnp.where(fwd, s_f[:, :half], s_b[:, :half]),
                jnp.where(fwd, s_f[:, half:], s_b[:, half:]))

    def step(hr, hi, sr, si):
        return ar * hr - ai * hi + sr, ar * hi + ai * hr + si

    def ctx_body(j, carry):
        sr, si = load(sc_ref, j, n_ctx_chunks)
        return step(*carry, sr, si)

    def lat_body(j, carry):
        hr, hi = carry
        jf = pl.multiple_of(j * bsz, bsz)
        jb = pl.multiple_of((n_lat_chunks - 1 - j) * bsz, bsz)
        hbr_ref[pl.ds(jf, bsz), :SSM_STATE] = hr[:, :SSM_STATE]
        hbr_ref[pl.ds(jb, bsz), SSM_STATE:] = hr[:, SSM_STATE:]
        hbi_ref[pl.ds(jf, bsz), :SSM_STATE] = hi[:, :SSM_STATE]
        hbi_ref[pl.ds(jb, bsz), SSM_STATE:] = hi[:, SSM_STATE:]
        sr, si = load(sl_ref, j, n_lat_chunks)
        return step(hr, hi, sr, si)

    zero = jnp.zeros((bsz, half), F32)
    carry = lax.fori_loop(0, n_ctx_chunks, ctx_body, (zero, zero))
    lax.fori_loop(0, n_lat_chunks, lat_body, carry)

    toep = toep_ref[0]
    width = S5_CHUNK * SSM_GROUP
    y = (_dot(ul, toep[:, :width]) + _dot(ul, toep[:, width:])
         + _dot(hbr_ref[...].astype(BF16), wor_ref[0]) + _dot(hbi_ref[...].astype(BF16), woi_ref[0]))
    o_ref[0] = jax.nn.gelu(y).astype(o_ref.dtype)


def _s5(u_lat, u_ctx, tables, bsz):
    toep, w_state, wo_re, wo_im, a_re, a_im = tables
    width = S5_CHUNK * SSM_GROUP
    n_lat_rows = u_lat.shape[1]
    n_ctx_rows = u_ctx.shape[1]

    def gspec(*shape):
        return pl.BlockSpec((1,) + shape, lambda g: (g,) + tuple(0 for _ in shape))

    return pl.pallas_call(
        functools.partial(_s5_kernel, bsz=bsz, n_lat_chunks=n_lat_rows // bsz, n_ctx_chunks=n_ctx_rows // bsz),
        grid=(SSM_GROUPS,),
        in_specs=[gspec(n_lat_rows, width), gspec(n_ctx_rows, width), gspec(width, 2 * width),
                  gspec(width, width), gspec(2 * SSM_STATE, width), gspec(2 * SSM_STATE, width),
                  gspec(1, 2 * SSM_STATE), gspec(1, 2 * SSM_STATE)],
        out_specs=gspec(n_lat_rows, width),
        out_shape=jax.ShapeDtypeStruct((SSM_GROUPS, n_lat_rows, width), BF16),
        scratch_shapes=[pltpu.VMEM((n_lat_rows, width), F32), pltpu.VMEM((n_ctx_rows, width), F32),
                        pltpu.VMEM((n_lat_rows, 2 * SSM_STATE), F32), pltpu.VMEM((n_lat_rows, 2 * SSM_STATE), F32)],
        compiler_params=_cparams(("arbitrary",)),
        name="s5_scan",
    )(u_lat, u_ctx, toep, w_state, wo_re, wo_im, a_re, a_im)


def _to_chunks(u, bsz, seq):
    n_chunks = seq // S5_CHUNK
    u = u.reshape(bsz, n_chunks, S5_CHUNK, SSM_GROUPS, SSM_GROUP)
    return jnp.transpose(u, (3, 1, 0, 2, 4)).reshape(SSM_GROUPS, n_chunks * bsz, S5_CHUNK * SSM_GROUP)


def _from_chunks(y, bsz, seq):
    n_chunks = seq // S5_CHUNK
    y = y.reshape(SSM_GROUPS, n_chunks, bsz, S5_CHUNK, SSM_GROUP)
    return jnp.transpose(y, (2, 1, 3, 0, 4)).reshape(bsz * seq, SSM_GROUPS * SSM_GROUP)


def _moe_weights(w_gate_up, b_gate_up, w_down, b_down):
    return w_gate_up.astype(BF16), b_gate_up, w_down.astype(BF16), b_down


def kernel(x, c, ctx, c_ctx, l0_ada_w, l0_ada_b, l0_norm_mix, l0_w_in, l0_rpb, l0_w_fourier, l0_w_out, l0_norm_ffn, l0_router_w, l0_router_b, l0_w_gate_up, l0_b_gate_up, l0_w_down, l0_b_down, l1_ada_w, l1_ada_b, l1_norm_mix, l1_w_in, l1_lambda_re, l1_lambda_im, l1_log_dt, l1_b_re, l1_b_im, l1_c_re, l1_c_im, l1_d_skip, l1_w_glu, l1_b_glu, l1_w_out, l1_norm_ffn, l1_router_w, l1_router_b, l1_w_gate_up, l1_b_gate_up, l1_w_down, l1_b_down, final_norm):
    bsz, seq, _ = x.shape
    ctx_len = ctx.shape[1]
    n_lat = bsz * seq
    n_ctx = bsz * ctx_len
    n_tok = n_lat + n_ctx
    assert seq % TM == 0 and n_ctx % TM == 0 and seq % COMBINE_TM == 0
    x2 = x.reshape(n_lat, D_MODEL)
    ctx2 = ctx.reshape(n_ctx, D_MODEL)

    n_mod = bsz + 1
    c_pad = jnp.zeros((-(-n_mod // 8) * 8, D_MODEL), F32).at[:bsz].set(c).at[bsz].set(c_ctx)
    mod0 = _ada(c_pad, l0_ada_w, l0_ada_b)[:n_mod].reshape(n_mod * 6, 1, D_MODEL)
    mod1 = _ada(c_pad, l1_ada_w, l1_ada_b)[:n_mod].reshape(n_mod * 6, 1, D_MODEL)
    tpb = (seq // TM, bsz)
    ctpb = (seq // COMBINE_TM, bsz)

    q, k, v, f = _modproj([x2, ctx2], mod0, l0_norm_mix, l0_w_in.astype(BF16),
                          (NA_WIDTH, NA_WIDTH, NA_WIDTH, FOURIER_WIDTH), n_lat, n_tok, tpb,
                          q_scale=HEAD_DIM ** -0.5)
    a_lat, a_ctx = _attention(q, k, v, _na_bias_table(l0_rpb), bsz, seq, ctx_len)
    bdw = (jnp.eye(FOURIER_GROUPS, dtype=F32)[:, None, :, None] * l0_w_fourier[:, :, None, :]).reshape(
        FOURIER_WIDTH, FOURIER_WIDTH).astype(BF16)
    f_lat = _fourier(f, bdw, bsz, seq, 0, TM)
    f_ctx = _fourier(f, bdw, bsz, ctx_len, n_lat // ctx_len, ctx_len)
    x_all, h2, m_sel, gates = _post_l0(a_lat, a_ctx, f_lat, f_ctx, x2, ctx2, mod0, l0_norm_ffn,
                                       l0_w_out.astype(BF16), _router_args(l0_router_w, l0_router_b),
                                       n_lat, n_tok, tpb)
    x_all = _moe(h2, m_sel, gates, x_all, mod0, _moe_weights(l0_w_gate_up, l0_b_gate_up, l0_w_down, l0_b_down),
                 ctpb, final_norm, False)

    (u,) = _modproj([x_all], mod1, l1_norm_mix, l1_w_in.astype(BF16), (D_MODEL,), n_lat, n_tok, tpb)
    tables = _s5_tables(l1_lambda_re, l1_lambda_im, l1_log_dt, l1_b_re, l1_b_im, l1_c_re, l1_c_im, l1_d_skip)
    y = _s5(_to_chunks(u[:n_lat], bsz, seq), _to_chunks(u[n_lat:], bsz, ctx_len), tables, bsz)
    gy = _from_chunks(y, bsz, seq)
    x1, h2, m_sel, gates = _post_l1(gy, x_all, mod1, l1_norm_ffn, l1_w_glu.astype(BF16), l1_b_glu,
                                    l1_w_out.astype(BF16), _router_args(l1_router_w, l1_router_b), n_lat, tpb)
    out = _moe(h2, m_sel, gates, x1, mod1, _moe_weights(l1_w_gate_up, l1_b_gate_up, l1_w_down, l1_b_down),
               ctpb, final_norm, True)
    return out.reshape(bsz, seq, D_MODEL)
```

```python
import functools
import math

import numpy as np
import jax
import jax.numpy as jnp
from jax import lax
from jax.experimental import pallas as pl
from jax.experimental.pallas import tpu as pltpu

F32 = jnp.float32
BF16 = jnp.bfloat16

D_MODEL = 1024
GRID_W = 64
HEAD_DIM = 64
NA_WIDTH = 512
NA_HEADS = 8
MAX_KR = 8
KC = 16
FOURIER_DIM = 64
FOURIER_GROUPS = 8
FOURIER_WIDTH = 512
SSM_GROUP = 16
SSM_GROUPS = 64
SSM_STATE = 64
LAMBDA_RE_MAX = -1e-4
N_EXPERTS = 32
TOP_K = 4
SWIGLU_ALPHA = 1.702
SWIGLU_LIMIT = 7.0
EPS = 1e-6

TM = 512
EXPERT_ROWS = 512
COMBINE_TM = 256
S5_Q = 128
VMEM_LIMIT = 48 * 1024 * 1024
EXPERT_VMEM_LIMIT = 58 * 1024 * 1024


def _cparams(sem):
    return pltpu.CompilerParams(dimension_semantics=sem, vmem_limit_bytes=VMEM_LIMIT)


def _split_bf16(a):
    hi = a.astype(BF16)
    lo = (a - hi.astype(F32)).astype(BF16)
    return hi, lo


def _dot(a, b):
    return jnp.dot(a, b, preferred_element_type=F32)


def _dot_nt(a, b):
    return lax.dot_general(a, b, (((1,), (1,)), ((), ())), preferred_element_type=F32)


def _ada_kernel(c_ref, w_ref, b_ref, o_ref):
    c = c_ref[...]
    s = c * jax.nn.sigmoid(c)
    s_hi, s_lo = _split_bf16(s)
    w_hi, w_lo = _split_bf16(w_ref[...])
    o_ref[...] = _dot(s_hi, w_hi) + _dot(s_lo, w_hi) + _dot(s_hi, w_lo) + b_ref[...]


def _ada(c_pad, ada_w, ada_b):
    n = ada_w.shape[1]
    tn = 1024
    return pl.pallas_call(
        _ada_kernel,
        grid=(n // tn,),
        in_specs=[pl.BlockSpec((c_pad.shape[0], D_MODEL), lambda j: (0, 0)),
                  pl.BlockSpec((D_MODEL, tn), lambda j: (0, j)),
                  pl.BlockSpec((1, tn), lambda j: (0, j))],
        out_specs=pl.BlockSpec((c_pad.shape[0], tn), lambda j: (0, j)),
        out_shape=jax.ShapeDtypeStruct((c_pad.shape[0], n), F32),
        compiler_params=_cparams(("arbitrary",)),
        name="ada_mod",
    )(c_pad, ada_w, ada_b.reshape(1, n))


def _mod_spec(which, tiles_per_batch):
    tiles, bsz = tiles_per_batch
    return pl.BlockSpec((1, 1, D_MODEL), lambda i: (jnp.minimum(i // tiles, bsz) * 6 + which, 0, 0))


def _modulate(x, g, sh, sc):
    ms = jnp.mean(x * x, axis=-1, keepdims=True)
    y = x * lax.rsqrt(ms + EPS) * g
    return y * (1.0 + sc) + sh


def _modproj_kernel(*refs, n_lat_tiles, two_src, widths, q_scale):
    if two_src:
        x_ref, c_ref, sh_ref, sc_ref, g_ref, w_ref = refs[:6]
        o_refs = refs[6:]
        i = pl.program_id(0)
        x = jnp.where(i < n_lat_tiles, x_ref[...], c_ref[...])
    else:
        x_ref, sh_ref, sc_ref, g_ref, w_ref = refs[:5]
        o_refs = refs[5:]
        x = x_ref[...]
    h = _modulate(x, g_ref[...], sh_ref[0], sc_ref[0]).astype(BF16)
    off = 0
    for j, (o_ref, wd) in enumerate(zip(o_refs, widths)):
        y = _dot(h, w_ref[:, off:off + wd])
        if j == 0 and q_scale != 1.0:
            y = y * q_scale
        o_ref[...] = y.astype(o_ref.dtype)
        off += wd


def _modproj(x_srcs, mod, norm_g, w_bf, widths, n_lat, n_tok, tiles_per_batch, q_scale=1.0):
    n_tiles = n_tok // TM
    n_lat_tiles = n_lat // TM
    two_src = len(x_srcs) == 2
    n_out = w_bf.shape[1]
    if two_src:
        x_specs = [pl.BlockSpec((TM, D_MODEL), lambda i: (jnp.minimum(i, n_lat_tiles - 1), 0)),
                   pl.BlockSpec((TM, D_MODEL), lambda i: (jnp.maximum(i - n_lat_tiles, 0), 0))]
    else:
        x_specs = [pl.BlockSpec((TM, D_MODEL), lambda i: (i, 0))]
    return pl.pallas_call(
        functools.partial(_modproj_kernel, n_lat_tiles=n_lat_tiles, two_src=two_src,
                          widths=tuple(widths), q_scale=q_scale),
        grid=(n_tiles,),
        in_specs=x_specs + [_mod_spec(0, tiles_per_batch), _mod_spec(1, tiles_per_batch),
                            pl.BlockSpec((1, D_MODEL), lambda i: (0, 0)),
                            pl.BlockSpec((D_MODEL, n_out), lambda i: (0, 0))],
        out_specs=[pl.BlockSpec((TM, wd), lambda i: (i, 0)) for wd in widths],
        out_shape=[jax.ShapeDtypeStruct((n_tok, wd), BF16) for wd in widths],
        compiler_params=_cparams(("arbitrary",)),
        name="modulate_in_proj",
    )(*x_srcs, mod, mod, norm_g.reshape(1, D_MODEL), w_bf)


def _modproj_t_kernel(x_ref, sh_ref, sc_ref, g_ref, wt_ref, o_ref):
    h = _modulate(x_ref[...], g_ref[...], sh_ref[0], sc_ref[0]).astype(BF16)
    o_ref[...] = _dot_nt(wt_ref[...], h).astype(o_ref.dtype)


def _modproj_t(x_all, mod, norm_g, wt_bf, n_tok, tiles_per_batch):
    n_out = wt_bf.shape[0]
    return pl.pallas_call(
        _modproj_t_kernel,
        grid=(n_tok // TM,),
        in_specs=[pl.BlockSpec((TM, D_MODEL), lambda i: (i, 0)),
                  _mod_spec(0, tiles_per_batch), _mod_spec(1, tiles_per_batch),
                  pl.BlockSpec((1, D_MODEL), lambda i: (0, 0)),
                  pl.BlockSpec((n_out, D_MODEL), lambda i: (0, 0))],
        out_specs=pl.BlockSpec((n_out, TM), lambda i: (0, i)),
        out_shape=jax.ShapeDtypeStruct((n_out, n_tok), BF16),
        compiler_params=_cparams(("arbitrary",)),
        name="modulate_in_proj_t",
    )(x_all, mod, mod, norm_g.reshape(1, D_MODEL), wt_bf)


Q_ROWS = 4
BAND_ROWS = Q_ROWS + MAX_KR


def _na_bias_table(rpb, rows):
    qc = np.arange(GRID_W)[:, None]
    kc = np.arange(GRID_W)[None, :]
    win0 = np.clip(qc - KC // 2, 0, GRID_W - KC)
    col_valid = (kc >= win0) & (kc < win0 + KC)
    col_off = np.clip(kc - qc, -(KC - 1), KC - 1) + KC - 1
    cb = rpb.astype(F32)[:, :, col_off]
    cb = jnp.where(jnp.asarray(col_valid)[None, None], cb, -jnp.inf)
    n_blk = rows // Q_ROWS
    idx = np.zeros((3, Q_ROWS, BAND_ROWS), np.int64)
    ok = np.zeros((3, Q_ROWS, BAND_ROWS), bool)
    for cls, blk in enumerate((0, 1, n_blk - 1)):
        band0 = int(np.clip(blk * Q_ROWS - MAX_KR // 2, 0, rows - BAND_ROWS))
        for rho in range(Q_ROWS):
            r = blk * Q_ROWS + rho
            r0 = int(np.clip(r - MAX_KR // 2, 0, rows - MAX_KR))
            for kap in range(BAND_ROWS):
                kr = band0 + kap
                ok[cls, rho, kap] = r0 <= kr < r0 + MAX_KR
                idx[cls, rho, kap] = np.clip(kr - r + MAX_KR - 1, 0, 2 * MAX_KR - 2)
    t = cb[:, idx]
    t = jnp.where(jnp.asarray(ok)[None, :, :, :, None, None], t, -jnp.inf)
    t = jnp.transpose(t, (1, 0, 2, 4, 3, 5))
    return t.reshape(3, NA_HEADS, Q_ROWS * GRID_W, BAND_ROWS * GRID_W)


def _attn_kernel(q_ref, k_ref, v_ref, qc_ref, kc_ref, vc_ref, bias_ref, o_ref, oc_ref, *, rows):
    lane = lax.broadcasted_iota(jnp.int32, (1, 2 * HEAD_DIM), 1)
    first = lane < HEAD_DIM
    kc = kc_ref[...]
    vc = vc_ref[...]
    zero = jnp.zeros((), BF16)

    def softmax_pv(s_list, v_list):
        m = s_list[0].max(axis=-1, keepdims=True)
        for s in s_list[1:]:
            m = jnp.maximum(m, s.max(axis=-1, keepdims=True))
        den = None
        acc = None
        for s, vv in zip(s_list, v_list):
            p = jnp.exp(s - m)
            ps = p.sum(axis=-1, keepdims=True)
            den = ps if den is None else den + ps
            pv = _dot(p.astype(BF16), vv)
            acc = pv if acc is None else acc + pv
        return acc / den

    n_blk = rows // Q_ROWS
    q_len = Q_ROWS * GRID_W

    def block_body(i, carry):
        band0 = jnp.clip(i * Q_ROWS - MAX_KR // 2, 0, rows - BAND_ROWS)
        cls = jnp.where(i == 0, 0, jnp.where(i == n_blk - 1, 2, 1))
        rows_q = pl.ds(pl.multiple_of(i * q_len, q_len), q_len)
        q_b = q_ref[rows_q, :]
        band = pl.ds(pl.multiple_of(band0 * GRID_W, GRID_W), BAND_ROWS * GRID_W)
        kb = k_ref[band, :]
        vb = v_ref[band, :]
        outs = []
        for hh in range(2):
            q_h = jnp.where(first if hh == 0 else ~first, q_b, zero)
            s_w = _dot_nt(q_h, kb) + bias_ref[cls, hh]
            s_c = _dot_nt(q_h, kc)
            outs.append(softmax_pv([s_w, s_c], [vb, vc]))
        o_ref[rows_q, :] = jnp.where(first, outs[0], outs[1]).astype(o_ref.dtype)
        return carry

    lax.fori_loop(0, n_blk, block_body, 0)

    q_c = qc_ref[...]
    outs = []
    for hh in range(2):
        q_h = jnp.where(first if hh == 0 else ~first, q_c, zero)
        outs.append(softmax_pv([_dot_nt(q_h, kc)], [vc]))
    oc_ref[...] = jnp.where(first, outs[0], outs[1]).astype(oc_ref.dtype)


def _attention(q, k, v, rpb, bsz, seq, ctx_len):
    rows = seq // GRID_W
    assert rows % Q_ROWS == 0 and rows >= 3 * Q_ROWS and seq % ctx_len == 0
    bias = _na_bias_table(rpb, rows)
    cb0 = bsz * seq // ctx_len
    lat = pl.BlockSpec((seq, 2 * HEAD_DIM), lambda hp, b: (b, hp))
    ctx = pl.BlockSpec((ctx_len, 2 * HEAD_DIM), lambda hp, b: (cb0 + b, hp))
    return pl.pallas_call(
        functools.partial(_attn_kernel, rows=rows),
        grid=(NA_HEADS // 2, bsz),
        in_specs=[lat, lat, lat, ctx, ctx, ctx,
                  pl.BlockSpec((3, 2, Q_ROWS * GRID_W, BAND_ROWS * GRID_W), lambda hp, b: (0, hp, 0, 0))],
        out_specs=[pl.BlockSpec((seq, 2 * HEAD_DIM), lambda hp, b: (b, hp)),
                   pl.BlockSpec((ctx_len, 2 * HEAD_DIM), lambda hp, b: (b, hp))],
        out_shape=[jax.ShapeDtypeStruct((bsz * seq, NA_WIDTH), BF16),
                   jax.ShapeDtypeStruct((bsz * ctx_len, NA_WIDTH), BF16)],
        compiler_params=_cparams(("arbitrary", "arbitrary")),
        name="na_attention",
    )(q, k, v, q, k, v, bias)


@functools.lru_cache(maxsize=None)
def _dft_tables(n):
    j = np.arange(n, dtype=np.int64)
    ang = 2.0 * np.pi * ((j[:, None] * j[None, :]) % n).astype(np.float64) / n
    return np.cos(ang).astype(BF16), np.sin(ang).astype(BF16)


def _fourier_kernel(cos_ref, sin_ref, f_ref, bdc_ref, bds_ref, bdw_ref, o_ref, *, scale):
    fb = f_ref[...]
    zr = _dot(cos_ref[...], fb)
    zs = _dot(sin_ref[...], fb)
    fr = (_dot(zr.astype(BF16), bdc_ref[...]) - _dot(zs.astype(BF16), bds_ref[...])) * scale
    o_ref[...] = _dot(fr.astype(BF16), bdw_ref[...]).astype(o_ref.dtype)


def _fourier(f_all, bdw, bsz, n, row_block0, tm):
    cos_n, sin_n = _dft_tables(n)
    cos64, sin64 = _dft_tables(FOURIER_DIM)
    eye = np.eye(FOURIER_GROUPS)
    bdc = jnp.asarray(np.kron(eye, cos64.astype(np.float32)), BF16)
    bds = jnp.asarray(np.kron(eye, sin64.astype(np.float32)), BF16)
    mt = n // tm
    small = pl.BlockSpec((FOURIER_WIDTH, FOURIER_WIDTH), lambda b, m: (0, 0))
    return pl.pallas_call(
        functools.partial(_fourier_kernel, scale=1.0 / math.sqrt(n * FOURIER_DIM)),
        grid=(bsz, mt),
        in_specs=[pl.BlockSpec((tm, n), lambda b, m: (m, 0)),
                  pl.BlockSpec((tm, n), lambda b, m: (m, 0)),
                  pl.BlockSpec((n, FOURIER_WIDTH), lambda b, m: (row_block0 + b, 0)),
                  small, small, small],
        out_specs=pl.BlockSpec((tm, FOURIER_WIDTH), lambda b, m: (b * mt + m, 0)),
        out_shape=jax.ShapeDtypeStruct((bsz * n, FOURIER_WIDTH), BF16),
        compiler_params=_cparams(("arbitrary", "arbitrary")),
        name="fourier_mix",
    )(jnp.asarray(cos_n), jnp.asarray(sin_n), f_all, bdc, bds, bdw)


def _route(h2, rw_hi_ref, rw_lo_ref, rb_ref, m_ref, g_ref):
    h_hi, h_lo = _split_bf16(h2)
    logits = (_dot(h_hi, rw_hi_ref[...]) + _dot(h_lo, rw_hi_ref[...]) + _dot(h_hi, rw_lo_ref[...])
              + rb_ref[...])
    lane = lax.broadcasted_iota(jnp.int32, logits.shape, 1).astype(F32)
    work = logits
    sel_any = jnp.zeros(logits.shape, jnp.bool_)
    top = None
    for _ in range(TOP_K):
        m = work.max(axis=-1, keepdims=True)
        if top is None:
            top = m
        idx = jnp.where(work == m, lane, float(N_EXPERTS)).min(axis=-1, keepdims=True)
        sel = lane == idx
        sel_any = sel_any | sel
        work = jnp.where(sel, -jnp.inf, work)
    e = jnp.where(sel_any, jnp.exp(logits - top), 0.0)
    g_ref[...] = e / e.sum(axis=-1, keepdims=True)
    m_ref[...] = sel_any.astype(F32).astype(m_ref.dtype)


def _post_l0_kernel(a_ref, ac_ref, f_ref, fc_ref, x_ref, c_ref, g1_ref, sh_ref, sc_ref, ng_ref, w_ref,
                    rwh_ref, rwl_ref, rb_ref, xo_ref, h_ref, m_ref, g_ref, *, n_lat_tiles):
    lat = pl.program_id(0) < n_lat_tiles
    a = jnp.where(lat, a_ref[...], ac_ref[...])
    f = jnp.where(lat, f_ref[...], fc_ref[...])
    x = jnp.where(lat, x_ref[...], c_ref[...])
    y = _dot(a, w_ref[:NA_WIDTH, :]) + _dot(f, w_ref[NA_WIDTH:, :])
    x = x + g1_ref[0] * y
    xo_ref[...] = x
    h2 = _modulate(x, ng_ref[...], sh_ref[0], sc_ref[0])
    h_ref[...] = h2
    _route(h2, rwh_ref, rwl_ref, rb_ref, m_ref, g_ref)


def _post_l1_kernel(yt_ref, x_ref, g1_ref, sh_ref, sc_ref, ng_ref, wgt_ref, bg_ref, wot_ref,
                    rwh_ref, rwl_ref, rb_ref, xo_ref, h_ref, m_ref, g_ref):
    gyt = yt_ref[...]
    zt = _dot(wgt_ref[...], gyt) + bg_ref[...]
    vt = (gyt.astype(F32) * jax.nn.sigmoid(zt)).astype(BF16)
    x = x_ref[...] + g1_ref[0] * _dot(wot_ref[...], vt).T
    xo_ref[...] = x
    h2 = _modulate(x, ng_ref[...], sh_ref[0], sc_ref[0])
    h_ref[...] = h2
    _route(h2, rwh_ref, rwl_ref, rb_ref, m_ref, g_ref)


def _router_args(router_w, router_b):
    rw_hi = router_w.astype(BF16)
    rw_lo = (router_w - rw_hi.astype(F32)).astype(BF16)
    return rw_hi, rw_lo, router_b.reshape(1, N_EXPERTS).astype(F32)


def _post_out(n_tok):
    specs = [pl.BlockSpec((TM, D_MODEL), lambda i: (i, 0)),
             pl.BlockSpec((TM, D_MODEL), lambda i: (i, 0)),
             pl.BlockSpec((TM, N_EXPERTS), lambda i: (i, 0)),
             pl.BlockSpec((TM, N_EXPERTS), lambda i: (i, 0))]
    shapes = [jax.ShapeDtypeStruct((n_tok, D_MODEL), F32),
              jax.ShapeDtypeStruct((n_tok, D_MODEL), F32),
              jax.ShapeDtypeStruct((n_tok, N_EXPERTS), BF16),
              jax.ShapeDtypeStruct((n_tok, N_EXPERTS), F32)]
    return specs, shapes


def _const_spec(shape):
    return pl.BlockSpec(shape, lambda i: tuple(0 for _ in shape))


def _post_l0(a_lat, a_ctx, f_lat, f_ctx, x, ctx, mod, norm_g, w_out_bf, router, n_lat, n_tok, tiles_per_batch):
    n_lat_tiles = n_lat // TM

    def lat(wd):
        return pl.BlockSpec((TM, wd), lambda i: (jnp.minimum(i, n_lat_tiles - 1), 0))

    def cx(wd):
        return pl.BlockSpec((TM, wd), lambda i: (jnp.maximum(i - n_lat_tiles, 0), 0))

    out_specs, out_shapes = _post_out(n_tok)
    return pl.pallas_call(
        functools.partial(_post_l0_kernel, n_lat_tiles=n_lat_tiles),
        grid=(n_tok // TM,),
        in_specs=[lat(NA_WIDTH), cx(NA_WIDTH), lat(FOURIER_WIDTH), cx(FOURIER_WIDTH),
                  lat(D_MODEL), cx(D_MODEL),
                  _mod_spec(2, tiles_per_batch), _mod_spec(3, tiles_per_batch), _mod_spec(4, tiles_per_batch),
                  _const_spec((1, D_MODEL)), _const_spec((D_MODEL, D_MODEL)),
                  _const_spec((D_MODEL, N_EXPERTS)), _const_spec((D_MODEL, N_EXPERTS)),
                  _const_spec((1, N_EXPERTS))],
        out_specs=out_specs, out_shape=out_shapes,
        compiler_params=_cparams(("arbitrary",)),
        name="post_mixer_l0",
    )(a_lat, a_ctx, f_lat, f_ctx, x, ctx, mod, mod, mod, norm_g.reshape(1, D_MODEL), w_out_bf, *router)


def _post_l1(gyt, x_all, mod, norm_g, w_glu_t_bf, b_glu, w_out_t_bf, router, n_lat, tiles_per_batch):
    out_specs, out_shapes = _post_out(n_lat)
    return pl.pallas_call(
        _post_l1_kernel,
        grid=(n_lat // TM,),
        in_specs=[pl.BlockSpec((D_MODEL, TM), lambda i: (0, i)),
                  pl.BlockSpec((TM, D_MODEL), lambda i: (i, 0)),
                  _mod_spec(2, tiles_per_batch), _mod_spec(3, tiles_per_batch), _mod_spec(4, tiles_per_batch),
                  _const_spec((1, D_MODEL)), _const_spec((D_MODEL, D_MODEL)), _const_spec((D_MODEL, 1)),
                  _const_spec((D_MODEL, D_MODEL)),
                  _const_spec((D_MODEL, N_EXPERTS)), _const_spec((D_MODEL, N_EXPERTS)),
                  _const_spec((1, N_EXPERTS))],
        out_specs=out_specs, out_shape=out_shapes,
        compiler_params=_cparams(("arbitrary",)),
        name="post_mixer_l1",
    )(gyt, x_all, mod, mod, mod, norm_g.reshape(1, D_MODEL), w_glu_t_bf, b_glu.reshape(D_MODEL, 1),
      w_out_t_bf, *router)


def _rank_kernel(m_ref, rank_ref, cnt_ref, base_ref):
    @pl.when(pl.program_id(0) == 0)
    def _():
        base_ref[...] = jnp.zeros_like(base_ref)

    m = m_ref[...]
    r = lax.broadcasted_iota(jnp.int32, (TM, TM), 0)
    c = lax.broadcasted_iota(jnp.int32, (TM, TM), 1)
    tri = (r > c).astype(F32).astype(BF16)
    rank_ref[...] = _dot(tri, m) + base_ref[...]
    base_ref[...] = base_ref[...] + m.astype(F32).sum(axis=0, keepdims=True)
    cnt_ref[...] = base_ref[...]


def _rank(m_sel):
    n_tok = m_sel.shape[0]
    return pl.pallas_call(
        _rank_kernel,
        grid=(n_tok // TM,),
        in_specs=[pl.BlockSpec((TM, N_EXPERTS), lambda i: (i, 0))],
        out_specs=[pl.BlockSpec((TM, N_EXPERTS), lambda i: (i, 0)),
                   pl.BlockSpec((1, N_EXPERTS), lambda i: (0, 0))],
        out_shape=[jax.ShapeDtypeStruct((n_tok, N_EXPERTS), F32),
                   jax.ShapeDtypeStruct((1, N_EXPERTS), F32)],
        scratch_shapes=[pltpu.VMEM((1, N_EXPERTS), F32)],
        compiler_params=_cparams(("arbitrary",)),
        name="moe_rank",
    )(m_sel)


def _dest_kernel(rank_ref, m_ref, g_ref, ps_ref, d_ref, gate_ref):
    m = m_ref[...]
    dest = rank_ref[...] + ps_ref[...]
    r = lax.broadcasted_iota(jnp.int32, (N_EXPERTS, N_EXPERTS), 0)
    c = lax.broadcasted_iota(jnp.int32, (N_EXPERTS, N_EXPERTS), 1)
    upper = (r < c).astype(F32).astype(BF16)
    slot = _dot(m, upper)
    chosen = m.astype(F32) > 0.5
    g = g_ref[...]
    lane = lax.broadcasted_iota(jnp.int32, d_ref.shape, 1)
    d_out = jnp.zeros(d_ref.shape, F32)
    g_out = jnp.zeros(gate_ref.shape, F32)
    for k in range(TOP_K):
        sel = chosen & (slot == float(k))
        dk = jnp.where(sel, dest, 0.0).sum(axis=-1, keepdims=True)
        gk = jnp.where(sel, g, 0.0).sum(axis=-1, keepdims=True)
        d_out = jnp.where(lane == k, dk, d_out)
        g_out = jnp.where(lane == k, gk, g_out)
    d_ref[...] = d_out.astype(jnp.int32)
    gate_ref[...] = g_out


def _dest(rank, m_sel, gates, pstart):
    n_tok = m_sel.shape[0]
    tok = pl.BlockSpec((TM, N_EXPERTS), lambda i: (i, 0))
    wide = pl.BlockSpec((TM, 128), lambda i: (i, 0))
    return pl.pallas_call(
        _dest_kernel,
        grid=(n_tok // TM,),
        in_specs=[tok, tok, tok, _const_spec((1, N_EXPERTS))],
        out_specs=[wide, wide],
        out_shape=[jax.ShapeDtypeStruct((n_tok, 128), jnp.int32),
                   jax.ShapeDtypeStruct((n_tok, 128), F32)],
        compiler_params=_cparams(("arbitrary",)),
        name="moe_dest",
    )(rank, m_sel, gates, pstart)


def _row_copy(src_ref, src_row, dst_ref, dst_row, sem):
    return pltpu.make_async_copy(src_ref.at[pl.ds(src_row, 1), :], dst_ref.at[pl.ds(dst_row, 1), :], sem)


def _dispatch_kernel(cnt_ref, ps_ref, d_ref, h_ref, xs_ref, zero_ref, sem):
    def issue(i, carry):
        for k in range(TOP_K):
            _row_copy(h_ref, i, xs_ref, d_ref[i * TOP_K + k], sem).start()
        return carry

    def drain(i, carry):
        for k in range(TOP_K):
            _row_copy(h_ref, 0, xs_ref, 0, sem).wait()
        return carry

    lax.fori_loop(0, TM, issue, 0)
    lax.fori_loop(0, TM, drain, 0)

    @pl.when(pl.program_id(0) == pl.num_programs(0) - 1)
    def _():
        zero_ref[...] = jnp.zeros_like(zero_ref)
        for e in range(N_EXPERTS):
            pad = (-cnt_ref[e]) % EXPERT_ROWS
            first = ps_ref[e] + cnt_ref[e]

            def fill(r, carry):
                _row_copy(zero_ref, 0, xs_ref, first + r, sem).start()
                return carry

            def fill_wait(r, carry):
                _row_copy(zero_ref, 0, xs_ref, 0, sem).wait()
                return carry

            lax.fori_loop(0, pad, fill, 0)
            lax.fori_loop(0, pad, fill_wait, 0)


def _dispatch(h2, dest_flat, cnt_i, pstart_i, cap):
    n_tok = h2.shape[0]
    grid_spec = pltpu.PrefetchScalarGridSpec(
        num_scalar_prefetch=2,
        grid=(n_tok // TM,),
        in_specs=[pl.BlockSpec((TM * TOP_K,), lambda i, *_: (i,), memory_space=pltpu.SMEM),
                  pl.BlockSpec((TM, D_MODEL), lambda i, *_: (i, 0))],
        out_specs=pl.BlockSpec(memory_space=pl.ANY),
        scratch_shapes=[pltpu.VMEM((8, D_MODEL), F32), pltpu.SemaphoreType.DMA(())],
    )
    return pl.pallas_call(
        _dispatch_kernel,
        grid_spec=grid_spec,
        out_shape=jax.ShapeDtypeStruct((cap, D_MODEL), F32),
        compiler_params=_cparams(("arbitrary",)),
        name="moe_dispatch",
    )(cnt_i, pstart_i, dest_flat, h2)


def _expert_kernel(be_ref, nu_ref, x_ref, wgu_ref, bgu_ref, wd_ref, bd_ref, o_ref, wgu_bf, wd_bf):
    i = pl.program_id(0)

    @pl.when((i == 0) | (be_ref[i] != be_ref[jnp.maximum(i - 1, 0)]))
    def _():
        wgu_bf[...] = wgu_ref[0].astype(BF16)
        wd_bf[...] = wd_ref[0].astype(BF16)

    @pl.when(i < nu_ref[0])
    def _():
        xb = x_ref[...].astype(BF16)
        gu = _dot(xb, wgu_bf[...]) + bgu_ref[0]
        gate = jnp.minimum(gu[:, :D_MODEL], SWIGLU_LIMIT)
        up = jnp.clip(gu[:, D_MODEL:], -SWIGLU_LIMIT, SWIGLU_LIMIT)
        act = (up + 1.0) * (gate * jax.nn.sigmoid(SWIGLU_ALPHA * gate))
        o_ref[...] = _dot(act.astype(BF16), wd_bf[...]) + bd_ref[0]


def _experts(xs, block_expert, n_used, wgu, bgu, wd, bd):
    n_blocks = xs.shape[0] // EXPERT_ROWS

    def row_map(i, be, nu):
        return (jnp.minimum(i, nu[0] - 1), 0)

    def exp_map(i, be, nu):
        return (be[i], 0, 0)

    grid_spec = pltpu.PrefetchScalarGridSpec(
        num_scalar_prefetch=2,
        grid=(n_blocks,),
        in_specs=[pl.BlockSpec((EXPERT_ROWS, D_MODEL), row_map),
                  pl.BlockSpec((1, D_MODEL, 2 * D_MODEL), exp_map),
                  pl.BlockSpec((1, 1, 2 * D_MODEL), exp_map),
                  pl.BlockSpec((1, D_MODEL, D_MODEL), exp_map),
                  pl.BlockSpec((1, 1, D_MODEL), exp_map)],
        out_specs=pl.BlockSpec((EXPERT_ROWS, D_MODEL), row_map),
        scratch_shapes=[pltpu.VMEM((D_MODEL, 2 * D_MODEL), BF16), pltpu.VMEM((D_MODEL, D_MODEL), BF16)],
    )
    return pl.pallas_call(
        _expert_kernel,
        grid_spec=grid_spec,
        out_shape=jax.ShapeDtypeStruct(xs.shape, F32),
        compiler_params=pltpu.CompilerParams(dimension_semantics=("arbitrary",),
                                             vmem_limit_bytes=EXPERT_VMEM_LIMIT),
        name="moe_experts",
    )(block_expert, n_used, xs, wgu, bgu.reshape(N_EXPERTS, 1, 2 * D_MODEL), wd,
      bd.reshape(N_EXPERTS, 1, D_MODEL))


def _combine_kernel(d_ref, x_ref, gate_ref, g2_ref, ng_ref, ys_ref, o_ref, buf_ref, sem, *, final_norm):
    def issue(i, carry):
        for k in range(TOP_K):
            pltpu.make_async_copy(ys_ref.at[pl.ds(d_ref[i * TOP_K + k], 1), :],
                                  buf_ref.at[k, pl.ds(i, 1), :], sem).start()
        return carry

    def drain(i, carry):
        for k in range(TOP_K):
            pltpu.make_async_copy(ys_ref.at[pl.ds(0, 1), :], buf_ref.at[k, pl.ds(0, 1), :], sem).wait()
        return carry

    lax.fori_loop(0, COMBINE_TM, issue, 0)
    lax.fori_loop(0, COMBINE_TM, drain, 0)
    gate = gate_ref[...]
    acc = gate[:, 0:1] * buf_ref[0]
    for k in range(1, TOP_K):
        acc = acc + gate[:, k:k + 1] * buf_ref[k]
    x = x_ref[...] + g2_ref[0] * acc
    if final_norm:
        ms = jnp.mean(x * x, axis=-1, keepdims=True)
        x = x * lax.rsqrt(ms + EPS) * ng_ref[...]
    o_ref[...] = x


def _combine(ys, dest_flat, gate4, x_res, mod, norm_g, tiles_per_batch, final_norm):
    n_tok = x_res.shape[0]
    tok = pl.BlockSpec((COMBINE_TM, D_MODEL), lambda i: (i, 0))
    return pl.pallas_call(
        functools.partial(_combine_kernel, final_norm=final_norm),
        grid=(n_tok // COMBINE_TM,),
        in_specs=[pl.BlockSpec((COMBINE_TM * TOP_K,), lambda i: (i,), memory_space=pltpu.SMEM),
                  tok,
                  pl.BlockSpec((COMBINE_TM, 128), lambda i: (i, 0)),
                  _mod_spec(5, tiles_per_batch),
                  _const_spec((1, D_MODEL)),
                  pl.BlockSpec(memory_space=pl.ANY)],
        out_specs=tok,
        out_shape=jax.ShapeDtypeStruct((n_tok, D_MODEL), F32),
        scratch_shapes=[pltpu.VMEM((TOP_K, COMBINE_TM, D_MODEL), F32), pltpu.SemaphoreType.DMA(())],
        compiler_params=_cparams(("arbitrary",)),
        name="moe_combine",
    )(dest_flat, x_res, gate4, mod, norm_g.reshape(1, D_MODEL), ys)


def _moe(h2, m_sel, gates, x_res, mod, moe_w, combine_tiles_per_batch, norm_g, final_norm):
    wgu_bf, bgu, wd_bf, bd = moe_w
    n_tok = h2.shape[0]
    n_blocks = -(-n_tok * TOP_K // EXPERT_ROWS) + N_EXPERTS
    cap = n_blocks * EXPERT_ROWS
    rank, cnt = _rank(m_sel)
    cnt_i = cnt[0].astype(jnp.int32)
    nblk = (cnt_i + EXPERT_ROWS - 1) // EXPERT_ROWS
    bend = jnp.cumsum(nblk)
    pstart_i = (bend - nblk) * EXPERT_ROWS
    n_used = jnp.maximum(bend[-1], 1)
    blk = jnp.arange(n_blocks, dtype=jnp.int32)
    block_expert = (bend[None, :] <= jnp.minimum(blk, n_used - 1)[:, None]).sum(axis=1).astype(jnp.int32)
    block_expert = jnp.minimum(block_expert, N_EXPERTS - 1)
    dest4, gate4 = _dest(rank, m_sel, gates, pstart_i.astype(F32).reshape(1, N_EXPERTS))
    dest_flat = dest4[:, :TOP_K].reshape(-1)
    xs = _dispatch(h2, dest_flat, cnt_i, pstart_i, cap)
    ys = _experts(xs, block_expert, n_used.reshape(1).astype(jnp.int32), wgu_bf, bgu, wd_bf, bd)
    return _combine(ys, dest_flat, gate4, x_res, mod, norm_g, combine_tiles_per_batch, final_norm)


def _s5_tables(lam_re, lam_im, log_dt, b_re, b_im, c_re, c_im, d_skip):
    q = S5_Q
    lam = lax.complex(jnp.minimum(lam_re.astype(F32), LAMBDA_RE_MAX), lam_im.astype(F32))
    dt = jnp.exp(log_dt.astype(F32))[..., None]
    a = jnp.exp(lam * dt)
    bbar = ((a - 1) / lam)[..., None] * lax.complex(b_re.astype(F32), b_im.astype(F32))
    cc = lax.complex(c_re.astype(F32), c_im.astype(F32))
    steps = jnp.arange(q + 1, dtype=F32)
    pw = jnp.exp((lam * dt)[:, :, None, :] * steps[None, None, :, None])
    kern = jnp.einsum('dgop,dgkp,dgpc->dgcok', cc, pw[:, :, :q], bbar).real
    eye = jnp.eye(SSM_GROUP, dtype=F32)
    k_fwd = kern[0].at[..., 0].add(kern[1][..., 0] + d_skip.astype(F32)[:, :, None] * eye[None])
    k_bwd = jnp.concatenate([jnp.zeros_like(kern[1][..., :1]), kern[1][..., :0:-1]], axis=-1)
    k_fwd = k_fwd.reshape(SSM_GROUPS, SSM_GROUP * SSM_GROUP, q)
    k_bwd = k_bwd.reshape(SSM_GROUPS, SSM_GROUP * SSM_GROUP, q)
    ws = []
    for d in range(2):
        e = pw[d][:, ::-1][:, 1:] if d == 0 else pw[d][:, :q]
        w = e[:, None, :, :] * jnp.transpose(bbar[d], (0, 2, 1))[:, :, None, :]
        ws.append(w.reshape(SSM_GROUPS, SSM_GROUP * q, SSM_STATE))
    w_state = jnp.concatenate([ws[0].real, ws[1].real, ws[0].imag, ws[1].imag], axis=-1)
    wo = []
    for d in range(2):
        e = pw[d][:, 1:] if d == 0 else pw[d][:, ::-1][:, :q]
        w = cc[d][:, :, None, :] * e[:, None, :, :]
        wo.append(jnp.transpose(w, (0, 3, 1, 2)).reshape(SSM_GROUPS, SSM_STATE, SSM_GROUP * q))
    z = jnp.zeros_like(wo[0].real)
    wo_a = jnp.concatenate([wo[0].real, z, -wo[0].imag, z], axis=1)
    wo_b = jnp.concatenate([z, wo[1].real, z, -wo[1].imag], axis=1)
    a_q = pw[:, :, q]
    a_re = jnp.concatenate([a_q[0].real, a_q[1].real], axis=-1).reshape(SSM_GROUPS, 1, 2 * SSM_STATE)
    a_im = jnp.concatenate([a_q[0].imag, a_q[1].imag], axis=-1).reshape(SSM_GROUPS, 1, 2 * SSM_STATE)
    return k_fwd, k_bwd, w_state.astype(BF16), wo_a.astype(BF16), wo_b.astype(BF16), a_re, a_im


def _s5_kernel(u_ref, kf_ref, kb_ref, ws_ref, woa_ref, wob_ref, ar_ref, ai_ref, o_ref,
               w_ref, s_ref, h_ref, *, bsz, n_lat_chunks, n_ctx_chunks):
    q = S5_Q
    half = 2 * SSM_STATE
    n_lat = bsz * n_lat_chunks

    causal = (lax.broadcasted_iota(jnp.int32, (q, q), 1) >= lax.broadcasted_iota(jnp.int32, (q, q), 0))

    def build(c, carry):
        for o in range(SSM_GROUP):
            row = pl.ds(c * SSM_GROUP + o, 1)
            kf = pltpu.roll(jnp.broadcast_to(kf_ref[0, row, :], (q, q)), 0, 1, stride=1, stride_axis=0)
            kb = pltpu.roll(jnp.broadcast_to(kb_ref[0, row, :], (q, q)), 0, 1, stride=1, stride_axis=0)
            w_ref[pl.ds(pl.multiple_of(c * q, q), q), o * q:(o + 1) * q] = jnp.where(causal, kf, kb).astype(BF16)
        return carry

    lax.fori_loop(0, SSM_GROUP, build, 0)

    u = u_ref[...]
    ucat = jnp.concatenate([u[c] for c in range(SSM_GROUP)], axis=-1)
    s_all = _dot(ucat, ws_ref[0])
    s_ref[0] = s_all[:, :half]
    s_ref[1] = s_all[:, half:]

    fwd = lax.broadcasted_iota(jnp.int32, (bsz, half), 1) < SSM_STATE
    ar = jnp.broadcast_to(ar_ref[0], (bsz, half))
    ai = jnp.broadcast_to(ai_ref[0], (bsz, half))

    def chunk_rows(base, j, per_sample):
        return pl.ds(base + j, bsz, stride=per_sample)

    def load(base, j, n):
        rows_f = chunk_rows(base, j, n)
        rows_b = chunk_rows(base, n - 1 - j, n)
        return (jnp.where(fwd, s_ref[0, rows_f, :], s_ref[0, rows_b, :]),
                jnp.where(fwd, s_ref[1, rows_f, :], s_ref[1, rows_b, :]))

    def step(hr, hi, sr, si):
        return ar * hr - ai * hi + sr, ar * hi + ai * hr + si

    def ctx_body(j, carry):
        return step(*carry, *load(n_lat, j, n_ctx_chunks))

    def lat_body(j, carry):
        hr, hi = carry
        rows_f = chunk_rows(0, j, n_lat_chunks)
        rows_b = chunk_rows(0, n_lat_chunks - 1 - j, n_lat_chunks)
        h_ref[0, rows_f, :] = hr
        h_ref[1, rows_f, :] = hi
        h_ref[2, rows_b, :] = hr
        h_ref[3, rows_b, :] = hi
        return step(hr, hi, *load(0, j, n_lat_chunks))

    zero = jnp.zeros((bsz, half), F32)
    carry = lax.fori_loop(0, n_ctx_chunks, ctx_body, (zero, zero))
    lax.fori_loop(0, n_lat_chunks, lat_body, carry)

    y = _dot(ucat[:n_lat], w_ref[...])
    for part, wo_ref in ((0, woa_ref), (2, wob_ref)):
        y = y + _dot(h_ref[part].astype(BF16), wo_ref[0, :half, :])
        y = y + _dot(h_ref[part + 1].astype(BF16), wo_ref[0, half:, :])
    for o in range(SSM_GROUP):
        o_ref[o] = jax.nn.gelu(y[:, o * q:(o + 1) * q]).astype(o_ref.dtype)


def _s5(u3, tables, bsz, n_lat_chunks, n_ctx_chunks):
    k_fwd, k_bwd, w_state, wo_a, wo_b, a_re, a_im = tables
    q = S5_Q
    n_chunks = u3.shape[1]
    n_lat = bsz * n_lat_chunks
    assert n_chunks == n_lat + bsz * n_ctx_chunks

    def gspec(*shape):
        return pl.BlockSpec((1,) + shape, lambda g: (g,) + tuple(0 for _ in shape))

    return pl.pallas_call(
        functools.partial(_s5_kernel, bsz=bsz, n_lat_chunks=n_lat_chunks, n_ctx_chunks=n_ctx_chunks),
        grid=(SSM_GROUPS,),
        in_specs=[pl.BlockSpec((SSM_GROUP, n_chunks, q), lambda g: (g, 0, 0)),
                  gspec(SSM_GROUP * SSM_GROUP, q), gspec(SSM_GROUP * SSM_GROUP, q),
                  gspec(SSM_GROUP * q, 4 * SSM_STATE),
                  gspec(4 * SSM_STATE, SSM_GROUP * q), gspec(4 * SSM_STATE, SSM_GROUP * q),
                  gspec(1, 2 * SSM_STATE), gspec(1, 2 * SSM_STATE)],
        out_specs=pl.BlockSpec((SSM_GROUP, n_lat, q), lambda g: (g, 0, 0)),
        out_shape=jax.ShapeDtypeStruct((SSM_GROUPS * SSM_GROUP, n_lat, q), BF16),
        scratch_shapes=[pltpu.VMEM((SSM_GROUP * q, SSM_GROUP * q), BF16),
                        pltpu.VMEM((2, n_chunks, 2 * SSM_STATE), F32),
                        pltpu.VMEM((4, n_lat, 2 * SSM_STATE), F32)],
        compiler_params=_cparams(("arbitrary",)),
        name="s5_scan",
    )(u3, k_fwd, k_bwd, w_state, wo_a, wo_b, a_re, a_im)


def _moe_weights(w_gate_up, b_gate_up, w_down, b_down):
    return w_gate_up, b_gate_up, w_down, b_down


def kernel(x, c, ctx, c_ctx, l0_ada_w, l0_ada_b, l0_norm_mix, l0_w_in, l0_rpb, l0_w_fourier, l0_w_out, l0_norm_ffn, l0_router_w, l0_router_b, l0_w_gate_up, l0_b_gate_up, l0_w_down, l0_b_down, l1_ada_w, l1_ada_b, l1_norm_mix, l1_w_in, l1_lambda_re, l1_lambda_im, l1_log_dt, l1_b_re, l1_b_im, l1_c_re, l1_c_im, l1_d_skip, l1_w_glu, l1_b_glu, l1_w_out, l1_norm_ffn, l1_router_w, l1_router_b, l1_w_gate_up, l1_b_gate_up, l1_w_down, l1_b_down, final_norm):
    bsz, seq, _ = x.shape
    ctx_len = ctx.shape[1]
    n_lat = bsz * seq
    n_ctx = bsz * ctx_len
    n_tok = n_lat + n_ctx
    assert seq % TM == 0 and n_ctx % TM == 0 and seq % COMBINE_TM == 0
    x2 = x.reshape(n_lat, D_MODEL)
    ctx2 = ctx.reshape(n_ctx, D_MODEL)

    n_mod = bsz + 1
    c_pad = jnp.zeros((-(-n_mod // 8) * 8, D_MODEL), F32).at[:bsz].set(c).at[bsz].set(c_ctx)
    mod0 = _ada(c_pad, l0_ada_w, l0_ada_b)[:n_mod].reshape(n_mod * 6, 1, D_MODEL)
    mod1 = _ada(c_pad, l1_ada_w, l1_ada_b)[:n_mod].reshape(n_mod * 6, 1, D_MODEL)
    tpb = (seq // TM, bsz)
    ctpb = (seq // COMBINE_TM, bsz)

    q, k, v, f = _modproj([x2, ctx2], mod0, l0_norm_mix, l0_w_in.astype(BF16),
                          (NA_WIDTH, NA_WIDTH, NA_WIDTH, FOURIER_WIDTH), n_lat, n_tok, tpb,
                          q_scale=HEAD_DIM ** -0.5)
    a_lat, a_ctx = _attention(q, k, v, l0_rpb, bsz, seq, ctx_len)
    bdw = (jnp.eye(FOURIER_GROUPS, dtype=F32)[:, None, :, None] * l0_w_fourier[:, :, None, :]).reshape(
        FOURIER_WIDTH, FOURIER_WIDTH).astype(BF16)
    f_lat = _fourier(f, bdw, bsz, seq, 0, TM)
    f_ctx = _fourier(f, bdw, bsz, ctx_len, n_lat // ctx_len, ctx_len)
    x_all, h2, m_sel, gates = _post_l0(a_lat, a_ctx, f_lat, f_ctx, x2, ctx2, mod0, l0_norm_ffn,
                                       l0_w_out.astype(BF16), _router_args(l0_router_w, l0_router_b),
                                       n_lat, n_tok, tpb)
    x_all = _moe(h2, m_sel, gates, x_all, mod0, _moe_weights(l0_w_gate_up, l0_b_gate_up, l0_w_down, l0_b_down),
                 ctpb, final_norm, False)

    assert seq % S5_Q == 0 and ctx_len % S5_Q == 0
    ut = _modproj_t(x_all, mod1, l1_norm_mix, l1_w_in.T.astype(BF16), n_tok, tpb)
    tables = _s5_tables(l1_lambda_re, l1_lambda_im, l1_log_dt, l1_b_re, l1_b_im, l1_c_re, l1_c_im, l1_d_skip)
    yt = _s5(ut.reshape(D_MODEL, n_tok // S5_Q, S5_Q), tables, bsz, seq // S5_Q, ctx_len // S5_Q)
    gy = yt.reshape(D_MODEL, n_lat)
    x1, h2, m_sel, gates = _post_l1(gy, x_all, mod1, l1_norm_ffn, l1_w_glu.T.astype(BF16), l1_b_glu,
                                    l1_w_out.T.astype(BF16), _router_args(l1_router_w, l1_router_b), n_lat, tpb)
    out = _moe(h2, m_sel, gates, x1, mod1, _moe_weights(l1_w_gate_up, l1_b_gate_up, l1_w_down, l1_b_down),
               ctpb, final_norm, True)
    return out.reshape(bsz, seq, D_MODEL)
```

```python
import functools
import math

import numpy as np
import jax
import jax.numpy as jnp
from jax import lax
from jax.experimental import pallas as pl
from jax.experimental.pallas import tpu as pltpu

F32 = jnp.float32
BF16 = jnp.bfloat16

D_MODEL = 1024
GRID_W = 64
HEAD_DIM = 64
NA_WIDTH = 512
NA_HEADS = 8
MAX_KR = 8
KC = 16
FOURIER_DIM = 64
FOURIER_GROUPS = 8
FOURIER_WIDTH = 512
SSM_GROUP = 16
SSM_GROUPS = 64
SSM_STATE = 64
LAMBDA_RE_MAX = -1e-4
N_EXPERTS = 32
TOP_K = 4
SWIGLU_ALPHA = 1.702
SWIGLU_LIMIT = 7.0
EPS = 1e-6

TM = 512
EXPERT_ROWS = 512
S5_Q = 128
VMEM_LIMIT = 48 * 1024 * 1024
EXPERT_VMEM_LIMIT = 58 * 1024 * 1024


def _cparams(sem):
    return pltpu.CompilerParams(dimension_semantics=sem, vmem_limit_bytes=VMEM_LIMIT)


def _split_bf16(a):
    hi = a.astype(BF16)
    lo = (a - hi.astype(F32)).astype(BF16)
    return hi, lo


def _dot(a, b):
    return jnp.dot(a, b, preferred_element_type=F32)


def _dot_nt(a, b):
    return lax.dot_general(a, b, (((1,), (1,)), ((), ())), preferred_element_type=F32)


def _ada_kernel(c_ref, w_ref, b_ref, o_ref):
    c = c_ref[...]
    s = c * jax.nn.sigmoid(c)
    s_hi, s_lo = _split_bf16(s)
    w_hi, w_lo = _split_bf16(w_ref[...])
    o_ref[...] = _dot(s_hi, w_hi) + _dot(s_lo, w_hi) + _dot(s_hi, w_lo) + b_ref[...]


def _ada(c_pad, ada_w, ada_b):
    n = ada_w.shape[1]
    tn = 1024
    return pl.pallas_call(
        _ada_kernel,
        grid=(n // tn,),
        in_specs=[pl.BlockSpec((c_pad.shape[0], D_MODEL), lambda j: (0, 0)),
                  pl.BlockSpec((D_MODEL, tn), lambda j: (0, j)),
                  pl.BlockSpec((1, tn), lambda j: (0, j))],
        out_specs=pl.BlockSpec((c_pad.shape[0], tn), lambda j: (0, j)),
        out_shape=jax.ShapeDtypeStruct((c_pad.shape[0], n), F32),
        compiler_params=_cparams(("arbitrary",)),
        name="ada_mod",
    )(c_pad, ada_w, ada_b.reshape(1, n))


def _mod_spec(which, tiles_per_batch):
    tiles, bsz = tiles_per_batch
    return pl.BlockSpec((1, 1, D_MODEL), lambda i: (jnp.minimum(i // tiles, bsz) * 6 + which, 0, 0))


def _modulate(x, g, sh, sc):
    ms = jnp.mean(x * x, axis=-1, keepdims=True)
    y = x * lax.rsqrt(ms + EPS) * g
    return y * (1.0 + sc) + sh


def _modproj_kernel(*refs, n_lat_tiles, two_src, widths, q_scale):
    if two_src:
        x_ref, c_ref, sh_ref, sc_ref, g_ref, w_ref = refs[:6]
        o_refs = refs[6:]
        i = pl.program_id(0)
        x = jnp.where(i < n_lat_tiles, x_ref[...], c_ref[...])
    else:
        x_ref, sh_ref, sc_ref, g_ref, w_ref = refs[:5]
        o_refs = refs[5:]
        x = x_ref[...]
    h = _modulate(x, g_ref[...], sh_ref[0], sc_ref[0]).astype(BF16)
    off = 0
    for j, (o_ref, wd) in enumerate(zip(o_refs, widths)):
        y = _dot(h, w_ref[:, off:off + wd])
        if j == 0 and q_scale != 1.0:
            y = y * q_scale
        o_ref[...] = y.astype(o_ref.dtype)
        off += wd


def _modproj(x_srcs, mod, norm_g, w_bf, widths, n_lat, n_tok, tiles_per_batch, q_scale=1.0):
    n_tiles = n_tok // TM
    n_lat_tiles = n_lat // TM
    two_src = len(x_srcs) == 2
    n_out = w_bf.shape[1]
    if two_src:
        x_specs = [pl.BlockSpec((TM, D_MODEL), lambda i: (jnp.minimum(i, n_lat_tiles - 1), 0)),
                   pl.BlockSpec((TM, D_MODEL), lambda i: (jnp.maximum(i - n_lat_tiles, 0), 0))]
    else:
        x_specs = [pl.BlockSpec((TM, D_MODEL), lambda i: (i, 0))]
    return pl.pallas_call(
        functools.partial(_modproj_kernel, n_lat_tiles=n_lat_tiles, two_src=two_src,
                          widths=tuple(widths), q_scale=q_scale),
        grid=(n_tiles,),
        in_specs=x_specs + [_mod_spec(0, tiles_per_batch), _mod_spec(1, tiles_per_batch),
                            pl.BlockSpec((1, D_MODEL), lambda i: (0, 0)),
                            pl.BlockSpec((D_MODEL, n_out), lambda i: (0, 0))],
        out_specs=[pl.BlockSpec((TM, wd), lambda i: (i, 0)) for wd in widths],
        out_shape=[jax.ShapeDtypeStruct((n_tok, wd), BF16) for wd in widths],
        compiler_params=_cparams(("arbitrary",)),
        name="modulate_in_proj",
    )(*x_srcs, mod, mod, norm_g.reshape(1, D_MODEL), w_bf)


def _modproj_t_kernel(x_ref, sh_ref, sc_ref, g_ref, wt_ref, o_ref):
    h = _modulate(x_ref[...], g_ref[...], sh_ref[0], sc_ref[0]).astype(BF16)
    o_ref[...] = _dot_nt(wt_ref[...], h).astype(o_ref.dtype)


def _modproj_t(x_all, mod, norm_g, wt_bf, n_tok, tiles_per_batch):
    n_out = wt_bf.shape[0]
    return pl.pallas_call(
        _modproj_t_kernel,
        grid=(n_tok // TM,),
        in_specs=[pl.BlockSpec((TM, D_MODEL), lambda i: (i, 0)),
                  _mod_spec(0, tiles_per_batch), _mod_spec(1, tiles_per_batch),
                  pl.BlockSpec((1, D_MODEL), lambda i: (0, 0)),
                  pl.BlockSpec((n_out, D_MODEL), lambda i: (0, 0))],
        out_specs=pl.BlockSpec((n_out, TM), lambda i: (0, i)),
        out_shape=jax.ShapeDtypeStruct((n_out, n_tok), BF16),
        compiler_params=_cparams(("arbitrary",)),
        name="modulate_in_proj_t",
    )(x_all, mod, mod, norm_g.reshape(1, D_MODEL), wt_bf)


Q_ROWS = 4
BAND_ROWS = Q_ROWS + MAX_KR


def _na_bias_table(rpb, rows):
    qc = np.arange(GRID_W)[:, None]
    kc = np.arange(GRID_W)[None, :]
    win0 = np.clip(qc - KC // 2, 0, GRID_W - KC)
    col_valid = (kc >= win0) & (kc < win0 + KC)
    col_off = np.clip(kc - qc, -(KC - 1), KC - 1) + KC - 1
    cb = rpb.astype(F32)[:, :, col_off]
    cb = jnp.where(jnp.asarray(col_valid)[None, None], cb, -jnp.inf)
    n_blk = rows // Q_ROWS
    idx = np.zeros((3, Q_ROWS, BAND_ROWS), np.int64)
    ok = np.zeros((3, Q_ROWS, BAND_ROWS), bool)
    for cls, blk in enumerate((0, 1, n_blk - 1)):
        band0 = int(np.clip(blk * Q_ROWS - MAX_KR // 2, 0, rows - BAND_ROWS))
        for rho in range(Q_ROWS):
            r = blk * Q_ROWS + rho
            r0 = int(np.clip(r - MAX_KR // 2, 0, rows - MAX_KR))
            for kap in range(BAND_ROWS):
                kr = band0 + kap
                ok[cls, rho, kap] = r0 <= kr < r0 + MAX_KR
                idx[cls, rho, kap] = np.clip(kr - r + MAX_KR - 1, 0, 2 * MAX_KR - 2)
    t = cb[:, idx]
    t = jnp.where(jnp.asarray(ok)[None, :, :, :, None, None], t, -jnp.inf)
    t = jnp.transpose(t, (1, 0, 2, 4, 3, 5))
    return t.reshape(3, NA_HEADS, Q_ROWS * GRID_W, BAND_ROWS * GRID_W)


def _attn_kernel(q_ref, k_ref, v_ref, qc_ref, kc_ref, vc_ref, bias_ref, o_ref, oc_ref, *, rows):
    lane = lax.broadcasted_iota(jnp.int32, (1, 2 * HEAD_DIM), 1)
    first = lane < HEAD_DIM
    kc = kc_ref[...]
    vc = vc_ref[...]
    zero = jnp.zeros((), BF16)

    def softmax_pv(s_list, v_list):
        m = s_list[0].max(axis=-1, keepdims=True)
        for s in s_list[1:]:
            m = jnp.maximum(m, s.max(axis=-1, keepdims=True))
        den = None
        acc = None
        for s, vv in zip(s_list, v_list):
            p = jnp.exp(s - m)
            ps = p.sum(axis=-1, keepdims=True)
            den = ps if den is None else den + ps
            pv = _dot(p.astype(BF16), vv)
            acc = pv if acc is None else acc + pv
        return acc / den

    n_blk = rows // Q_ROWS
    q_len = Q_ROWS * GRID_W

    def block_body(i, carry):
        band0 = jnp.clip(i * Q_ROWS - MAX_KR // 2, 0, rows - BAND_ROWS)
        cls = jnp.where(i == 0, 0, jnp.where(i == n_blk - 1, 2, 1))
        rows_q = pl.ds(pl.multiple_of(i * q_len, q_len), q_len)
        q_b = q_ref[rows_q, :]
        band = pl.ds(pl.multiple_of(band0 * GRID_W, GRID_W), BAND_ROWS * GRID_W)
        kb = k_ref[band, :]
        vb = v_ref[band, :]
        outs = []
        for hh in range(2):
            q_h = jnp.where(first if hh == 0 else ~first, q_b, zero)
            s_w = _dot_nt(q_h, kb) + bias_ref[cls, hh]
            s_c = _dot_nt(q_h, kc)
            outs.append(softmax_pv([s_w, s_c], [vb, vc]))
        o_ref[rows_q, :] = jnp.where(first, outs[0], outs[1]).astype(o_ref.dtype)
        return carry

    lax.fori_loop(0, n_blk, block_body, 0)

    q_c = qc_ref[...]
    outs = []
    for hh in range(2):
        q_h = jnp.where(first if hh == 0 else ~first, q_c, zero)
        outs.append(softmax_pv([_dot_nt(q_h, kc)], [vc]))
    oc_ref[...] = jnp.where(first, outs[0], outs[1]).astype(oc_ref.dtype)


def _attention(q, k, v, rpb, bsz, seq, ctx_len):
    rows = seq // GRID_W
    assert rows % Q_ROWS == 0 and rows >= 3 * Q_ROWS and seq % ctx_len == 0
    bias = _na_bias_table(rpb, rows)
    cb0 = bsz * seq // ctx_len
    lat = pl.BlockSpec((seq, 2 * HEAD_DIM), lambda hp, b: (b, hp))
    ctx = pl.BlockSpec((ctx_len, 2 * HEAD_DIM), lambda hp, b: (cb0 + b, hp))
    return pl.pallas_call(
        functools.partial(_attn_kernel, rows=rows),
        grid=(NA_HEADS // 2, bsz),
        in_specs=[lat, lat, lat, ctx, ctx, ctx,
                  pl.BlockSpec((3, 2, Q_ROWS * GRID_W, BAND_ROWS * GRID_W), lambda hp, b: (0, hp, 0, 0))],
        out_specs=[pl.BlockSpec((seq, 2 * HEAD_DIM), lambda hp, b: (b, hp)),
                   pl.BlockSpec((ctx_len, 2 * HEAD_DIM), lambda hp, b: (b, hp))],
        out_shape=[jax.ShapeDtypeStruct((bsz * seq, NA_WIDTH), BF16),
                   jax.ShapeDtypeStruct((bsz * ctx_len, NA_WIDTH), BF16)],
        compiler_params=_cparams(("arbitrary", "arbitrary")),
        name="na_attention",
    )(q, k, v, q, k, v, bias)


@functools.lru_cache(maxsize=None)
def _dft_tables(n):
    j = np.arange(n, dtype=np.int64)
    ang = 2.0 * np.pi * ((j[:, None] * j[None, :]) % n).astype(np.float64) / n
    return np.cos(ang).astype(BF16), np.sin(ang).astype(BF16)


def _fourier_kernel(cos_ref, sin_ref, f_ref, bdc_ref, bds_ref, bdw_ref, o_ref, *, scale):
    fb = f_ref[...]
    zr = _dot(cos_ref[...], fb)
    zs = _dot(sin_ref[...], fb)
    fr = (_dot(zr.astype(BF16), bdc_ref[...]) - _dot(zs.astype(BF16), bds_ref[...])) * scale
    o_ref[...] = _dot(fr.astype(BF16), bdw_ref[...]).astype(o_ref.dtype)


def _fourier(f_all, bdw, bsz, n, row_block0, tm):
    cos_n, sin_n = _dft_tables(n)
    cos64, sin64 = _dft_tables(FOURIER_DIM)
    eye = np.eye(FOURIER_GROUPS)
    bdc = jnp.asarray(np.kron(eye, cos64.astype(np.float32)), BF16)
    bds = jnp.asarray(np.kron(eye, sin64.astype(np.float32)), BF16)
    mt = n // tm
    small = pl.BlockSpec((FOURIER_WIDTH, FOURIER_WIDTH), lambda b, m: (0, 0))
    return pl.pallas_call(
        functools.partial(_fourier_kernel, scale=1.0 / math.sqrt(n * FOURIER_DIM)),
        grid=(bsz, mt),
        in_specs=[pl.BlockSpec((tm, n), lambda b, m: (m, 0)),
                  pl.BlockSpec((tm, n), lambda b, m: (m, 0)),
                  pl.BlockSpec((n, FOURIER_WIDTH), lambda b, m: (row_block0 + b, 0)),
                  small, small, small],
        out_specs=pl.BlockSpec((tm, FOURIER_WIDTH), lambda b, m: (b * mt + m, 0)),
        out_shape=jax.ShapeDtypeStruct((bsz * n, FOURIER_WIDTH), BF16),
        compiler_params=_cparams(("arbitrary", "arbitrary")),
        name="fourier_mix",
    )(jnp.asarray(cos_n), jnp.asarray(sin_n), f_all, bdc, bds, bdw)


def _route(h2, rw_hi_ref, rw_lo_ref, rb_ref, m_ref, g_ref):
    h_hi, h_lo = _split_bf16(h2)
    logits = (_dot(h_hi, rw_hi_ref[...]) + _dot(h_lo, rw_hi_ref[...]) + _dot(h_hi, rw_lo_ref[...])
              + rb_ref[...])
    lane = lax.broadcasted_iota(jnp.int32, logits.shape, 1).astype(F32)
    work = logits
    sel_any = jnp.zeros(logits.shape, jnp.bool_)
    top = None
    for _ in range(TOP_K):
        m = work.max(axis=-1, keepdims=True)
        if top is None:
            top = m
        idx = jnp.where(work == m, lane, float(N_EXPERTS)).min(axis=-1, keepdims=True)
        sel = lane == idx
        sel_any = sel_any | sel
        work = jnp.where(sel, -jnp.inf, work)
    e = jnp.where(sel_any, jnp.exp(logits - top), 0.0)
    g_ref[...] = e / e.sum(axis=-1, keepdims=True)
    m_ref[...] = sel_any.astype(F32).astype(m_ref.dtype)


def _post_l0_kernel(a_ref, ac_ref, f_ref, fc_ref, x_ref, c_ref, g1_ref, sh_ref, sc_ref, ng_ref, w_ref,
                    rwh_ref, rwl_ref, rb_ref, xo_ref, h_ref, m_ref, g_ref, *, n_lat_tiles):
    lat = pl.program_id(0) < n_lat_tiles
    a = jnp.where(lat, a_ref[...], ac_ref[...])
    f = jnp.where(lat, f_ref[...], fc_ref[...])
    x = jnp.where(lat, x_ref[...], c_ref[...])
    y = _dot(a, w_ref[:NA_WIDTH, :]) + _dot(f, w_ref[NA_WIDTH:, :])
    x = x + g1_ref[0] * y
    xo_ref[...] = x
    h2 = _modulate(x, ng_ref[...], sh_ref[0], sc_ref[0])
    h_ref[...] = h2
    _route(h2, rwh_ref, rwl_ref, rb_ref, m_ref, g_ref)


def _post_l1_kernel(yt_ref, x_ref, g1_ref, sh_ref, sc_ref, ng_ref, wgt_ref, bg_ref, wot_ref,
                    rwh_ref, rwl_ref, rb_ref, xo_ref, h_ref, m_ref, g_ref):
    gyt = yt_ref[...]
    zt = _dot(wgt_ref[...], gyt) + bg_ref[...]
    vt = (gyt.astype(F32) * jax.nn.sigmoid(zt)).astype(BF16)
    x = x_ref[...] + g1_ref[0] * _dot(wot_ref[...], vt).T
    xo_ref[...] = x
    h2 = _modulate(x, ng_ref[...], sh_ref[0], sc_ref[0])
    h_ref[...] = h2
    _route(h2, rwh_ref, rwl_ref, rb_ref, m_ref, g_ref)


def _router_args(router_w, router_b):
    rw_hi = router_w.astype(BF16)
    rw_lo = (router_w - rw_hi.astype(F32)).astype(BF16)
    return rw_hi, rw_lo, router_b.reshape(1, N_EXPERTS).astype(F32)


def _post_out(n_tok):
    specs = [pl.BlockSpec((TM, D_MODEL), lambda i: (i, 0)),
             pl.BlockSpec((TM, D_MODEL), lambda i: (i, 0)),
             pl.BlockSpec((TM, N_EXPERTS), lambda i: (i, 0)),
             pl.BlockSpec((TM, N_EXPERTS), lambda i: (i, 0))]
    shapes = [jax.ShapeDtypeStruct((n_tok, D_MODEL), F32),
              jax.ShapeDtypeStruct((n_tok, D_MODEL), F32),
              jax.ShapeDtypeStruct((n_tok, N_EXPERTS), BF16),
              jax.ShapeDtypeStruct((n_tok, N_EXPERTS), F32)]
    return specs, shapes


def _const_spec(shape):
    return pl.BlockSpec(shape, lambda i: tuple(0 for _ in shape))


def _post_l0(a_lat, a_ctx, f_lat, f_ctx, x, ctx, mod, norm_g, w_out_bf, router, n_lat, n_tok, tiles_per_batch):
    n_lat_tiles = n_lat // TM

    def lat(wd):
        return pl.BlockSpec((TM, wd), lambda i: (jnp.minimum(i, n_lat_tiles - 1), 0))

    def cx(wd):
        return pl.BlockSpec((TM, wd), lambda i: (jnp.maximum(i - n_lat_tiles, 0), 0))

    out_specs, out_shapes = _post_out(n_tok)
    return pl.pallas_call(
        functools.partial(_post_l0_kernel, n_lat_tiles=n_lat_tiles),
        grid=(n_tok // TM,),
        in_specs=[lat(NA_WIDTH), cx(NA_WIDTH), lat(FOURIER_WIDTH), cx(FOURIER_WIDTH),
                  lat(D_MODEL), cx(D_MODEL),
                  _mod_spec(2, tiles_per_batch), _mod_spec(3, tiles_per_batch), _mod_spec(4, tiles_per_batch),
                  _const_spec((1, D_MODEL)), _const_spec((D_MODEL, D_MODEL)),
                  _const_spec((D_MODEL, N_EXPERTS)), _const_spec((D_MODEL, N_EXPERTS)),
                  _const_spec((1, N_EXPERTS))],
        out_specs=out_specs, out_shape=out_shapes,
        compiler_params=_cparams(("arbitrary",)),
        name="post_mixer_l0",
    )(a_lat, a_ctx, f_lat, f_ctx, x, ctx, mod, mod, mod, norm_g.reshape(1, D_MODEL), w_out_bf, *router)


def _post_l1(gyt, x_all, mod, norm_g, w_glu_t_bf, b_glu, w_out_t_bf, router, n_lat, tiles_per_batch):
    out_specs, out_shapes = _post_out(n_lat)
    return pl.pallas_call(
        _post_l1_kernel,
        grid=(n_lat // TM,),
        in_specs=[pl.BlockSpec((D_MODEL, TM), lambda i: (0, i)),
                  pl.BlockSpec((TM, D_MODEL), lambda i: (i, 0)),
                  _mod_spec(2, tiles_per_batch), _mod_spec(3, tiles_per_batch), _mod_spec(4, tiles_per_batch),
                  _const_spec((1, D_MODEL)), _const_spec((D_MODEL, D_MODEL)), _const_spec((D_MODEL, 1)),
                  _const_spec((D_MODEL, D_MODEL)),
                  _const_spec((D_MODEL, N_EXPERTS)), _const_spec((D_MODEL, N_EXPERTS)),
                  _const_spec((1, N_EXPERTS))],
        out_specs=out_specs, out_shape=out_shapes,
        compiler_params=_cparams(("arbitrary",)),
        name="post_mixer_l1",
    )(gyt, x_all, mod, mod, mod, norm_g.reshape(1, D_MODEL), w_glu_t_bf, b_glu.reshape(D_MODEL, 1),
      w_out_t_bf, *router)


CHUNK = 8
TILE_ROWS = TM * TOP_K + N_EXPERTS * CHUNK
TILE_CHUNKS = TILE_ROWS // CHUNK
BLOCK_CHUNKS = EXPERT_ROWS // CHUNK
SORT_ROWS = 256
HALF = D_MODEL // 2


def _pack_bf16_pairs(y):
    bits = pltpu.bitcast(y, jnp.uint32)
    return (bits[:, :HALF] & jnp.uint32(0xFFFF0000)) | (bits[:, HALF:] >> 16)


def _unpack_bf16_pairs(w):
    hi = pltpu.bitcast(w & jnp.uint32(0xFFFF0000), F32).astype(BF16)
    lo = pltpu.bitcast(w << 16, F32).astype(BF16)
    return hi, lo


def _sort_kernel(h_ref, m_ref, g_ref, hs_ref, pos_ref, gate_ref, cnt_ref):
    m = m_ref[...]
    mf = m.astype(F32)
    r = lax.broadcasted_iota(jnp.int32, (TM, TM), 0)
    c = lax.broadcasted_iota(jnp.int32, (TM, TM), 1)
    rank = _dot((r > c).astype(F32).astype(BF16), m)
    cnt = mf.sum(axis=0, keepdims=True)
    n8 = jnp.floor((cnt + (CHUNK - 1)) * (1.0 / CHUNK))
    er = lax.broadcasted_iota(jnp.int32, (N_EXPERTS, N_EXPERTS), 0)
    ec = lax.broadcasted_iota(jnp.int32, (N_EXPERTS, N_EXPERTS), 1)
    upper = (er < ec).astype(F32).astype(BF16)
    off = CHUNK * _dot(jnp.broadcast_to(n8, (8, N_EXPERTS)).astype(BF16), upper)[0:1]
    pos = rank + off
    slot = _dot(m, upper)
    chosen = mf > 0.5
    g = g_ref[...]
    lane = lax.broadcasted_iota(jnp.int32, pos_ref.shape, 1)
    p_out = jnp.zeros(pos_ref.shape, F32)
    g_out = jnp.zeros(gate_ref.shape, F32)
    for k in range(TOP_K):
        sel = chosen & (slot == float(k))
        pk = jnp.where(sel, pos, 0.0).sum(axis=-1, keepdims=True)
        gk = jnp.where(sel, g, 0.0).sum(axis=-1, keepdims=True)
        p_out = jnp.where(lane == k, pk, p_out)
        g_out = jnp.where(lane == k, gk, g_out)
    pos_ref[...] = p_out
    gate_ref[...] = g_out
    cnt_ref[...] = jnp.zeros_like(cnt_ref)
    cnt_ref[0:1, 0:N_EXPERTS] = cnt

    pos_t = p_out.T
    hb = h_ref[...].astype(BF16)
    for rc in range(TILE_ROWS // SORT_ROWS):
        rid = (lax.broadcasted_iota(jnp.int32, (SORT_ROWS, TM), 0) + rc * SORT_ROWS).astype(F32)
        hit = rid == pos_t[0:1, :]
        for k in range(1, TOP_K):
            hit = hit | (rid == pos_t[k:k + 1, :])
        onehot = hit.astype(F32).astype(BF16)
        hs_ref[rc * SORT_ROWS:(rc + 1) * SORT_ROWS, :] = _pack_bf16_pairs(_dot(onehot, hb))


def _sort(h2, m_sel, gates):
    n_tok = h2.shape[0]
    n_tiles = n_tok // TM
    tok = pl.BlockSpec((TM, N_EXPERTS), lambda i: (i, 0))
    wide = pl.BlockSpec((TM, 128), lambda i: (i, 0))
    return pl.pallas_call(
        _sort_kernel,
        grid=(n_tiles,),
        in_specs=[pl.BlockSpec((TM, D_MODEL), lambda i: (i, 0)), tok, tok],
        out_specs=[pl.BlockSpec((TILE_ROWS, HALF), lambda i: (i, 0)), wide, wide,
                   pl.BlockSpec((8, 128), lambda i: (i, 0))],
        out_shape=[jax.ShapeDtypeStruct((n_tiles * TILE_ROWS, HALF), jnp.uint32),
                   jax.ShapeDtypeStruct((n_tok, 128), F32),
                   jax.ShapeDtypeStruct((n_tok, 128), F32),
                   jax.ShapeDtypeStruct((n_tiles * 8, 128), F32)],
        compiler_params=_cparams(("arbitrary",)),
        name="moe_sort",
    )(h2, m_sel, gates)


def _chunk_tables(cnt, n_blocks):
    n_tiles = cnt.shape[0]
    n8 = (cnt + CHUNK - 1) // CHUNK
    lend = jnp.cumsum(n8, axis=1)
    lstart = lend - n8
    cum = jnp.cumsum(n8, axis=0)
    before = cum - n8
    total = cum[-1]
    nblk = (total + BLOCK_CHUNKS - 1) // BLOCK_CHUNKS
    bend = jnp.cumsum(nblk)
    gstart = (bend - nblk) * BLOCK_CHUNKS
    n_used = jnp.maximum(bend[-1], 1)
    blk = jnp.arange(n_blocks, dtype=jnp.int32)
    block_expert = (bend[None, :] <= jnp.minimum(blk, n_used - 1)[:, None]).sum(axis=1).astype(jnp.int32)
    block_expert = jnp.minimum(block_expert, N_EXPERTS - 1)
    lc = jnp.arange(TILE_CHUNKS, dtype=jnp.int32)
    e_of = (lend[:, None, :] <= lc[None, :, None]).sum(axis=-1)
    e_c = jnp.minimum(e_of, N_EXPERTS - 1)
    shift = gstart[None, :] + before - lstart
    to_sorted = jnp.where(e_of < N_EXPERTS, jnp.take_along_axis(shift, e_c, axis=1) + lc[None, :], 0)
    gc = jnp.arange(n_blocks * BLOCK_CHUNKS, dtype=jnp.int32)
    e_g = block_expert[gc // BLOCK_CHUNKS]
    rel = gc - gstart[e_g]
    valid = (gc // BLOCK_CHUNKS < n_used) & (rel < total[e_g])
    tile = jnp.minimum((cum.T[e_g] <= rel[:, None]).sum(axis=-1), n_tiles - 1)
    from_tiles = tile * TILE_CHUNKS + lstart[tile, e_g] + rel - before[tile, e_g]
    from_tiles = jnp.where(valid, from_tiles, TILE_CHUNKS - 1)
    return (block_expert, n_used.reshape(1).astype(jnp.int32), from_tiles.astype(jnp.int32),
            to_sorted.reshape(-1).astype(jnp.int32))


def _chunk_rows(chunk):
    start = chunk * CHUNK
    return pl.ds(start if isinstance(start, int) else pl.multiple_of(start, CHUNK), CHUNK)


def _chunk_copy(src_ref, src_chunk, dst_ref, dst_chunk, sem):
    return pltpu.make_async_copy(src_ref.at[_chunk_rows(src_chunk), :], dst_ref.at[_chunk_rows(dst_chunk), :], sem)


def _expert_kernel(be_ref, nu_ref, src_ref, hs_ref, wgu_ref, bgu_ref, wd_ref, bd_ref, o_ref,
                   wgu_bf, wd_bf, x_buf, sems):
    i = pl.program_id(0)
    n_used = nu_ref[0]

    def gather(block, slot, start):
        for cidx in range(BLOCK_CHUNKS):
            cp = _chunk_copy(hs_ref, src_ref[block * BLOCK_CHUNKS + cidx], x_buf.at[slot], cidx, sems.at[slot])
            if start:
                cp.start()
            else:
                cp.wait()

    @pl.when(i == 0)
    def _():
        gather(0, 0, True)

    for slot in range(2):
        @pl.when((i + 1 < n_used) & ((i + 1) % 2 == slot))
        def _(slot=slot):
            gather(i + 1, slot, True)

    @pl.when((i == 0) | (be_ref[i] != be_ref[jnp.maximum(i - 1, 0)]))
    def _():
        wgu_bf[...] = wgu_ref[0].astype(BF16)
        wd_bf[...] = wd_ref[0].astype(BF16)

    for slot in range(2):
        @pl.when((i < n_used) & (i % 2 == slot))
        def _(slot=slot):
            gather(i, slot, False)
            x_hi, x_lo = _unpack_bf16_pairs(x_buf[slot])
            gu = _dot(x_hi, wgu_bf[:HALF, :]) + _dot(x_lo, wgu_bf[HALF:, :]) + bgu_ref[0]
            gate = jnp.minimum(gu[:, :D_MODEL], SWIGLU_LIMIT)
            up = jnp.clip(gu[:, D_MODEL:], -SWIGLU_LIMIT, SWIGLU_LIMIT)
            act = (up + 1.0) * (gate * jax.nn.sigmoid(SWIGLU_ALPHA * gate))
            y = _dot(act.astype(BF16), wd_bf[...]) + bd_ref[0]
            o_ref[...] = _pack_bf16_pairs(y.astype(BF16).astype(F32))

    @pl.when(i >= n_used)
    def _():
        o_ref[...] = jnp.zeros_like(o_ref)


def _experts(hs, block_expert, n_used, from_tiles, wgu, bgu, wd, bd):
    n_blocks = block_expert.shape[0]

    def row_map(i, be, nu, src):
        return (i, 0)

    def exp_map(i, be, nu, src):
        return (be[i], 0, 0)

    grid_spec = pltpu.PrefetchScalarGridSpec(
        num_scalar_prefetch=3,
        grid=(n_blocks,),
        in_specs=[pl.BlockSpec(memory_space=pl.ANY),
                  pl.BlockSpec((1, D_MODEL, 2 * D_MODEL), exp_map),
                  pl.BlockSpec((1, 1, 2 * D_MODEL), exp_map),
                  pl.BlockSpec((1, D_MODEL, D_MODEL), exp_map),
                  pl.BlockSpec((1, 1, D_MODEL), exp_map)],
        out_specs=pl.BlockSpec((EXPERT_ROWS, HALF), row_map),
        scratch_shapes=[pltpu.VMEM((D_MODEL, 2 * D_MODEL), BF16), pltpu.VMEM((D_MODEL, D_MODEL), BF16),
                        pltpu.VMEM((2, EXPERT_ROWS, HALF), jnp.uint32), pltpu.SemaphoreType.DMA((2,))],
    )
    return pl.pallas_call(
        _expert_kernel,
        grid_spec=grid_spec,
        out_shape=jax.ShapeDtypeStruct((n_blocks * EXPERT_ROWS, HALF), jnp.uint32),
        compiler_params=pltpu.CompilerParams(dimension_semantics=("arbitrary",),
                                             vmem_limit_bytes=EXPERT_VMEM_LIMIT),
        name="moe_experts",
    )(block_expert, n_used, from_tiles, hs, wgu, bgu.reshape(N_EXPERTS, 1, 2 * D_MODEL), wd,
      bd.reshape(N_EXPERTS, 1, D_MODEL))


def _combine_kernel(dst_ref, ys_ref, pos_ref, gate_ref, x_ref, g2_ref, ng_ref, o_ref, y_buf, sems, *, final_norm):
    i = pl.program_id(0)
    n_tiles = pl.num_programs(0)

    def gather(tile, slot, start):
        def body(cidx, carry):
            cp = _chunk_copy(ys_ref, dst_ref[tile * TILE_CHUNKS + cidx], y_buf.at[slot], cidx, sems.at[slot])
            if start:
                cp.start()
            else:
                cp.wait()
            return carry
        lax.fori_loop(0, TILE_CHUNKS, body, 0)

    @pl.when(i == 0)
    def _():
        gather(0, 0, True)

    for slot in range(2):
        @pl.when((i + 1 < n_tiles) & ((i + 1) % 2 == slot))
        def _(slot=slot):
            gather(i + 1, slot, True)

    pos = pos_ref[...]
    gate = gate_ref[...]
    for slot in range(2):
        @pl.when(i % 2 == slot)
        def _(slot=slot):
            gather(i, slot, False)
            acc_hi = jnp.zeros((TM, HALF), F32)
            acc_lo = jnp.zeros((TM, HALF), F32)
            for rc in range(TILE_ROWS // SORT_ROWS):
                rid = (lax.broadcasted_iota(jnp.int32, (TM, SORT_ROWS), 1) + rc * SORT_ROWS).astype(F32)
                w = jnp.where(rid == pos[:, 0:1], gate[:, 0:1], 0.0)
                for k in range(1, TOP_K):
                    w = jnp.where(rid == pos[:, k:k + 1], gate[:, k:k + 1], w)
                w = w.astype(BF16)
                y_hi, y_lo = _unpack_bf16_pairs(y_buf[slot, rc * SORT_ROWS:(rc + 1) * SORT_ROWS, :])
                acc_hi = acc_hi + _dot(w, y_hi)
                acc_lo = acc_lo + _dot(w, y_lo)
            x = x_ref[...] + g2_ref[0] * jnp.concatenate([acc_hi, acc_lo], axis=-1)
            if final_norm:
                ms = jnp.mean(x * x, axis=-1, keepdims=True)
                x = x * lax.rsqrt(ms + EPS) * ng_ref[...]
            o_ref[...] = x


def _combine(ys, to_sorted, pos4, gate4, x_res, mod, norm_g, tiles_per_batch, final_norm):
    n_tok = x_res.shape[0]
    tok = pl.BlockSpec((TM, D_MODEL), lambda i, dst: (i, 0))
    wide = pl.BlockSpec((TM, 128), lambda i, dst: (i, 0))
    tiles, bsz = tiles_per_batch
    grid_spec = pltpu.PrefetchScalarGridSpec(
        num_scalar_prefetch=1,
        grid=(n_tok // TM,),
        in_specs=[pl.BlockSpec(memory_space=pl.ANY), wide, wide, tok,
                  pl.BlockSpec((1, 1, D_MODEL), lambda i, dst: (jnp.minimum(i // tiles, bsz) * 6 + 5, 0, 0)),
                  pl.BlockSpec((1, D_MODEL), lambda i, dst: (0, 0))],
        out_specs=tok,
        scratch_shapes=[pltpu.VMEM((2, TILE_ROWS, HALF), jnp.uint32), pltpu.SemaphoreType.DMA((2,))],
    )
    return pl.pallas_call(
        functools.partial(_combine_kernel, final_norm=final_norm),
        grid_spec=grid_spec,
        out_shape=jax.ShapeDtypeStruct((n_tok, D_MODEL), F32),
        compiler_params=_cparams(("arbitrary",)),
        name="moe_combine",
    )(to_sorted, ys, pos4, gate4, x_res, mod, norm_g.reshape(1, D_MODEL))


def _moe(h2, m_sel, gates, x_res, mod, moe_w, tiles_per_batch, norm_g, final_norm):
    wgu, bgu, wd, bd = moe_w
    n_tiles = h2.shape[0] // TM
    n_blocks = -(-n_tiles * TILE_CHUNKS // BLOCK_CHUNKS) + N_EXPERTS
    hs, pos4, gate4, cnt = _sort(h2, m_sel, gates)
    cnt_i = cnt.reshape(n_tiles, 8, 128)[:, 0, :N_EXPERTS].astype(jnp.int32)
    block_expert, n_used, from_tiles, to_sorted = _chunk_tables(cnt_i, n_blocks)
    ys = _experts(hs, block_expert, n_used, from_tiles, wgu, bgu, wd, bd)
    return _combine(ys, to_sorted, pos4, gate4, x_res, mod, norm_g, tiles_per_batch, final_norm)


def _s5_tables(lam_re, lam_im, log_dt, b_re, b_im, c_re, c_im, d_skip):
    q = S5_Q
    lam = lax.complex(jnp.minimum(lam_re.astype(F32), LAMBDA_RE_MAX), lam_im.astype(F32))
    dt = jnp.exp(log_dt.astype(F32))[..., None]
    a = jnp.exp(lam * dt)
    bbar = ((a - 1) / lam)[..., None] * lax.complex(b_re.astype(F32), b_im.astype(F32))
    cc = lax.complex(c_re.astype(F32), c_im.astype(F32))
    steps = jnp.arange(q + 1, dtype=F32)
    pw = jnp.exp((lam * dt)[:, :, None, :] * steps[None, None, :, None])
    kern = jnp.einsum('dgop,dgkp,dgpc->dgcok', cc, pw[:, :, :q], bbar).real
    eye = jnp.eye(SSM_GROUP, dtype=F32)
    k_fwd = kern[0].at[..., 0].add(kern[1][..., 0] + d_skip.astype(F32)[:, :, None] * eye[None])
    k_bwd = jnp.concatenate([jnp.zeros_like(kern[1][..., :1]), kern[1][..., :0:-1]], axis=-1)
    k_fwd = k_fwd.reshape(SSM_GROUPS, SSM_GROUP * SSM_GROUP, q)
    k_bwd = k_bwd.reshape(SSM_GROUPS, SSM_GROUP * SSM_GROUP, q)
    ws = []
    for d in range(2):
        e = pw[d][:, ::-1][:, 1:] if d == 0 else pw[d][:, :q]
        w = e[:, None, :, :] * jnp.transpose(bbar[d], (0, 2, 1))[:, :, None, :]
        ws.append(w.reshape(SSM_GROUPS, SSM_GROUP * q, SSM_STATE))
    w_state = jnp.concatenate([ws[0].real, ws[1].real, ws[0].imag, ws[1].imag], axis=-1)
    wo = []
    for d in range(2):
        e = pw[d][:, 1:] if d == 0 else pw[d][:, ::-1][:, :q]
        w = cc[d][:, :, None, :] * e[:, None, :, :]
        wo.append(jnp.transpose(w, (0, 3, 1, 2)).reshape(SSM_GROUPS, SSM_STATE, SSM_GROUP * q))
    z = jnp.zeros_like(wo[0].real)
    wo_a = jnp.concatenate([wo[0].real, z, -wo[0].imag, z], axis=1)
    wo_b = jnp.concatenate([z, wo[1].real, z, -wo[1].imag], axis=1)
    a_q = pw[:, :, q]
    a_re = jnp.concatenate([a_q[0].real, a_q[1].real], axis=-1).reshape(SSM_GROUPS, 1, 2 * SSM_STATE)
    a_im = jnp.concatenate([a_q[0].imag, a_q[1].imag], axis=-1).reshape(SSM_GROUPS, 1, 2 * SSM_STATE)
    return k_fwd, k_bwd, w_state.astype(BF16), wo_a.astype(BF16), wo_b.astype(BF16), a_re, a_im


def _s5_kernel(u_ref, kf_ref, kb_ref, ws_ref, woa_ref, wob_ref, ar_ref, ai_ref, o_ref,
               w_ref, s_ref, h_ref, *, bsz, n_lat_chunks, n_ctx_chunks):
    q = S5_Q
    half = 2 * SSM_STATE
    n_lat = bsz * n_lat_chunks

    causal = (lax.broadcasted_iota(jnp.int32, (q, q), 1) >= lax.broadcasted_iota(jnp.int32, (q, q), 0))

    def build(c, carry):
        for o in range(SSM_GROUP):
            row = pl.ds(c * SSM_GROUP + o, 1)
            kf = pltpu.roll(jnp.broadcast_to(kf_ref[0, row, :], (q, q)), 0, 1, stride=1, stride_axis=0)
            kb = pltpu.roll(jnp.broadcast_to(kb_ref[0, row, :], (q, q)), 0, 1, stride=1, stride_axis=0)
            w_ref[pl.ds(pl.multiple_of(c * q, q), q), o * q:(o + 1) * q] = jnp.where(causal, kf, kb).astype(BF16)
        return carry

    lax.fori_loop(0, SSM_GROUP, build, 0)

    u = u_ref[...]
    ucat = jnp.concatenate([u[c] for c in range(SSM_GROUP)], axis=-1)
    s_all = _dot(ucat, ws_ref[0])
    s_ref[0] = s_all[:, :half]
    s_ref[1] = s_all[:, half:]

    fwd = lax.broadcasted_iota(jnp.int32, (bsz, half), 1) < SSM_STATE
    ar = jnp.broadcast_to(ar_ref[0], (bsz, half))
    ai = jnp.broadcast_to(ai_ref[0], (bsz, half))

    def chunk_rows(base, j, per_sample):
        return pl.ds(base + j, bsz, stride=per_sample)

    def load(base, j, n):
        rows_f = chunk_rows(base, j, n)
        rows_b = chunk_rows(base, n - 1 - j, n)
        return (jnp.where(fwd, s_ref[0, rows_f, :], s_ref[0, rows_b, :]),
                jnp.where(fwd, s_ref[1, rows_f, :], s_ref[1, rows_b, :]))

    def step(hr, hi, sr, si):
        return ar * hr - ai * hi + sr, ar * hi + ai * hr + si

    def ctx_body(j, carry):
        return step(*carry, *load(n_lat, j, n_ctx_chunks))

    def lat_body(j, carry):
        hr, hi = carry
        rows_f = chunk_rows(0, j, n_lat_chunks)
        rows_b = chunk_rows(0, n_lat_chunks - 1 - j, n_lat_chunks)
        h_ref[0, rows_f, :] = hr
        h_ref[1, rows_f, :] = hi
        h_ref[2, rows_b, :] = hr
        h_ref[3, rows_b, :] = hi
        return step(hr, hi, *load(0, j, n_lat_chunks))

    zero = jnp.zeros((bsz, half), F32)
    carry = lax.fori_loop(0, n_ctx_chunks, ctx_body, (zero, zero))
    lax.fori_loop(0, n_lat_chunks, lat_body, carry)

    y = _dot(ucat[:n_lat], w_ref[...])
    for part, wo_ref in ((0, woa_ref), (2, wob_ref)):
        y = y + _dot(h_ref[part].astype(BF16), wo_ref[0, :half, :])
        y = y + _dot(h_ref[part + 1].astype(BF16), wo_ref[0, half:, :])
    for o in range(SSM_GROUP):
        o_ref[o] = jax.nn.gelu(y[:, o * q:(o + 1) * q]).astype(o_ref.dtype)


def _s5(u3, tables, bsz, n_lat_chunks, n_ctx_chunks):
    k_fwd, k_bwd, w_state, wo_a, wo_b, a_re, a_im = tables
    q = S5_Q
    n_chunks = u3.shape[1]
    n_lat = bsz * n_lat_chunks
    assert n_chunks == n_lat + bsz * n_ctx_chunks

    def gspec(*shape):
        return pl.BlockSpec((1,) + shape, lambda g: (g,) + tuple(0 for _ in shape))

    return pl.pallas_call(
        functools.partial(_s5_kernel, bsz=bsz, n_lat_chunks=n_lat_chunks, n_ctx_chunks=n_ctx_chunks),
        grid=(SSM_GROUPS,),
        in_specs=[pl.BlockSpec((SSM_GROUP, n_chunks, q), lambda g: (g, 0, 0)),
                  gspec(SSM_GROUP * SSM_GROUP, q), gspec(SSM_GROUP * SSM_GROUP, q),
                  gspec(SSM_GROUP * q, 4 * SSM_STATE),
                  gspec(4 * SSM_STATE, SSM_GROUP * q), gspec(4 * SSM_STATE, SSM_GROUP * q),
                  gspec(1, 2 * SSM_STATE), gspec(1, 2 * SSM_STATE)],
        out_specs=pl.BlockSpec((SSM_GROUP, n_lat, q), lambda g: (g, 0, 0)),
        out_shape=jax.ShapeDtypeStruct((SSM_GROUPS * SSM_GROUP, n_lat, q), BF16),
        scratch_shapes=[pltpu.VMEM((SSM_GROUP * q, SSM_GROUP * q), BF16),
                        pltpu.VMEM((2, n_chunks, 2 * SSM_STATE), F32),
                        pltpu.VMEM((4, n_lat, 2 * SSM_STATE), F32)],
        compiler_params=_cparams(("arbitrary",)),
        name="s5_scan",
    )(u3, k_fwd, k_bwd, w_state, wo_a, wo_b, a_re, a_im)


def _moe_weights(w_gate_up, b_gate_up, w_down, b_down):
    return w_gate_up, b_gate_up, w_down, b_down


def kernel(x, c, ctx, c_ctx, l0_ada_w, l0_ada_b, l0_norm_mix, l0_w_in, l0_rpb, l0_w_fourier, l0_w_out, l0_norm_ffn, l0_router_w, l0_router_b, l0_w_gate_up, l0_b_gate_up, l0_w_down, l0_b_down, l1_ada_w, l1_ada_b, l1_norm_mix, l1_w_in, l1_lambda_re, l1_lambda_im, l1_log_dt, l1_b_re, l1_b_im, l1_c_re, l1_c_im, l1_d_skip, l1_w_glu, l1_b_glu, l1_w_out, l1_norm_ffn, l1_router_w, l1_router_b, l1_w_gate_up, l1_b_gate_up, l1_w_down, l1_b_down, final_norm):
    bsz, seq, _ = x.shape
    ctx_len = ctx.shape[1]
    n_lat = bsz * seq
    n_ctx = bsz * ctx_len
    n_tok = n_lat + n_ctx
    assert seq % TM == 0 and n_ctx % TM == 0
    x2 = x.reshape(n_lat, D_MODEL)
    ctx2 = ctx.reshape(n_ctx, D_MODEL)

    n_mod = bsz + 1
    c_pad = jnp.zeros((-(-n_mod // 8) * 8, D_MODEL), F32).at[:bsz].set(c).at[bsz].set(c_ctx)
    mod0 = _ada(c_pad, l0_ada_w, l0_ada_b)[:n_mod].reshape(n_mod * 6, 1, D_MODEL)
    mod1 = _ada(c_pad, l1_ada_w, l1_ada_b)[:n_mod].reshape(n_mod * 6, 1, D_MODEL)
    tpb = (seq // TM, bsz)

    q, k, v, f = _modproj([x2, ctx2], mod0, l0_norm_mix, l0_w_in.astype(BF16),
                          (NA_WIDTH, NA_WIDTH, NA_WIDTH, FOURIER_WIDTH), n_lat, n_tok, tpb,
                          q_scale=HEAD_DIM ** -0.5)
    a_lat, a_ctx = _attention(q, k, v, l0_rpb, bsz, seq, ctx_len)
    bdw = (jnp.eye(FOURIER_GROUPS, dtype=F32)[:, None, :, None] * l0_w_fourier[:, :, None, :]).reshape(
        FOURIER_WIDTH, FOURIER_WIDTH).astype(BF16)
    f_lat = _fourier(f, bdw, bsz, seq, 0, TM)
    f_ctx = _fourier(f, bdw, bsz, ctx_len, n_lat // ctx_len, ctx_len)
    x_all, h2, m_sel, gates = _post_l0(a_lat, a_ctx, f_lat, f_ctx, x2, ctx2, mod0, l0_norm_ffn,
                                       l0_w_out.astype(BF16), _router_args(l0_router_w, l0_router_b),
                                       n_lat, n_tok, tpb)
    x_all = _moe(h2, m_sel, gates, x_all, mod0, _moe_weights(l0_w_gate_up, l0_b_gate_up, l0_w_down, l0_b_down),
                 tpb, final_norm, False)

    assert seq % S5_Q == 0 and ctx_len % S5_Q == 0
    ut = _modproj_t(x_all, mod1, l1_norm_mix, l1_w_in.T.astype(BF16), n_tok, tpb)
    tables = _s5_tables(l1_lambda_re, l1_lambda_im, l1_log_dt, l1_b_re, l1_b_im, l1_c_re, l1_c_im, l1_d_skip)
    yt = _s5(ut.reshape(D_MODEL, n_tok // S5_Q, S5_Q), tables, bsz, seq // S5_Q, ctx_len // S5_Q)
    gy = yt.reshape(D_MODEL, n_lat)
    x1, h2, m_sel, gates = _post_l1(gy, x_all, mod1, l1_norm_ffn, l1_w_glu.T.astype(BF16), l1_b_glu,
                                    l1_w_out.T.astype(BF16), _router_args(l1_router_w, l1_router_b), n_lat, tpb)
    out = _moe(h2, m_sel, gates, x1, mod1, _moe_weights(l1_w_gate_up, l1_b_gate_up, l1_w_down, l1_b_down),
               tpb, final_norm, True)
    return out.reshape(bsz, seq, D_MODEL)
```

```python
import functools
import math

import numpy as np
import jax
import jax.numpy as jnp
from jax import lax
from jax.experimental import pallas as pl
from jax.experimental.pallas import tpu as pltpu

F32 = jnp.float32
BF16 = jnp.bfloat16

D_MODEL = 1024
GRID_W = 64
HEAD_DIM = 64
NA_WIDTH = 512
NA_HEADS = 8
MAX_KR = 8
KC = 16
FOURIER_DIM = 64
FOURIER_GROUPS = 8
FOURIER_WIDTH = 512
SSM_GROUP = 16
SSM_GROUPS = 64
SSM_STATE = 64
LAMBDA_RE_MAX = -1e-4
N_EXPERTS = 32
TOP_K = 4
SWIGLU_ALPHA = 1.702
SWIGLU_LIMIT = 7.0
EPS = 1e-6

TM = 512
EXPERT_ROWS = 512
S5_Q = 128
VMEM_LIMIT = 48 * 1024 * 1024
EXPERT_VMEM_LIMIT = 58 * 1024 * 1024


def _cparams(sem):
    return pltpu.CompilerParams(dimension_semantics=sem, vmem_limit_bytes=VMEM_LIMIT)


def _split_bf16(a):
    hi = a.astype(BF16)
    lo = (a - hi.astype(F32)).astype(BF16)
    return hi, lo


def _dot(a, b):
    return jnp.dot(a, b, preferred_element_type=F32)


def _dot_nt(a, b):
    return lax.dot_general(a, b, (((1,), (1,)), ((), ())), preferred_element_type=F32)


def _ada_kernel(c_ref, w_ref, b_ref, o_ref):
    c = c_ref[...]
    s = c * jax.nn.sigmoid(c)
    s_hi, s_lo = _split_bf16(s)
    w_hi, w_lo = _split_bf16(w_ref[...])
    o_ref[...] = _dot(s_hi, w_hi) + _dot(s_lo, w_hi) + _dot(s_hi, w_lo) + b_ref[...]


def _ada(c_pad, ada_w, ada_b):
    n = ada_w.shape[1]
    tn = 1024
    return pl.pallas_call(
        _ada_kernel,
        grid=(n // tn,),
        in_specs=[pl.BlockSpec((c_pad.shape[0], D_MODEL), lambda j: (0, 0)),
                  pl.BlockSpec((D_MODEL, tn), lambda j: (0, j)),
                  pl.BlockSpec((1, tn), lambda j: (0, j))],
        out_specs=pl.BlockSpec((c_pad.shape[0], tn), lambda j: (0, j)),
        out_shape=jax.ShapeDtypeStruct((c_pad.shape[0], n), F32),
        compiler_params=_cparams(("arbitrary",)),
        name="ada_mod",
    )(c_pad, ada_w, ada_b.reshape(1, n))


def _mod_spec(which, tiles_per_batch):
    tiles, bsz = tiles_per_batch
    return pl.BlockSpec((1, 1, D_MODEL), lambda i: (jnp.minimum(i // tiles, bsz) * 6 + which, 0, 0))


def _modulate(x, g, sh, sc):
    ms = jnp.mean(x * x, axis=-1, keepdims=True)
    y = x * lax.rsqrt(ms + EPS) * g
    return y * (1.0 + sc) + sh


def _modproj_kernel(*refs, n_lat_tiles, two_src, widths, q_scale):
    if two_src:
        x_ref, c_ref, sh_ref, sc_ref, g_ref, w_ref = refs[:6]
        o_refs = refs[6:]
        i = pl.program_id(0)
        x = jnp.where(i < n_lat_tiles, x_ref[...], c_ref[...])
    else:
        x_ref, sh_ref, sc_ref, g_ref, w_ref = refs[:5]
        o_refs = refs[5:]
        x = x_ref[...]
    h = _modulate(x, g_ref[...], sh_ref[0], sc_ref[0]).astype(BF16)
    off = 0
    for j, (o_ref, wd) in enumerate(zip(o_refs, widths)):
        y = _dot(h, w_ref[:, off:off + wd])
        if j == 0 and q_scale != 1.0:
            y = y * q_scale
        o_ref[...] = y.astype(o_ref.dtype)
        off += wd


def _modproj(x_srcs, mod, norm_g, w_bf, widths, n_lat, n_tok, tiles_per_batch, q_scale=1.0):
    n_tiles = n_tok // TM
    n_lat_tiles = n_lat // TM
    two_src = len(x_srcs) == 2
    n_out = w_bf.shape[1]
    if two_src:
        x_specs = [pl.BlockSpec((TM, D_MODEL), lambda i: (jnp.minimum(i, n_lat_tiles - 1), 0)),
                   pl.BlockSpec((TM, D_MODEL), lambda i: (jnp.maximum(i - n_lat_tiles, 0), 0))]
    else:
        x_specs = [pl.BlockSpec((TM, D_MODEL), lambda i: (i, 0))]
    return pl.pallas_call(
        functools.partial(_modproj_kernel, n_lat_tiles=n_lat_tiles, two_src=two_src,
                          widths=tuple(widths), q_scale=q_scale),
        grid=(n_tiles,),
        in_specs=x_specs + [_mod_spec(0, tiles_per_batch), _mod_spec(1, tiles_per_batch),
                            pl.BlockSpec((1, D_MODEL), lambda i: (0, 0)),
                            pl.BlockSpec((D_MODEL, n_out), lambda i: (0, 0))],
        out_specs=[pl.BlockSpec((TM, wd), lambda i: (i, 0)) for wd in widths],
        out_shape=[jax.ShapeDtypeStruct((n_tok, wd), BF16) for wd in widths],
        compiler_params=_cparams(("arbitrary",)),
        name="modulate_in_proj",
    )(*x_srcs, mod, mod, norm_g.reshape(1, D_MODEL), w_bf)


def _modproj_t_kernel(x_ref, sh_ref, sc_ref, g_ref, wt_ref, o_ref):
    h = _modulate(x_ref[...], g_ref[...], sh_ref[0], sc_ref[0]).astype(BF16)
    o_ref[...] = _dot_nt(wt_ref[...], h).astype(o_ref.dtype)


def _modproj_t(x_all, mod, norm_g, wt_bf, n_tok, tiles_per_batch):
    n_out = wt_bf.shape[0]
    return pl.pallas_call(
        _modproj_t_kernel,
        grid=(n_tok // TM,),
        in_specs=[pl.BlockSpec((TM, D_MODEL), lambda i: (i, 0)),
                  _mod_spec(0, tiles_per_batch), _mod_spec(1, tiles_per_batch),
                  pl.BlockSpec((1, D_MODEL), lambda i: (0, 0)),
                  pl.BlockSpec((n_out, D_MODEL), lambda i: (0, 0))],
        out_specs=pl.BlockSpec((n_out, TM), lambda i: (0, i)),
        out_shape=jax.ShapeDtypeStruct((n_out, n_tok), BF16),
        compiler_params=_cparams(("arbitrary",)),
        name="modulate_in_proj_t",
    )(x_all, mod, mod, norm_g.reshape(1, D_MODEL), wt_bf)


Q_ROWS = 4
BAND_ROWS = Q_ROWS + MAX_KR


def _na_bias_table(rpb, rows):
    qc = np.arange(GRID_W)[:, None]
    kc = np.arange(GRID_W)[None, :]
    win0 = np.clip(qc - KC // 2, 0, GRID_W - KC)
    col_valid = (kc >= win0) & (kc < win0 + KC)
    col_off = np.clip(kc - qc, -(KC - 1), KC - 1) + KC - 1
    cb = rpb.astype(F32)[:, :, col_off]
    cb = jnp.where(jnp.asarray(col_valid)[None, None], cb, -jnp.inf)
    n_blk = rows // Q_ROWS
    idx = np.zeros((3, Q_ROWS, BAND_ROWS), np.int64)
    ok = np.zeros((3, Q_ROWS, BAND_ROWS), bool)
    for cls, blk in enumerate((0, 1, n_blk - 1)):
        band0 = int(np.clip(blk * Q_ROWS - MAX_KR // 2, 0, rows - BAND_ROWS))
        for rho in range(Q_ROWS):
            r = blk * Q_ROWS + rho
            r0 = int(np.clip(r - MAX_KR // 2, 0, rows - MAX_KR))
            for kap in range(BAND_ROWS):
                kr = band0 + kap
                ok[cls, rho, kap] = r0 <= kr < r0 + MAX_KR
                idx[cls, rho, kap] = np.clip(kr - r + MAX_KR - 1, 0, 2 * MAX_KR - 2)
    t = cb[:, idx]
    t = jnp.where(jnp.asarray(ok)[None, :, :, :, None, None], t, -jnp.inf)
    t = jnp.transpose(t, (1, 0, 2, 4, 3, 5))
    return t.reshape(3, NA_HEADS, Q_ROWS * GRID_W, BAND_ROWS * GRID_W)


def _attn_kernel(q_ref, k_ref, v_ref, qc_ref, kc_ref, vc_ref, bias_ref, o_ref, oc_ref, *, rows):
    lane = lax.broadcasted_iota(jnp.int32, (1, 2 * HEAD_DIM), 1)
    first = lane < HEAD_DIM
    kc = kc_ref[...]
    vc = vc_ref[...]
    zero = jnp.zeros((), BF16)

    def softmax_pv(s_list, v_list):
        m = s_list[0].max(axis=-1, keepdims=True)
        for s in s_list[1:]:
            m = jnp.maximum(m, s.max(axis=-1, keepdims=True))
        den = None
        acc = None
        for s, vv in zip(s_list, v_list):
            p = jnp.exp(s - m)
            ps = p.sum(axis=-1, keepdims=True)
            den = ps if den is None else den + ps
            pv = _dot(p.astype(BF16), vv)
            acc = pv if acc is None else acc + pv
        return acc / den

    n_blk = rows // Q_ROWS
    q_len = Q_ROWS * GRID_W

    def block_body(i, carry):
        band0 = jnp.clip(i * Q_ROWS - MAX_KR // 2, 0, rows - BAND_ROWS)
        cls = jnp.where(i == 0, 0, jnp.where(i == n_blk - 1, 2, 1))
        rows_q = pl.ds(pl.multiple_of(i * q_len, q_len), q_len)
        q_b = q_ref[rows_q, :]
        band = pl.ds(pl.multiple_of(band0 * GRID_W, GRID_W), BAND_ROWS * GRID_W)
        kb = k_ref[band, :]
        vb = v_ref[band, :]
        outs = []
        for hh in range(2):
            q_h = jnp.where(first if hh == 0 else ~first, q_b, zero)
            s_w = _dot_nt(q_h, kb) + bias_ref[cls, hh]
            s_c = _dot_nt(q_h, kc)
            outs.append(softmax_pv([s_w, s_c], [vb, vc]))
        o_ref[rows_q, :] = jnp.where(first, outs[0], outs[1]).astype(o_ref.dtype)
        return carry

    lax.fori_loop(0, n_blk, block_body, 0)

    q_c = qc_ref[...]
    outs = []
    for hh in range(2):
        q_h = jnp.where(first if hh == 0 else ~first, q_c, zero)
        outs.append(softmax_pv([_dot_nt(q_h, kc)], [vc]))
    oc_ref[...] = jnp.where(first, outs[0], outs[1]).astype(oc_ref.dtype)


def _attention(q, k, v, rpb, bsz, seq, ctx_len):
    rows = seq // GRID_W
    assert rows % Q_ROWS == 0 and rows >= 3 * Q_ROWS and seq % ctx_len == 0
    bias = _na_bias_table(rpb, rows)
    cb0 = bsz * seq // ctx_len
    lat = pl.BlockSpec((seq, 2 * HEAD_DIM), lambda hp, b: (b, hp))
    ctx = pl.BlockSpec((ctx_len, 2 * HEAD_DIM), lambda hp, b: (cb0 + b, hp))
    return pl.pallas_call(
        functools.partial(_attn_kernel, rows=rows),
        grid=(NA_HEADS // 2, bsz),
        in_specs=[lat, lat, lat, ctx, ctx, ctx,
                  pl.BlockSpec((3, 2, Q_ROWS * GRID_W, BAND_ROWS * GRID_W), lambda hp, b: (0, hp, 0, 0))],
        out_specs=[pl.BlockSpec((seq, 2 * HEAD_DIM), lambda hp, b: (b, hp)),
                   pl.BlockSpec((ctx_len, 2 * HEAD_DIM), lambda hp, b: (b, hp))],
        out_shape=[jax.ShapeDtypeStruct((bsz * seq, NA_WIDTH), BF16),
                   jax.ShapeDtypeStruct((bsz * ctx_len, NA_WIDTH), BF16)],
        compiler_params=_cparams(("arbitrary", "arbitrary")),
        name="na_attention",
    )(q, k, v, q, k, v, bias)


@functools.lru_cache(maxsize=None)
def _dft_tables(n):
    j = np.arange(n, dtype=np.int64)
    ang = 2.0 * np.pi * ((j[:, None] * j[None, :]) % n).astype(np.float64) / n
    return np.cos(ang).astype(BF16), np.sin(ang).astype(BF16)


def _fourier_kernel(cos_ref, sin_ref, f_ref, bdc_ref, bds_ref, bdw_ref, o_ref, *, scale):
    fb = f_ref[...]
    zr = _dot(cos_ref[...], fb)
    zs = _dot(sin_ref[...], fb)
    fr = (_dot(zr.astype(BF16), bdc_ref[...]) - _dot(zs.astype(BF16), bds_ref[...])) * scale
    o_ref[...] = _dot(fr.astype(BF16), bdw_ref[...]).astype(o_ref.dtype)


def _fourier(f_all, bdw, bsz, n, row_block0, tm):
    cos_n, sin_n = _dft_tables(n)
    cos64, sin64 = _dft_tables(FOURIER_DIM)
    eye = np.eye(FOURIER_GROUPS)
    bdc = jnp.asarray(np.kron(eye, cos64.astype(np.float32)), BF16)
    bds = jnp.asarray(np.kron(eye, sin64.astype(np.float32)), BF16)
    mt = n // tm
    small = pl.BlockSpec((FOURIER_WIDTH, FOURIER_WIDTH), lambda b, m: (0, 0))
    return pl.pallas_call(
        functools.partial(_fourier_kernel, scale=1.0 / math.sqrt(n * FOURIER_DIM)),
        grid=(bsz, mt),
        in_specs=[pl.BlockSpec((tm, n), lambda b, m: (m, 0)),
                  pl.BlockSpec((tm, n), lambda b, m: (m, 0)),
                  pl.BlockSpec((n, FOURIER_WIDTH), lambda b, m: (row_block0 + b, 0)),
                  small, small, small],
        out_specs=pl.BlockSpec((tm, FOURIER_WIDTH), lambda b, m: (b * mt + m, 0)),
        out_shape=jax.ShapeDtypeStruct((bsz * n, FOURIER_WIDTH), BF16),
        compiler_params=_cparams(("arbitrary", "arbitrary")),
        name="fourier_mix",
    )(jnp.asarray(cos_n), jnp.asarray(sin_n), f_all, bdc, bds, bdw)


def _route(h2, rw_hi_ref, rw_lo_ref, rb_ref, m_ref, g_ref):
    h_hi, h_lo = _split_bf16(h2)
    logits = (_dot(h_hi, rw_hi_ref[...]) + _dot(h_lo, rw_hi_ref[...]) + _dot(h_hi, rw_lo_ref[...])
              + rb_ref[...])
    lane = lax.broadcasted_iota(jnp.int32, logits.shape, 1).astype(F32)
    work = logits
    sel_any = jnp.zeros(logits.shape, jnp.bool_)
    top = None
    for _ in range(TOP_K):
        m = work.max(axis=-1, keepdims=True)
        if top is None:
            top = m
        idx = jnp.where(work == m, lane, float(N_EXPERTS)).min(axis=-1, keepdims=True)
        sel = lane == idx
        sel_any = sel_any | sel
        work = jnp.where(sel, -jnp.inf, work)
    e = jnp.where(sel_any, jnp.exp(logits - top), 0.0)
    g_ref[...] = e / e.sum(axis=-1, keepdims=True)
    m_ref[...] = sel_any.astype(F32).astype(m_ref.dtype)


def _post_l0_kernel(a_ref, ac_ref, f_ref, fc_ref, x_ref, c_ref, g1_ref, sh_ref, sc_ref, ng_ref, w_ref,
                    rwh_ref, rwl_ref, rb_ref, xo_ref, h_ref, m_ref, g_ref, *, n_lat_tiles):
    lat = pl.program_id(0) < n_lat_tiles
    a = jnp.where(lat, a_ref[...], ac_ref[...])
    f = jnp.where(lat, f_ref[...], fc_ref[...])
    x = jnp.where(lat, x_ref[...], c_ref[...])
    y = _dot(a, w_ref[:NA_WIDTH, :]) + _dot(f, w_ref[NA_WIDTH:, :])
    x = x + g1_ref[0] * y
    xo_ref[...] = x
    h2 = _modulate(x, ng_ref[...], sh_ref[0], sc_ref[0])
    h_ref[...] = h2
    _route(h2, rwh_ref, rwl_ref, rb_ref, m_ref, g_ref)


def _post_l1_kernel(yt_ref, x_ref, g1_ref, sh_ref, sc_ref, ng_ref, wgt_ref, bg_ref, wot_ref,
                    rwh_ref, rwl_ref, rb_ref, xo_ref, h_ref, m_ref, g_ref):
    gyt = yt_ref[...]
    zt = _dot(wgt_ref[...], gyt) + bg_ref[...]
    vt = (gyt.astype(F32) * jax.nn.sigmoid(zt)).astype(BF16)
    x = x_ref[...] + g1_ref[0] * _dot(wot_ref[...], vt).T
    xo_ref[...] = x
    h2 = _modulate(x, ng_ref[...], sh_ref[0], sc_ref[0])
    h_ref[...] = h2
    _route(h2, rwh_ref, rwl_ref, rb_ref, m_ref, g_ref)


def _router_args(router_w, router_b):
    rw_hi = router_w.astype(BF16)
    rw_lo = (router_w - rw_hi.astype(F32)).astype(BF16)
    return rw_hi, rw_lo, router_b.reshape(1, N_EXPERTS).astype(F32)


def _post_out(n_tok):
    specs = [pl.BlockSpec((TM, D_MODEL), lambda i: (i, 0)),
             pl.BlockSpec((TM, D_MODEL), lambda i: (i, 0)),
             pl.BlockSpec((TM, N_EXPERTS), lambda i: (i, 0)),
             pl.BlockSpec((TM, N_EXPERTS), lambda i: (i, 0))]
    shapes = [jax.ShapeDtypeStruct((n_tok, D_MODEL), F32),
              jax.ShapeDtypeStruct((n_tok, D_MODEL), F32),
              jax.ShapeDtypeStruct((n_tok, N_EXPERTS), BF16),
              jax.ShapeDtypeStruct((n_tok, N_EXPERTS), F32)]
    return specs, shapes


def _const_spec(shape):
    return pl.BlockSpec(shape, lambda i: tuple(0 for _ in shape))


def _post_l0(a_lat, a_ctx, f_lat, f_ctx, x, ctx, mod, norm_g, w_out_bf, router, n_lat, n_tok, tiles_per_batch):
    n_lat_tiles = n_lat // TM

    def lat(wd):
        return pl.BlockSpec((TM, wd), lambda i: (jnp.minimum(i, n_lat_tiles - 1), 0))

    def cx(wd):
        return pl.BlockSpec((TM, wd), lambda i: (jnp.maximum(i - n_lat_tiles, 0), 0))

    out_specs, out_shapes = _post_out(n_tok)
    return pl.pallas_call(
        functools.partial(_post_l0_kernel, n_lat_tiles=n_lat_tiles),
        grid=(n_tok // TM,),
        in_specs=[lat(NA_WIDTH), cx(NA_WIDTH), lat(FOURIER_WIDTH), cx(FOURIER_WIDTH),
                  lat(D_MODEL), cx(D_MODEL),
                  _mod_spec(2, tiles_per_batch), _mod_spec(3, tiles_per_batch), _mod_spec(4, tiles_per_batch),
                  _const_spec((1, D_MODEL)), _const_spec((D_MODEL, D_MODEL)),
                  _const_spec((D_MODEL, N_EXPERTS)), _const_spec((D_MODEL, N_EXPERTS)),
                  _const_spec((1, N_EXPERTS))],
        out_specs=out_specs, out_shape=out_shapes,
        compiler_params=_cparams(("arbitrary",)),
        name="post_mixer_l0",
    )(a_lat, a_ctx, f_lat, f_ctx, x, ctx, mod, mod, mod, norm_g.reshape(1, D_MODEL), w_out_bf, *router)


def _post_l1(gyt, x_all, mod, norm_g, w_glu_t_bf, b_glu, w_out_t_bf, router, n_lat, tiles_per_batch):
    out_specs, out_shapes = _post_out(n_lat)
    return pl.pallas_call(
        _post_l1_kernel,
        grid=(n_lat // TM,),
        in_specs=[pl.BlockSpec((D_MODEL, TM), lambda i: (0, i)),
                  pl.BlockSpec((TM, D_MODEL), lambda i: (i, 0)),
                  _mod_spec(2, tiles_per_batch), _mod_spec(3, tiles_per_batch), _mod_spec(4, tiles_per_batch),
                  _const_spec((1, D_MODEL)), _const_spec((D_MODEL, D_MODEL)), _const_spec((D_MODEL, 1)),
                  _const_spec((D_MODEL, D_MODEL)),
                  _const_spec((D_MODEL, N_EXPERTS)), _const_spec((D_MODEL, N_EXPERTS)),
                  _const_spec((1, N_EXPERTS))],
        out_specs=out_specs, out_shape=out_shapes,
        compiler_params=_cparams(("arbitrary",)),
        name="post_mixer_l1",
    )(gyt, x_all, mod, mod, mod, norm_g.reshape(1, D_MODEL), w_glu_t_bf, b_glu.reshape(D_MODEL, 1),
      w_out_t_bf, *router)


CHUNK = 8
TILE_ROWS = TM * TOP_K + N_EXPERTS * CHUNK
TILE_CHUNKS = TILE_ROWS // CHUNK
BLOCK_CHUNKS = EXPERT_ROWS // CHUNK
SORT_ROWS = 256
HALF = D_MODEL // 2


def _pack_bf16_pairs(y):
    bits = pltpu.bitcast(y, jnp.uint32)
    return (bits[:, :HALF] & jnp.uint32(0xFFFF0000)) | (bits[:, HALF:] >> 16)


def _unpack_bf16_pairs(w):
    hi = pltpu.bitcast(w & jnp.uint32(0xFFFF0000), F32).astype(BF16)
    lo = pltpu.bitcast(w << 16, F32).astype(BF16)
    return hi, lo


def _sort_kernel(h_ref, m_ref, g_ref, hs_ref, pos_ref, gate_ref, cnt_ref):
    m = m_ref[...]
    mf = m.astype(F32)
    r = lax.broadcasted_iota(jnp.int32, (TM, TM), 0)
    c = lax.broadcasted_iota(jnp.int32, (TM, TM), 1)
    rank = _dot((r > c).astype(F32).astype(BF16), m)
    cnt = mf.sum(axis=0, keepdims=True)
    n8 = jnp.floor((cnt + (CHUNK - 1)) * (1.0 / CHUNK))
    er = lax.broadcasted_iota(jnp.int32, (N_EXPERTS, N_EXPERTS), 0)
    ec = lax.broadcasted_iota(jnp.int32, (N_EXPERTS, N_EXPERTS), 1)
    upper = (er < ec).astype(F32).astype(BF16)
    off = CHUNK * _dot(jnp.broadcast_to(n8, (8, N_EXPERTS)).astype(BF16), upper)[0:1]
    pos = rank + off
    slot = _dot(m, upper)
    chosen = mf > 0.5
    g = g_ref[...]
    lane = lax.broadcasted_iota(jnp.int32, pos_ref.shape, 1)
    p_out = jnp.zeros(pos_ref.shape, F32)
    g_out = jnp.zeros(gate_ref.shape, F32)
    for k in range(TOP_K):
        sel = chosen & (slot == float(k))
        pk = jnp.where(sel, pos, 0.0).sum(axis=-1, keepdims=True)
        gk = jnp.where(sel, g, 0.0).sum(axis=-1, keepdims=True)
        p_out = jnp.where(lane == k, pk, p_out)
        g_out = jnp.where(lane == k, gk, g_out)
    pos_ref[...] = p_out
    gate_ref[...] = g_out
    cnt_ref[...] = jnp.zeros_like(cnt_ref)
    cnt_ref[0:1, 0:N_EXPERTS] = cnt

    pos_t = p_out.T
    hb = h_ref[...].astype(BF16)
    for rc in range(TILE_ROWS // SORT_ROWS):
        rid = (lax.broadcasted_iota(jnp.int32, (SORT_ROWS, TM), 0) + rc * SORT_ROWS).astype(F32)
        hit = rid == pos_t[0:1, :]
        for k in range(1, TOP_K):
            hit = hit | (rid == pos_t[k:k + 1, :])
        onehot = hit.astype(F32).astype(BF16)
        hs_ref[rc * SORT_ROWS:(rc + 1) * SORT_ROWS, :] = _pack_bf16_pairs(_dot(onehot, hb))


def _sort(h2, m_sel, gates):
    n_tok = h2.shape[0]
    n_tiles = n_tok // TM
    tok = pl.BlockSpec((TM, N_EXPERTS), lambda i: (i, 0))
    wide = pl.BlockSpec((TM, 128), lambda i: (i, 0))
    return pl.pallas_call(
        _sort_kernel,
        grid=(n_tiles,),
        in_specs=[pl.BlockSpec((TM, D_MODEL), lambda i: (i, 0)), tok, tok],
        out_specs=[pl.BlockSpec((TILE_ROWS, HALF), lambda i: (i, 0)), wide, wide,
                   pl.BlockSpec((8, 128), lambda i: (i, 0))],
        out_shape=[jax.ShapeDtypeStruct((n_tiles * TILE_ROWS, HALF), jnp.uint32),
                   jax.ShapeDtypeStruct((n_tok, 128), F32),
                   jax.ShapeDtypeStruct((n_tok, 128), F32),
                   jax.ShapeDtypeStruct((n_tiles * 8, 128), F32)],
        compiler_params=_cparams(("arbitrary",)),
        name="moe_sort",
    )(h2, m_sel, gates)


def _chunk_tables(cnt, n_blocks):
    n_tiles = cnt.shape[0]
    n8 = (cnt + CHUNK - 1) // CHUNK
    lend = jnp.cumsum(n8, axis=1)
    lstart = lend - n8
    cum = jnp.cumsum(n8, axis=0)
    before = cum - n8
    total = cum[-1]
    nblk = (total + BLOCK_CHUNKS - 1) // BLOCK_CHUNKS
    bend = jnp.cumsum(nblk)
    gstart = (bend - nblk) * BLOCK_CHUNKS
    n_used = jnp.maximum(bend[-1], 1)
    blk = jnp.arange(n_blocks, dtype=jnp.int32)
    block_expert = (bend[None, :] <= jnp.minimum(blk, n_used - 1)[:, None]).sum(axis=1).astype(jnp.int32)
    block_expert = jnp.minimum(block_expert, N_EXPERTS - 1)
    experts = jnp.arange(N_EXPERTS, dtype=jnp.int32)
    lc = jnp.arange(TILE_CHUNKS, dtype=jnp.int32)
    e_of = (lend[:, None, :] <= lc[None, :, None]).sum(axis=-1).astype(jnp.int32)
    shift = gstart[None, :] + before - lstart
    pick = e_of[:, :, None] == experts[None, None, :]
    to_sorted = jnp.where(pick, shift[:, None, :], 0).sum(axis=-1) + jnp.where(e_of < N_EXPERTS, lc[None, :], 0)
    of_block = block_expert[:, None] == experts[None, :]
    gstart_b = jnp.where(of_block, gstart[None, :], 0).sum(axis=-1)
    total_b = jnp.where(of_block, total[None, :], 0).sum(axis=-1)

    def per_block(table):
        return jnp.where(of_block[:, :, None], table.T[None, :, :], 0).sum(axis=1)

    cum_b, lstart_b, before_b = per_block(cum), per_block(lstart), per_block(before)
    rel = (blk[:, None] * BLOCK_CHUNKS + jnp.arange(BLOCK_CHUNKS, dtype=jnp.int32)[None, :]) - gstart_b[:, None]
    valid = (blk[:, None] < n_used) & (rel < total_b[:, None])
    tile = jnp.minimum((cum_b[:, None, :] <= rel[:, :, None]).sum(axis=-1), n_tiles - 1).astype(jnp.int32)
    of_tile = tile[:, :, None] == jnp.arange(n_tiles, dtype=jnp.int32)[None, None, :]
    inside = jnp.where(of_tile, (lstart_b - before_b)[:, None, :], 0).sum(axis=-1)
    from_tiles = jnp.where(valid, tile * TILE_CHUNKS + inside + rel, TILE_CHUNKS - 1)
    return (block_expert, n_used.reshape(1).astype(jnp.int32), from_tiles.reshape(-1).astype(jnp.int32),
            to_sorted.reshape(-1).astype(jnp.int32))


def _chunk_rows(chunk):
    start = chunk * CHUNK
    return pl.ds(start if isinstance(start, int) else pl.multiple_of(start, CHUNK), CHUNK)


def _chunk_copy(src_ref, src_chunk, dst_ref, dst_chunk, sem):
    return pltpu.make_async_copy(src_ref.at[_chunk_rows(src_chunk), :], dst_ref.at[_chunk_rows(dst_chunk), :], sem)


def _expert_kernel(be_ref, nu_ref, src_ref, hs_ref, wgu_ref, bgu_ref, wd_ref, bd_ref, o_ref,
                   wgu_bf, wd_bf, x_buf, sems):
    i = pl.program_id(0)
    n_used = nu_ref[0]

    def gather(block, slot, start):
        for cidx in range(BLOCK_CHUNKS):
            cp = _chunk_copy(hs_ref, src_ref[block * BLOCK_CHUNKS + cidx], x_buf.at[slot], cidx, sems.at[slot])
            if start:
                cp.start()
            else:
                cp.wait()

    @pl.when(i == 0)
    def _():
        gather(0, 0, True)

    for slot in range(2):
        @pl.when((i + 1 < n_used) & ((i + 1) % 2 == slot))
        def _(slot=slot):
            gather(i + 1, slot, True)

    @pl.when((i == 0) | (be_ref[i] != be_ref[jnp.maximum(i - 1, 0)]))
    def _():
        wgu_bf[...] = wgu_ref[0].astype(BF16)
        wd_bf[...] = wd_ref[0].astype(BF16)

    for slot in range(2):
        @pl.when((i < n_used) & (i % 2 == slot))
        def _(slot=slot):
            gather(i, slot, False)
            x_hi, x_lo = _unpack_bf16_pairs(x_buf[slot])
            gu = _dot(x_hi, wgu_bf[:HALF, :]) + _dot(x_lo, wgu_bf[HALF:, :]) + bgu_ref[0]
            gate = jnp.minimum(gu[:, :D_MODEL], SWIGLU_LIMIT)
            up = jnp.clip(gu[:, D_MODEL:], -SWIGLU_LIMIT, SWIGLU_LIMIT)
            act = (up + 1.0) * (gate * jax.nn.sigmoid(SWIGLU_ALPHA * gate))
            y = _dot(act.astype(BF16), wd_bf[...]) + bd_ref[0]
            o_ref[...] = _pack_bf16_pairs(y.astype(BF16).astype(F32))

    @pl.when(i >= n_used)
    def _():
        o_ref[...] = jnp.zeros_like(o_ref)


def _experts(hs, block_expert, n_used, from_tiles, wgu, bgu, wd, bd):
    n_blocks = block_expert.shape[0]

    def row_map(i, be, nu, src):
        return (i, 0)

    def exp_map(i, be, nu, src):
        return (be[i], 0, 0)

    grid_spec = pltpu.PrefetchScalarGridSpec(
        num_scalar_prefetch=3,
        grid=(n_blocks,),
        in_specs=[pl.BlockSpec(memory_space=pl.ANY),
                  pl.BlockSpec((1, D_MODEL, 2 * D_MODEL), exp_map),
                  pl.BlockSpec((1, 1, 2 * D_MODEL), exp_map),
                  pl.BlockSpec((1, D_MODEL, D_MODEL), exp_map),
                  pl.BlockSpec((1, 1, D_MODEL), exp_map)],
        out_specs=pl.BlockSpec((EXPERT_ROWS, HALF), row_map),
        scratch_shapes=[pltpu.VMEM((D_MODEL, 2 * D_MODEL), BF16), pltpu.VMEM((D_MODEL, D_MODEL), BF16),
                        pltpu.VMEM((2, EXPERT_ROWS, HALF), jnp.uint32), pltpu.SemaphoreType.DMA((2,))],
    )
    return pl.pallas_call(
        _expert_kernel,
        grid_spec=grid_spec,
        out_shape=jax.ShapeDtypeStruct((n_blocks * EXPERT_ROWS, HALF), jnp.uint32),
        compiler_params=pltpu.CompilerParams(dimension_semantics=("arbitrary",),
                                             vmem_limit_bytes=EXPERT_VMEM_LIMIT),
        name="moe_experts",
    )(block_expert, n_used, from_tiles, hs, wgu, bgu.reshape(N_EXPERTS, 1, 2 * D_MODEL), wd,
      bd.reshape(N_EXPERTS, 1, D_MODEL))


def _combine_kernel(dst_ref, ys_ref, pos_ref, gate_ref, x_ref, g2_ref, ng_ref, o_ref, y_buf, sems, *, final_norm):
    i = pl.program_id(0)
    n_tiles = pl.num_programs(0)

    def gather(tile, slot, start):
        def body(cidx, carry):
            cp = _chunk_copy(ys_ref, dst_ref[tile * TILE_CHUNKS + cidx], y_buf.at[slot], cidx, sems.at[slot])
            if start:
                cp.start()
            else:
                cp.wait()
            return carry
        lax.fori_loop(0, TILE_CHUNKS, body, 0)

    @pl.when(i == 0)
    def _():
        gather(0, 0, True)

    for slot in range(2):
        @pl.when((i + 1 < n_tiles) & ((i + 1) % 2 == slot))
        def _(slot=slot):
            gather(i + 1, slot, True)

    pos = pos_ref[...]
    gate = gate_ref[...]
    for slot in range(2):
        @pl.when(i % 2 == slot)
        def _(slot=slot):
            gather(i, slot, False)
            acc_hi = jnp.zeros((TM, HALF), F32)
            acc_lo = jnp.zeros((TM, HALF), F32)
            for rc in range(TILE_ROWS // SORT_ROWS):
                rid = (lax.broadcasted_iota(jnp.int32, (TM, SORT_ROWS), 1) + rc * SORT_ROWS).astype(F32)
                w = jnp.where(rid == pos[:, 0:1], gate[:, 0:1], 0.0)
                for k in range(1, TOP_K):
                    w = jnp.where(rid == pos[:, k:k + 1], gate[:, k:k + 1], w)
                w = w.astype(BF16)
                y_hi, y_lo = _unpack_bf16_pairs(y_buf[slot, rc * SORT_ROWS:(rc + 1) * SORT_ROWS, :])
                acc_hi = acc_hi + _dot(w, y_hi)
                acc_lo = acc_lo + _dot(w, y_lo)
            x = x_ref[...] + g2_ref[0] * jnp.concatenate([acc_hi, acc_lo], axis=-1)
            if final_norm:
                ms = jnp.mean(x * x, axis=-1, keepdims=True)
                x = x * lax.rsqrt(ms + EPS) * ng_ref[...]
            o_ref[...] = x


def _combine(ys, to_sorted, pos4, gate4, x_res, mod, norm_g, tiles_per_batch, final_norm):
    n_tok = x_res.shape[0]
    tok = pl.BlockSpec((TM, D_MODEL), lambda i, dst: (i, 0))
    wide = pl.BlockSpec((TM, 128), lambda i, dst: (i, 0))
    tiles, bsz = tiles_per_batch
    grid_spec = pltpu.PrefetchScalarGridSpec(
        num_scalar_prefetch=1,
        grid=(n_tok // TM,),
        in_specs=[pl.BlockSpec(memory_space=pl.ANY), wide, wide, tok,
                  pl.BlockSpec((1, 1, D_MODEL), lambda i, dst: (jnp.minimum(i // tiles, bsz) * 6 + 5, 0, 0)),
                  pl.BlockSpec((1, D_MODEL), lambda i, dst: (0, 0))],
        out_specs=tok,
        scratch_shapes=[pltpu.VMEM((2, TILE_ROWS, HALF), jnp.uint32), pltpu.SemaphoreType.DMA((2,))],
    )
    return pl.pallas_call(
        functools.partial(_combine_kernel, final_norm=final_norm),
        grid_spec=grid_spec,
        out_shape=jax.ShapeDtypeStruct((n_tok, D_MODEL), F32),
        compiler_params=_cparams(("arbitrary",)),
        name="moe_combine",
    )(to_sorted, ys, pos4, gate4, x_res, mod, norm_g.reshape(1, D_MODEL))


def _moe(h2, m_sel, gates, x_res, mod, moe_w, tiles_per_batch, norm_g, final_norm):
    wgu, bgu, wd, bd = moe_w
    n_tiles = h2.shape[0] // TM
    n_blocks = -(-n_tiles * TILE_CHUNKS // BLOCK_CHUNKS) + N_EXPERTS
    hs, pos4, gate4, cnt = _sort(h2, m_sel, gates)
    cnt_i = cnt.reshape(n_tiles, 8, 128)[:, 0, :N_EXPERTS].astype(jnp.int32)
    block_expert, n_used, from_tiles, to_sorted = _chunk_tables(cnt_i, n_blocks)
    ys = _experts(hs, block_expert, n_used, from_tiles, wgu, bgu, wd, bd)
    return _combine(ys, to_sorted, pos4, gate4, x_res, mod, norm_g, tiles_per_batch, final_norm)


def _s5_tables(lam_re, lam_im, log_dt, b_re, b_im, c_re, c_im, d_skip):
    q = S5_Q
    lam = lax.complex(jnp.minimum(lam_re.astype(F32), LAMBDA_RE_MAX), lam_im.astype(F32))
    dt = jnp.exp(log_dt.astype(F32))[..., None]
    a = jnp.exp(lam * dt)
    bbar = ((a - 1) / lam)[..., None] * lax.complex(b_re.astype(F32), b_im.astype(F32))
    cc = lax.complex(c_re.astype(F32), c_im.astype(F32))
    steps = jnp.arange(q + 1, dtype=F32)
    pw = jnp.exp((lam * dt)[:, :, None, :] * steps[None, None, :, None])
    kern = jnp.einsum('dgop,dgkp,dgpc->dgcok', cc, pw[:, :, :q], bbar).real
    eye = jnp.eye(SSM_GROUP, dtype=F32)
    k_fwd = kern[0].at[..., 0].add(kern[1][..., 0] + d_skip.astype(F32)[:, :, None] * eye[None])
    k_bwd = jnp.concatenate([jnp.zeros_like(kern[1][..., :1]), kern[1][..., :0:-1]], axis=-1)
    k_fwd = k_fwd.reshape(SSM_GROUPS, SSM_GROUP * SSM_GROUP, q)
    k_bwd = k_bwd.reshape(SSM_GROUPS, SSM_GROUP * SSM_GROUP, q)
    def both(f, b):
        return jnp.concatenate([f, b], axis=-1)

    e_in = both(pw[0][:, ::-1][:, 1:], pw[1][:, :q])
    b_in = both(jnp.transpose(bbar[0], (0, 2, 1)), jnp.transpose(bbar[1], (0, 2, 1)))
    e_out = jnp.transpose(both(pw[0][:, 1:], pw[1][:, ::-1][:, :q]), (0, 2, 1))
    c_out = jnp.transpose(both(cc[0], cc[1]), (0, 2, 1))
    a_q = both(pw[0][:, q], pw[1][:, q]).reshape(SSM_GROUPS, 1, 2 * SSM_STATE)
    parts = [t for z in (e_in, b_in, e_out, c_out, a_q) for t in (z.real, z.imag)]
    return [k_fwd, k_bwd] + parts


def _s5_kernel(u_ref, kf_ref, kb_ref, er_ref, ei_ref, br_ref, bi_ref, pr_ref, pi_ref, cr_ref, ci_ref,
               ar_ref, ai_ref, o_ref, w_ref, s_ref, h_ref, *, bsz, n_lat_chunks, n_ctx_chunks):
    q = S5_Q
    half = 2 * SSM_STATE
    n_lat = bsz * n_lat_chunks

    causal = (lax.broadcasted_iota(jnp.int32, (q, q), 1) >= lax.broadcasted_iota(jnp.int32, (q, q), 0))

    def build(c, carry):
        for o in range(SSM_GROUP):
            row = pl.ds(c * SSM_GROUP + o, 1)
            kf = pltpu.roll(jnp.broadcast_to(kf_ref[0, row, :], (q, q)), 0, 1, stride=1, stride_axis=0)
            kb = pltpu.roll(jnp.broadcast_to(kb_ref[0, row, :], (q, q)), 0, 1, stride=1, stride_axis=0)
            w_ref[pl.ds(pl.multiple_of(c * q, q), q), o * q:(o + 1) * q] = jnp.where(causal, kf, kb).astype(BF16)
        return carry

    lax.fori_loop(0, SSM_GROUP, build, 0)

    u = u_ref[...]
    er, ei = er_ref[0], ei_ref[0]
    s_re = None
    for c in range(SSM_GROUP):
        br, bi = br_ref[0, c:c + 1, :], bi_ref[0, c:c + 1, :]
        w_re = (er * br - ei * bi).astype(BF16)
        w_im = (er * bi + ei * br).astype(BF16)
        s_re = _dot(u[c], w_re) if s_re is None else s_re + _dot(u[c], w_re)
        s_im = _dot(u[c], w_im) if c == 0 else s_im + _dot(u[c], w_im)
    s_ref[0] = s_re
    s_ref[1] = s_im

    fwd = lax.broadcasted_iota(jnp.int32, (bsz, half), 1) < SSM_STATE
    ar = jnp.broadcast_to(ar_ref[0], (bsz, half))
    ai = jnp.broadcast_to(ai_ref[0], (bsz, half))

    def chunk_rows(base, j, per_sample):
        return pl.ds(base + j, bsz, stride=per_sample)

    def load(base, j, n):
        rows_f = chunk_rows(base, j, n)
        rows_b = chunk_rows(base, n - 1 - j, n)
        return (jnp.where(fwd, s_ref[0, rows_f, :], s_ref[0, rows_b, :]),
                jnp.where(fwd, s_ref[1, rows_f, :], s_ref[1, rows_b, :]))

    def step(hr, hi, sr, si):
        return ar * hr - ai * hi + sr, ar * hi + ai * hr + si

    def ctx_body(j, carry):
        return step(*carry, *load(n_lat, j, n_ctx_chunks))

    def lat_body(j, carry):
        hr, hi = carry
        rows_f = chunk_rows(0, j, n_lat_chunks)
        rows_b = chunk_rows(0, n_lat_chunks - 1 - j, n_lat_chunks)
        h_ref[0, rows_f, :] = jnp.where(fwd, hr, 0.0)
        h_ref[1, rows_f, :] = jnp.where(fwd, hi, 0.0)
        h_ref[2, rows_b, :] = jnp.where(fwd, 0.0, hr)
        h_ref[3, rows_b, :] = jnp.where(fwd, 0.0, hi)
        return step(hr, hi, *load(0, j, n_lat_chunks))

    zero = jnp.zeros((bsz, half), F32)
    carry = lax.fori_loop(0, n_ctx_chunks, ctx_body, (zero, zero))
    lax.fori_loop(0, n_lat_chunks, lat_body, carry)

    ucat = jnp.concatenate([u[c][:n_lat] for c in range(SSM_GROUP)], axis=-1)
    y = _dot(ucat, w_ref[...])
    h_re = (h_ref[0] + h_ref[2]).astype(BF16)
    h_im = (h_ref[1] + h_ref[3]).astype(BF16)
    pr, pi = pr_ref[0], pi_ref[0]
    cr_all, ci_all = cr_ref[0], ci_ref[0]
    for o in range(SSM_GROUP):
        cr, ci = cr_all[:, o:o + 1], ci_all[:, o:o + 1]
        w_re = (pr * cr - pi * ci).astype(BF16)
        w_im = (pr * ci + pi * cr).astype(BF16)
        y_o = y[:, o * q:(o + 1) * q] + _dot(h_re, w_re) - _dot(h_im, w_im)
        o_ref[o] = jax.nn.gelu(y_o).astype(o_ref.dtype)


def _s5(u3, tables, bsz, n_lat_chunks, n_ctx_chunks):
    q = S5_Q
    n_chunks = u3.shape[1]
    n_lat = bsz * n_lat_chunks
    assert n_chunks == n_lat + bsz * n_ctx_chunks

    def gspec(*shape):
        return pl.BlockSpec((1,) + shape, lambda g: (g,) + tuple(0 for _ in shape))

    return pl.pallas_call(
        functools.partial(_s5_kernel, bsz=bsz, n_lat_chunks=n_lat_chunks, n_ctx_chunks=n_ctx_chunks),
        grid=(SSM_GROUPS,),
        in_specs=[pl.BlockSpec((SSM_GROUP, n_chunks, q), lambda g: (g, 0, 0)),
                  gspec(SSM_GROUP * SSM_GROUP, q), gspec(SSM_GROUP * SSM_GROUP, q),
                  gspec(q, 2 * SSM_STATE), gspec(q, 2 * SSM_STATE),
                  gspec(SSM_GROUP, 2 * SSM_STATE), gspec(SSM_GROUP, 2 * SSM_STATE),
                  gspec(2 * SSM_STATE, q), gspec(2 * SSM_STATE, q),
                  gspec(2 * SSM_STATE, SSM_GROUP), gspec(2 * SSM_STATE, SSM_GROUP),
                  gspec(1, 2 * SSM_STATE), gspec(1, 2 * SSM_STATE)],
        out_specs=pl.BlockSpec((SSM_GROUP, n_lat, q), lambda g: (g, 0, 0)),
        out_shape=jax.ShapeDtypeStruct((SSM_GROUPS * SSM_GROUP, n_lat, q), BF16),
        scratch_shapes=[pltpu.VMEM((SSM_GROUP * q, SSM_GROUP * q), BF16),
                        pltpu.VMEM((2, n_chunks, 2 * SSM_STATE), F32),
                        pltpu.VMEM((4, n_lat, 2 * SSM_STATE), F32)],
        compiler_params=_cparams(("arbitrary",)),
        name="s5_scan",
    )(u3, *tables)


def _moe_weights(w_gate_up, b_gate_up, w_down, b_down):
    return w_gate_up, b_gate_up, w_down, b_down


def kernel(x, c, ctx, c_ctx, l0_ada_w, l0_ada_b, l0_norm_mix, l0_w_in, l0_rpb, l0_w_fourier, l0_w_out, l0_norm_ffn, l0_router_w, l0_router_b, l0_w_gate_up, l0_b_gate_up, l0_w_down, l0_b_down, l1_ada_w, l1_ada_b, l1_norm_mix, l1_w_in, l1_lambda_re, l1_lambda_im, l1_log_dt, l1_b_re, l1_b_im, l1_c_re, l1_c_im, l1_d_skip, l1_w_glu, l1_b_glu, l1_w_out, l1_norm_ffn, l1_router_w, l1_router_b, l1_w_gate_up, l1_b_gate_up, l1_w_down, l1_b_down, final_norm):
    bsz, seq, _ = x.shape
    ctx_len = ctx.shape[1]
    n_lat = bsz * seq
    n_ctx = bsz * ctx_len
    n_tok = n_lat + n_ctx
    assert seq % TM == 0 and n_ctx % TM == 0
    x2 = x.reshape(n_lat, D_MODEL)
    ctx2 = ctx.reshape(n_ctx, D_MODEL)

    n_mod = bsz + 1
    c_pad = jnp.zeros((-(-n_mod // 8) * 8, D_MODEL), F32).at[:bsz].set(c).at[bsz].set(c_ctx)
    mod0 = _ada(c_pad, l0_ada_w, l0_ada_b)[:n_mod].reshape(n_mod * 6, 1, D_MODEL)
    mod1 = _ada(c_pad, l1_ada_w, l1_ada_b)[:n_mod].reshape(n_mod * 6, 1, D_MODEL)
    tpb = (seq // TM, bsz)

    q, k, v, f = _modproj([x2, ctx2], mod0, l0_norm_mix, l0_w_in.astype(BF16),
                          (NA_WIDTH, NA_WIDTH, NA_WIDTH, FOURIER_WIDTH), n_lat, n_tok, tpb,
                          q_scale=HEAD_DIM ** -0.5)
    a_lat, a_ctx = _attention(q, k, v, l0_rpb, bsz, seq, ctx_len)
    bdw = (jnp.eye(FOURIER_GROUPS, dtype=F32)[:, None, :, None] * l0_w_fourier[:, :, None, :]).reshape(
        FOURIER_WIDTH, FOURIER_WIDTH).astype(BF16)
    f_lat = _fourier(f, bdw, bsz, seq, 0, TM)
    f_ctx = _fourier(f, bdw, bsz, ctx_len, n_lat // ctx_len, ctx_len)
    x_all, h2, m_sel, gates = _post_l0(a_lat, a_ctx, f_lat, f_ctx, x2, ctx2, mod0, l0_norm_ffn,
                                       l0_w_out.astype(BF16), _router_args(l0_router_w, l0_router_b),
                                       n_lat, n_tok, tpb)
    x_all = _moe(h2, m_sel, gates, x_all, mod0, _moe_weights(l0_w_gate_up, l0_b_gate_up, l0_w_down, l0_b_down),
                 tpb, final_norm, False)

    assert seq % S5_Q == 0 and ctx_len % S5_Q == 0
    ut = _modproj_t(x_all, mod1, l1_norm_mix, l1_w_in.T.astype(BF16), n_tok, tpb)
    tables = _s5_tables(l1_lambda_re, l1_lambda_im, l1_log_dt, l1_b_re, l1_b_im, l1_c_re, l1_c_im, l1_d_skip)
    yt = _s5(ut.reshape(D_MODEL, n_tok // S5_Q, S5_Q), tables, bsz, seq // S5_Q, ctx_len // S5_Q)
    gy = yt.reshape(D_MODEL, n_lat)
    x1, h2, m_sel, gates = _post_l1(gy, x_all, mod1, l1_norm_ffn, l1_w_glu.T.astype(BF16), l1_b_glu,
                                    l1_w_out.T.astype(BF16), _router_args(l1_router_w, l1_router_b), n_lat, tpb)
    out = _moe(h2, m_sel, gates, x1, mod1, _moe_weights(l1_w_gate_up, l1_b_gate_up, l1_w_down, l1_b_down),
               tpb, final_norm, True)
    return out.reshape(bsz, seq, D_MODEL)
```

```python
import functools
import math

import numpy as np
import jax
import jax.numpy as jnp
from jax import lax
from jax.experimental import pallas as pl
from jax.experimental.pallas import tpu as pltpu

F32 = jnp.float32
BF16 = jnp.bfloat16

D_MODEL = 1024
GRID_W = 64
HEAD_DIM = 64
NA_WIDTH = 512
NA_HEADS = 8
MAX_KR = 8
KC = 16
FOURIER_DIM = 64
FOURIER_GROUPS = 8
FOURIER_WIDTH = 512
SSM_GROUP = 16
SSM_GROUPS = 64
SSM_STATE = 64
LAMBDA_RE_MAX = -1e-4
N_EXPERTS = 32
TOP_K = 4
SWIGLU_ALPHA = 1.702
SWIGLU_LIMIT = 7.0
EPS = 1e-6

TM = 512
EXPERT_ROWS = 512
S5_Q = 128
VMEM_LIMIT = 48 * 1024 * 1024
EXPERT_VMEM_LIMIT = 58 * 1024 * 1024


def _cparams(sem):
    return pltpu.CompilerParams(dimension_semantics=sem, vmem_limit_bytes=VMEM_LIMIT)


def _split_bf16(a):
    hi = a.astype(BF16)
    lo = (a - hi.astype(F32)).astype(BF16)
    return hi, lo


def _dot(a, b):
    return jnp.dot(a, b, preferred_element_type=F32)


def _dot_nt(a, b):
    return lax.dot_general(a, b, (((1,), (1,)), ((), ())), preferred_element_type=F32)


def _ada_kernel(c_ref, w_ref, b_ref, o_ref):
    c = c_ref[...]
    s = c * jax.nn.sigmoid(c)
    s_hi, s_lo = _split_bf16(s)
    w_hi, w_lo = _split_bf16(w_ref[...])
    o_ref[...] = _dot(s_hi, w_hi) + _dot(s_lo, w_hi) + _dot(s_hi, w_lo) + b_ref[...]


def _ada(c_pad, ada_w, ada_b):
    n = ada_w.shape[1]
    tn = 1024
    return pl.pallas_call(
        _ada_kernel,
        grid=(n // tn,),
        in_specs=[pl.BlockSpec((c_pad.shape[0], D_MODEL), lambda j: (0, 0)),
                  pl.BlockSpec((D_MODEL, tn), lambda j: (0, j)),
                  pl.BlockSpec((1, tn), lambda j: (0, j))],
        out_specs=pl.BlockSpec((c_pad.shape[0], tn), lambda j: (0, j)),
        out_shape=jax.ShapeDtypeStruct((c_pad.shape[0], n), F32),
        compiler_params=_cparams(("arbitrary",)),
        name="ada_mod",
    )(c_pad, ada_w, ada_b.reshape(1, n))


def _mod_spec(which, tiles_per_batch):
    tiles, bsz = tiles_per_batch
    return pl.BlockSpec((1, 1, D_MODEL), lambda i: (jnp.minimum(i // tiles, bsz) * 6 + which, 0, 0))


def _modulate(x, g, sh, sc):
    ms = jnp.mean(x * x, axis=-1, keepdims=True)
    y = x * lax.rsqrt(ms + EPS) * g
    return y * (1.0 + sc) + sh


def _modproj_kernel(*refs, n_lat_tiles, two_src, widths, q_scale):
    if two_src:
        x_ref, c_ref, sh_ref, sc_ref, g_ref, w_ref = refs[:6]
        o_refs = refs[6:]
        i = pl.program_id(0)
        x = jnp.where(i < n_lat_tiles, x_ref[...], c_ref[...])
    else:
        x_ref, sh_ref, sc_ref, g_ref, w_ref = refs[:5]
        o_refs = refs[5:]
        x = x_ref[...]
    h = _modulate(x, g_ref[...], sh_ref[0], sc_ref[0]).astype(BF16)
    off = 0
    for j, (o_ref, wd) in enumerate(zip(o_refs, widths)):
        y = _dot(h, w_ref[:, off:off + wd])
        if j == 0 and q_scale != 1.0:
            y = y * q_scale
        o_ref[...] = y.astype(o_ref.dtype)
        off += wd


def _modproj(x_srcs, mod, norm_g, w_bf, widths, n_lat, n_tok, tiles_per_batch, q_scale=1.0):
    n_tiles = n_tok // TM
    n_lat_tiles = n_lat // TM
    two_src = len(x_srcs) == 2
    n_out = w_bf.shape[1]
    if two_src:
        x_specs = [pl.BlockSpec((TM, D_MODEL), lambda i: (jnp.minimum(i, n_lat_tiles - 1), 0)),
                   pl.BlockSpec((TM, D_MODEL), lambda i: (jnp.maximum(i - n_lat_tiles, 0), 0))]
    else:
        x_specs = [pl.BlockSpec((TM, D_MODEL), lambda i: (i, 0))]
    return pl.pallas_call(
        functools.partial(_modproj_kernel, n_lat_tiles=n_lat_tiles, two_src=two_src,
                          widths=tuple(widths), q_scale=q_scale),
        grid=(n_tiles,),
        in_specs=x_specs + [_mod_spec(0, tiles_per_batch), _mod_spec(1, tiles_per_batch),
                            pl.BlockSpec((1, D_MODEL), lambda i: (0, 0)),
                            pl.BlockSpec((D_MODEL, n_out), lambda i: (0, 0))],
        out_specs=[pl.BlockSpec((TM, wd), lambda i: (i, 0)) for wd in widths],
        out_shape=[jax.ShapeDtypeStruct((n_tok, wd), BF16) for wd in widths],
        compiler_params=_cparams(("arbitrary",)),
        name="modulate_in_proj",
    )(*x_srcs, mod, mod, norm_g.reshape(1, D_MODEL), w_bf)


def _modproj_t_kernel(x_ref, sh_ref, sc_ref, g_ref, wt_ref, o_ref):
    h = _modulate(x_ref[...], g_ref[...], sh_ref[0], sc_ref[0]).astype(BF16)
    o_ref[...] = _dot_nt(wt_ref[...], h).astype(o_ref.dtype)


def _modproj_t(x_all, mod, norm_g, wt_bf, n_tok, tiles_per_batch):
    n_out = wt_bf.shape[0]
    return pl.pallas_call(
        _modproj_t_kernel,
        grid=(n_tok // TM,),
        in_specs=[pl.BlockSpec((TM, D_MODEL), lambda i: (i, 0)),
                  _mod_spec(0, tiles_per_batch), _mod_spec(1, tiles_per_batch),
                  pl.BlockSpec((1, D_MODEL), lambda i: (0, 0)),
                  pl.BlockSpec((n_out, D_MODEL), lambda i: (0, 0))],
        out_specs=pl.BlockSpec((n_out, TM), lambda i: (0, i)),
        out_shape=jax.ShapeDtypeStruct((n_out, n_tok), BF16),
        compiler_params=_cparams(("arbitrary",)),
        name="modulate_in_proj_t",
    )(x_all, mod, mod, norm_g.reshape(1, D_MODEL), wt_bf)


Q_ROWS = 4
BAND_ROWS = Q_ROWS + MAX_KR


def _na_bias_table(rpb, rows):
    qc = np.arange(GRID_W)[:, None]
    kc = np.arange(GRID_W)[None, :]
    win0 = np.clip(qc - KC // 2, 0, GRID_W - KC)
    col_valid = (kc >= win0) & (kc < win0 + KC)
    col_off = np.clip(kc - qc, -(KC - 1), KC - 1) + KC - 1
    cb = rpb.astype(F32)[:, :, col_off]
    cb = jnp.where(jnp.asarray(col_valid)[None, None], cb, -jnp.inf)
    n_blk = rows // Q_ROWS
    idx = np.zeros((3, Q_ROWS, BAND_ROWS), np.int64)
    ok = np.zeros((3, Q_ROWS, BAND_ROWS), bool)
    for cls, blk in enumerate((0, 1, n_blk - 1)):
        band0 = int(np.clip(blk * Q_ROWS - MAX_KR // 2, 0, rows - BAND_ROWS))
        for rho in range(Q_ROWS):
            r = blk * Q_ROWS + rho
            r0 = int(np.clip(r - MAX_KR // 2, 0, rows - MAX_KR))
            for kap in range(BAND_ROWS):
                kr = band0 + kap
                ok[cls, rho, kap] = r0 <= kr < r0 + MAX_KR
                idx[cls, rho, kap] = np.clip(kr - r + MAX_KR - 1, 0, 2 * MAX_KR - 2)
    t = cb[:, idx]
    t = jnp.where(jnp.asarray(ok)[None, :, :, :, None, None], t, -jnp.inf)
    t = jnp.transpose(t, (1, 0, 2, 4, 3, 5))
    return t.reshape(3, NA_HEADS, Q_ROWS * GRID_W, BAND_ROWS * GRID_W)


def _attn_kernel(q_ref, k_ref, v_ref, qc_ref, kc_ref, vc_ref, bias_ref, o_ref, oc_ref, *, rows):
    lane = lax.broadcasted_iota(jnp.int32, (1, 2 * HEAD_DIM), 1)
    first = lane < HEAD_DIM
    kc = kc_ref[...]
    vc = vc_ref[...]
    zero = jnp.zeros((), BF16)

    def softmax_pv(s_list, v_list):
        m = s_list[0].max(axis=-1, keepdims=True)
        for s in s_list[1:]:
            m = jnp.maximum(m, s.max(axis=-1, keepdims=True))
        den = None
        acc = None
        for s, vv in zip(s_list, v_list):
            p = jnp.exp(s - m)
            ps = p.sum(axis=-1, keepdims=True)
            den = ps if den is None else den + ps
            pv = _dot(p.astype(BF16), vv)
            acc = pv if acc is None else acc + pv
        return acc / den

    n_blk = rows // Q_ROWS
    q_len = Q_ROWS * GRID_W

    def block_body(i, carry):
        band0 = jnp.clip(i * Q_ROWS - MAX_KR // 2, 0, rows - BAND_ROWS)
        cls = jnp.where(i == 0, 0, jnp.where(i == n_blk - 1, 2, 1))
        rows_q = pl.ds(pl.multiple_of(i * q_len, q_len), q_len)
        q_b = q_ref[rows_q, :]
        band = pl.ds(pl.multiple_of(band0 * GRID_W, GRID_W), BAND_ROWS * GRID_W)
        kb = k_ref[band, :]
        vb = v_ref[band, :]
        outs = []
        for hh in range(2):
            q_h = jnp.where(first if hh == 0 else ~first, q_b, zero)
            s_w = _dot_nt(q_h, kb) + bias_ref[cls, hh]
            s_c = _dot_nt(q_h, kc)
            outs.append(softmax_pv([s_w, s_c], [vb, vc]))
        o_ref[rows_q, :] = jnp.where(first, outs[0], outs[1]).astype(o_ref.dtype)
        return carry

    lax.fori_loop(0, n_blk, block_body, 0)

    q_c = qc_ref[...]
    outs = []
    for hh in range(2):
        q_h = jnp.where(first if hh == 0 else ~first, q_c, zero)
        outs.append(softmax_pv([_dot_nt(q_h, kc)], [vc]))
    oc_ref[...] = jnp.where(first, outs[0], outs[1]).astype(oc_ref.dtype)


def _attention(q, k, v, rpb, bsz, seq, ctx_len):
    rows = seq // GRID_W
    assert rows % Q_ROWS == 0 and rows >= 3 * Q_ROWS and seq % ctx_len == 0
    bias = _na_bias_table(rpb, rows)
    cb0 = bsz * seq // ctx_len
    lat = pl.BlockSpec((seq, 2 * HEAD_DIM), lambda hp, b: (b, hp))
    ctx = pl.BlockSpec((ctx_len, 2 * HEAD_DIM), lambda hp, b: (cb0 + b, hp))
    return pl.pallas_call(
        functools.partial(_attn_kernel, rows=rows),
        grid=(NA_HEADS // 2, bsz),
        in_specs=[lat, lat, lat, ctx, ctx, ctx,
                  pl.BlockSpec((3, 2, Q_ROWS * GRID_W, BAND_ROWS * GRID_W), lambda hp, b: (0, hp, 0, 0))],
        out_specs=[pl.BlockSpec((seq, 2 * HEAD_DIM), lambda hp, b: (b, hp)),
                   pl.BlockSpec((ctx_len, 2 * HEAD_DIM), lambda hp, b: (b, hp))],
        out_shape=[jax.ShapeDtypeStruct((bsz * seq, NA_WIDTH), BF16),
                   jax.ShapeDtypeStruct((bsz * ctx_len, NA_WIDTH), BF16)],
        compiler_params=_cparams(("arbitrary", "arbitrary")),
        name="na_attention",
    )(q, k, v, q, k, v, bias)


@functools.lru_cache(maxsize=None)
def _dft_tables(n):
    j = np.arange(n, dtype=np.int64)
    ang = 2.0 * np.pi * ((j[:, None] * j[None, :]) % n).astype(np.float64) / n
    return np.cos(ang).astype(BF16), np.sin(ang).astype(BF16)


def _fourier_kernel(cos_ref, sin_ref, f_ref, bdc_ref, bds_ref, bdw_ref, o_ref, *, scale):
    fb = f_ref[...]
    zr = _dot(cos_ref[...], fb)
    zs = _dot(sin_ref[...], fb)
    fr = (_dot(zr.astype(BF16), bdc_ref[...]) - _dot(zs.astype(BF16), bds_ref[...])) * scale
    o_ref[...] = _dot(fr.astype(BF16), bdw_ref[...]).astype(o_ref.dtype)


def _fourier(f_all, bdw, bsz, n, row_block0, tm):
    cos_n, sin_n = _dft_tables(n)
    cos64, sin64 = _dft_tables(FOURIER_DIM)
    eye = np.eye(FOURIER_GROUPS)
    bdc = jnp.asarray(np.kron(eye, cos64.astype(np.float32)), BF16)
    bds = jnp.asarray(np.kron(eye, sin64.astype(np.float32)), BF16)
    mt = n // tm
    small = pl.BlockSpec((FOURIER_WIDTH, FOURIER_WIDTH), lambda b, m: (0, 0))
    return pl.pallas_call(
        functools.partial(_fourier_kernel, scale=1.0 / math.sqrt(n * FOURIER_DIM)),
        grid=(bsz, mt),
        in_specs=[pl.BlockSpec((tm, n), lambda b, m: (m, 0)),
                  pl.BlockSpec((tm, n), lambda b, m: (m, 0)),
                  pl.BlockSpec((n, FOURIER_WIDTH), lambda b, m: (row_block0 + b, 0)),
                  small, small, small],
        out_specs=pl.BlockSpec((tm, FOURIER_WIDTH), lambda b, m: (b * mt + m, 0)),
        out_shape=jax.ShapeDtypeStruct((bsz * n, FOURIER_WIDTH), BF16),
        compiler_params=_cparams(("arbitrary", "arbitrary")),
        name="fourier_mix",
    )(jnp.asarray(cos_n), jnp.asarray(sin_n), f_all, bdc, bds, bdw)


def _route(h2, rw_hi_ref, rw_lo_ref, rb_ref, m_ref, g_ref):
    h_hi, h_lo = _split_bf16(h2)
    logits = (_dot(h_hi, rw_hi_ref[...]) + _dot(h_lo, rw_hi_ref[...]) + _dot(h_hi, rw_lo_ref[...])
              + rb_ref[...])
    lane = lax.broadcasted_iota(jnp.int32, logits.shape, 1).astype(F32)
    work = logits
    sel_any = jnp.zeros(logits.shape, jnp.bool_)
    top = None
    for _ in range(TOP_K):
        m = work.max(axis=-1, keepdims=True)
        if top is None:
            top = m
        idx = jnp.where(work == m, lane, float(N_EXPERTS)).min(axis=-1, keepdims=True)
        sel = lane == idx
        sel_any = sel_any | sel
        work = jnp.where(sel, -jnp.inf, work)
    e = jnp.where(sel_any, jnp.exp(logits - top), 0.0)
    g_ref[...] = e / e.sum(axis=-1, keepdims=True)
    m_ref[...] = sel_any.astype(F32).astype(m_ref.dtype)


def _post_l0_kernel(a_ref, ac_ref, f_ref, fc_ref, x_ref, c_ref, g1_ref, sh_ref, sc_ref, ng_ref, w_ref,
                    rwh_ref, rwl_ref, rb_ref, xo_ref, h_ref, m_ref, g_ref, *, n_lat_tiles):
    lat = pl.program_id(0) < n_lat_tiles
    a = jnp.where(lat, a_ref[...], ac_ref[...])
    f = jnp.where(lat, f_ref[...], fc_ref[...])
    x = jnp.where(lat, x_ref[...], c_ref[...])
    y = _dot(a, w_ref[:NA_WIDTH, :]) + _dot(f, w_ref[NA_WIDTH:, :])
    x = x + g1_ref[0] * y
    xo_ref[...] = x
    h2 = _modulate(x, ng_ref[...], sh_ref[0], sc_ref[0])
    h_ref[...] = h2
    _route(h2, rwh_ref, rwl_ref, rb_ref, m_ref, g_ref)


def _post_l1_kernel(yt_ref, x_ref, g1_ref, sh_ref, sc_ref, ng_ref, wgt_ref, bg_ref, wot_ref,
                    rwh_ref, rwl_ref, rb_ref, xo_ref, h_ref, m_ref, g_ref):
    gyt = yt_ref[...]
    zt = _dot(wgt_ref[...], gyt) + bg_ref[...]
    vt = (gyt.astype(F32) * jax.nn.sigmoid(zt)).astype(BF16)
    x = x_ref[...] + g1_ref[0] * _dot(wot_ref[...], vt).T
    xo_ref[...] = x
    h2 = _modulate(x, ng_ref[...], sh_ref[0], sc_ref[0])
    h_ref[...] = h2
    _route(h2, rwh_ref, rwl_ref, rb_ref, m_ref, g_ref)


def _router_args(router_w, router_b):
    rw_hi = router_w.astype(BF16)
    rw_lo = (router_w - rw_hi.astype(F32)).astype(BF16)
    return rw_hi, rw_lo, router_b.reshape(1, N_EXPERTS).astype(F32)


def _post_out(n_tok):
    specs = [pl.BlockSpec((TM, D_MODEL), lambda i: (i, 0)),
             pl.BlockSpec((TM, D_MODEL), lambda i: (i, 0)),
             pl.BlockSpec((TM, N_EXPERTS), lambda i: (i, 0)),
             pl.BlockSpec((TM, N_EXPERTS), lambda i: (i, 0))]
    shapes = [jax.ShapeDtypeStruct((n_tok, D_MODEL), F32),
              jax.ShapeDtypeStruct((n_tok, D_MODEL), F32),
              jax.ShapeDtypeStruct((n_tok, N_EXPERTS), BF16),
              jax.ShapeDtypeStruct((n_tok, N_EXPERTS), F32)]
    return specs, shapes


def _const_spec(shape):
    return pl.BlockSpec(shape, lambda i: tuple(0 for _ in shape))


def _post_l0(a_lat, a_ctx, f_lat, f_ctx, x, ctx, mod, norm_g, w_out_bf, router, n_lat, n_tok, tiles_per_batch):
    n_lat_tiles = n_lat // TM

    def lat(wd):
        return pl.BlockSpec((TM, wd), lambda i: (jnp.minimum(i, n_lat_tiles - 1), 0))

    def cx(wd):
        return pl.BlockSpec((TM, wd), lambda i: (jnp.maximum(i - n_lat_tiles, 0), 0))

    out_specs, out_shapes = _post_out(n_tok)
    return pl.pallas_call(
        functools.partial(_post_l0_kernel, n_lat_tiles=n_lat_tiles),
        grid=(n_tok // TM,),
        in_specs=[lat(NA_WIDTH), cx(NA_WIDTH), lat(FOURIER_WIDTH), cx(FOURIER_WIDTH),
                  lat(D_MODEL), cx(D_MODEL),
                  _mod_spec(2, tiles_per_batch), _mod_spec(3, tiles_per_batch), _mod_spec(4, tiles_per_batch),
                  _const_spec((1, D_MODEL)), _const_spec((D_MODEL, D_MODEL)),
                  _const_spec((D_MODEL, N_EXPERTS)), _const_spec((D_MODEL, N_EXPERTS)),
                  _const_spec((1, N_EXPERTS))],
        out_specs=out_specs, out_shape=out_shapes,
        compiler_params=_cparams(("arbitrary",)),
        name="post_mixer_l0",
    )(a_lat, a_ctx, f_lat, f_ctx, x, ctx, mod, mod, mod, norm_g.reshape(1, D_MODEL), w_out_bf, *router)


def _post_l1(gyt, x_all, mod, norm_g, w_glu_t_bf, b_glu, w_out_t_bf, router, n_lat, tiles_per_batch):
    out_specs, out_shapes = _post_out(n_lat)
    return pl.pallas_call(
        _post_l1_kernel,
        grid=(n_lat // TM,),
        in_specs=[pl.BlockSpec((D_MODEL, TM), lambda i: (0, i)),
                  pl.BlockSpec((TM, D_MODEL), lambda i: (i, 0)),
                  _mod_spec(2, tiles_per_batch), _mod_spec(3, tiles_per_batch), _mod_spec(4, tiles_per_batch),
                  _const_spec((1, D_MODEL)), _const_spec((D_MODEL, D_MODEL)), _const_spec((D_MODEL, 1)),
                  _const_spec((D_MODEL, D_MODEL)),
                  _const_spec((D_MODEL, N_EXPERTS)), _const_spec((D_MODEL, N_EXPERTS)),
                  _const_spec((1, N_EXPERTS))],
        out_specs=out_specs, out_shape=out_shapes,
        compiler_params=_cparams(("arbitrary",)),
        name="post_mixer_l1",
    )(gyt, x_all, mod, mod, mod, norm_g.reshape(1, D_MODEL), w_glu_t_bf, b_glu.reshape(D_MODEL, 1),
      w_out_t_bf, *router)


CHUNK = 8
TILE_ROWS = TM * TOP_K + N_EXPERTS * CHUNK
TILE_CHUNKS = TILE_ROWS // CHUNK
BLOCK_CHUNKS = EXPERT_ROWS // CHUNK
SORT_ROWS = 256
ISSUE_UNROLL = 8
HALF = D_MODEL // 2


def _pack_bf16_pairs(y):
    bits = pltpu.bitcast(y, jnp.uint32)
    return (bits[:, :HALF] & jnp.uint32(0xFFFF0000)) | (bits[:, HALF:] >> 16)


def _unpack_bf16_pairs(w):
    hi = pltpu.bitcast(w & jnp.uint32(0xFFFF0000), F32).astype(BF16)
    lo = pltpu.bitcast(w << 16, F32).astype(BF16)
    return hi, lo


def _sort_kernel(h_ref, m_ref, g_ref, hs_ref, pos_ref, gate_ref, cnt_ref):
    m = m_ref[...]
    mf = m.astype(F32)
    r = lax.broadcasted_iota(jnp.int32, (TM, TM), 0)
    c = lax.broadcasted_iota(jnp.int32, (TM, TM), 1)
    rank = _dot((r > c).astype(F32).astype(BF16), m)
    cnt = mf.sum(axis=0, keepdims=True)
    n8 = jnp.floor((cnt + (CHUNK - 1)) * (1.0 / CHUNK))
    er = lax.broadcasted_iota(jnp.int32, (N_EXPERTS, N_EXPERTS), 0)
    ec = lax.broadcasted_iota(jnp.int32, (N_EXPERTS, N_EXPERTS), 1)
    upper = (er < ec).astype(F32).astype(BF16)
    off = CHUNK * _dot(jnp.broadcast_to(n8, (8, N_EXPERTS)).astype(BF16), upper)[0:1]
    pos = rank + off
    slot = _dot(m, upper)
    chosen = mf > 0.5
    g = g_ref[...]
    lane = lax.broadcasted_iota(jnp.int32, pos_ref.shape, 1)
    p_out = jnp.zeros(pos_ref.shape, F32)
    g_out = jnp.zeros(gate_ref.shape, F32)
    for k in range(TOP_K):
        sel = chosen & (slot == float(k))
        pk = jnp.where(sel, pos, 0.0).sum(axis=-1, keepdims=True)
        gk = jnp.where(sel, g, 0.0).sum(axis=-1, keepdims=True)
        p_out = jnp.where(lane == k, pk, p_out)
        g_out = jnp.where(lane == k, gk, g_out)
    pos_ref[...] = p_out
    gate_ref[...] = g_out
    cnt_ref[...] = jnp.zeros_like(cnt_ref)
    cnt_ref[0:1, 0:N_EXPERTS] = cnt

    pos_t = p_out.T
    hb = h_ref[...].astype(BF16)
    rid = lax.broadcasted_iota(jnp.int32, (SORT_ROWS, TM), 0).astype(F32).astype(BF16)
    one, nil = jnp.ones((), BF16), jnp.zeros((), BF16)
    for rc in range(TILE_ROWS // SORT_ROWS):
        rel = (pos_t[0:16, :] - float(rc * SORT_ROWS)).astype(BF16)
        onehot = jnp.where(rid == rel[0:1, :], one, nil)
        for k in range(1, TOP_K):
            onehot = jnp.where(rid == rel[k:k + 1, :], one, onehot)
        hs_ref[rc * SORT_ROWS:(rc + 1) * SORT_ROWS, :] = _pack_bf16_pairs(_dot(onehot, hb))


def _sort(h2, m_sel, gates):
    n_tok = h2.shape[0]
    n_tiles = n_tok // TM
    tok = pl.BlockSpec((TM, N_EXPERTS), lambda i: (i, 0))
    wide = pl.BlockSpec((TM, 128), lambda i: (i, 0))
    return pl.pallas_call(
        _sort_kernel,
        grid=(n_tiles,),
        in_specs=[pl.BlockSpec((TM, D_MODEL), lambda i: (i, 0)), tok, tok],
        out_specs=[pl.BlockSpec((TILE_ROWS, HALF), lambda i: (i, 0)), wide, wide,
                   pl.BlockSpec((8, 128), lambda i: (i, 0))],
        out_shape=[jax.ShapeDtypeStruct((n_tiles * TILE_ROWS, HALF), jnp.uint32),
                   jax.ShapeDtypeStruct((n_tok, 128), F32),
                   jax.ShapeDtypeStruct((n_tok, 128), F32),
                   jax.ShapeDtypeStruct((n_tiles * 8, 128), F32)],
        compiler_params=_cparams(("arbitrary",)),
        name="moe_sort",
    )(h2, m_sel, gates)


def _chunk_tables(cnt, n_blocks):
    n_tiles = cnt.shape[0]
    n8 = (cnt + CHUNK - 1) // CHUNK
    lend = jnp.cumsum(n8, axis=1)
    lstart = lend - n8
    cum = jnp.cumsum(n8, axis=0)
    before = cum - n8
    total = cum[-1]
    nblk = (total + BLOCK_CHUNKS - 1) // BLOCK_CHUNKS
    bend = jnp.cumsum(nblk)
    gstart = (bend - nblk) * BLOCK_CHUNKS
    n_used = jnp.maximum(bend[-1], 1)
    blk = jnp.arange(n_blocks, dtype=jnp.int32)
    block_expert = (bend[None, :] <= jnp.minimum(blk, n_used - 1)[:, None]).sum(axis=1).astype(jnp.int32)
    block_expert = jnp.minimum(block_expert, N_EXPERTS - 1)
    experts = jnp.arange(N_EXPERTS, dtype=jnp.int32)
    lc = jnp.arange(TILE_CHUNKS, dtype=jnp.int32)
    e_of = (lend[:, None, :] <= lc[None, :, None]).sum(axis=-1).astype(jnp.int32)
    shift = gstart[None, :] + before - lstart
    pick = e_of[:, :, None] == experts[None, None, :]
    to_sorted = jnp.where(pick, shift[:, None, :], 0).sum(axis=-1) + jnp.where(e_of < N_EXPERTS, lc[None, :], 0)
    of_block = block_expert[:, None] == experts[None, :]
    gstart_b = jnp.where(of_block, gstart[None, :], 0).sum(axis=-1)
    total_b = jnp.where(of_block, total[None, :], 0).sum(axis=-1)

    def per_block(table):
        return jnp.where(of_block[:, :, None], table.T[None, :, :], 0).sum(axis=1)

    cum_b, lstart_b, before_b = per_block(cum), per_block(lstart), per_block(before)
    rel = (blk[:, None] * BLOCK_CHUNKS + jnp.arange(BLOCK_CHUNKS, dtype=jnp.int32)[None, :]) - gstart_b[:, None]
    valid = (blk[:, None] < n_used) & (rel < total_b[:, None])
    tile = jnp.minimum((cum_b[:, None, :] <= rel[:, :, None]).sum(axis=-1), n_tiles - 1).astype(jnp.int32)
    of_tile = tile[:, :, None] == jnp.arange(n_tiles, dtype=jnp.int32)[None, None, :]
    inside = jnp.where(of_tile, (lstart_b - before_b)[:, None, :], 0).sum(axis=-1)
    from_tiles = jnp.where(valid, tile * TILE_CHUNKS + inside + rel, TILE_CHUNKS - 1)
    return (block_expert, n_used.reshape(1).astype(jnp.int32), from_tiles.reshape(-1).astype(jnp.int32),
            to_sorted.reshape(-1).astype(jnp.int32))


def _chunk_rows(chunk):
    start = chunk * CHUNK
    return pl.ds(start if isinstance(start, int) else pl.multiple_of(start, CHUNK), CHUNK)


def _chunk_copy(src_ref, src_chunk, dst_ref, dst_chunk, sem):
    return pltpu.make_async_copy(src_ref.at[_chunk_rows(src_chunk), :], dst_ref.at[_chunk_rows(dst_chunk), :], sem)


def _expert_kernel(be_ref, nu_ref, src_ref, hs_ref, wgu_ref, bgu_ref, wd_ref, bd_ref, o_ref,
                   wgu_bf, wd_bf, x_buf, sems):
    i = pl.program_id(0)
    n_used = nu_ref[0]

    def gather(block, slot, start):
        if not start:
            pltpu.make_async_copy(hs_ref.at[pl.ds(0, EXPERT_ROWS), :], x_buf.at[slot], sems.at[slot]).wait()
            return
        for cidx in range(BLOCK_CHUNKS):
            _chunk_copy(hs_ref, src_ref[block * BLOCK_CHUNKS + cidx], x_buf.at[slot], cidx, sems.at[slot]).start()

    @pl.when(i == 0)
    def _():
        gather(0, 0, True)

    for slot in range(2):
        @pl.when((i + 1 < n_used) & ((i + 1) % 2 == slot))
        def _(slot=slot):
            gather(i + 1, slot, True)

    @pl.when((i == 0) | (be_ref[i] != be_ref[jnp.maximum(i - 1, 0)]))
    def _():
        wgu_bf[...] = wgu_ref[0].astype(BF16)
        wd_bf[...] = wd_ref[0].astype(BF16)

    for slot in range(2):
        @pl.when((i < n_used) & (i % 2 == slot))
        def _(slot=slot):
            gather(i, slot, False)
            x_hi, x_lo = _unpack_bf16_pairs(x_buf[slot])
            gu = _dot(x_hi, wgu_bf[:HALF, :]) + _dot(x_lo, wgu_bf[HALF:, :]) + bgu_ref[0]
            gate = jnp.minimum(gu[:, :D_MODEL], SWIGLU_LIMIT)
            up = jnp.clip(gu[:, D_MODEL:], -SWIGLU_LIMIT, SWIGLU_LIMIT)
            act = (up + 1.0) * (gate * jax.nn.sigmoid(SWIGLU_ALPHA * gate))
            y = _dot(act.astype(BF16), wd_bf[...]) + bd_ref[0]
            o_ref[...] = _pack_bf16_pairs(y.astype(BF16).astype(F32))

    @pl.when(i >= n_used)
    def _():
        o_ref[...] = jnp.zeros_like(o_ref)


def _experts(hs, block_expert, n_used, from_tiles, wgu, bgu, wd, bd):
    n_blocks = block_expert.shape[0]

    def row_map(i, be, nu, src):
        return (i, 0)

    def exp_map(i, be, nu, src):
        return (be[i], 0, 0)

    grid_spec = pltpu.PrefetchScalarGridSpec(
        num_scalar_prefetch=3,
        grid=(n_blocks,),
        in_specs=[pl.BlockSpec(memory_space=pl.ANY),
                  pl.BlockSpec((1, D_MODEL, 2 * D_MODEL), exp_map),
                  pl.BlockSpec((1, 1, 2 * D_MODEL), exp_map),
                  pl.BlockSpec((1, D_MODEL, D_MODEL), exp_map),
                  pl.BlockSpec((1, 1, D_MODEL), exp_map)],
        out_specs=pl.BlockSpec((EXPERT_ROWS, HALF), row_map),
        scratch_shapes=[pltpu.VMEM((D_MODEL, 2 * D_MODEL), BF16), pltpu.VMEM((D_MODEL, D_MODEL), BF16),
                        pltpu.VMEM((2, EXPERT_ROWS, HALF), jnp.uint32), pltpu.SemaphoreType.DMA((2,))],
    )
    return pl.pallas_call(
        _expert_kernel,
        grid_spec=grid_spec,
        out_shape=jax.ShapeDtypeStruct((n_blocks * EXPERT_ROWS, HALF), jnp.uint32),
        compiler_params=pltpu.CompilerParams(dimension_semantics=("arbitrary",),
                                             vmem_limit_bytes=EXPERT_VMEM_LIMIT),
        name="moe_experts",
    )(block_expert, n_used, from_tiles, hs, wgu, bgu.reshape(N_EXPERTS, 1, 2 * D_MODEL), wd,
      bd.reshape(N_EXPERTS, 1, D_MODEL))


def _combine_kernel(dst_ref, ys_ref, pos_ref, gate_ref, x_ref, g2_ref, ng_ref, o_ref, y_buf, sems, *, final_norm):
    i = pl.program_id(0)
    n_tiles = pl.num_programs(0)

    def gather(tile, slot, start):
        if not start:
            pltpu.make_async_copy(ys_ref.at[pl.ds(0, TILE_ROWS), :], y_buf.at[slot], sems.at[slot]).wait()
            return

        def body(g, carry):
            for j in range(ISSUE_UNROLL):
                cidx = g * ISSUE_UNROLL + j
                _chunk_copy(ys_ref, dst_ref[tile * TILE_CHUNKS + cidx], y_buf.at[slot], cidx, sems.at[slot]).start()
            return carry
        lax.fori_loop(0, TILE_CHUNKS // ISSUE_UNROLL, body, 0)

    @pl.when(i == 0)
    def _():
        gather(0, 0, True)

    for slot in range(2):
        @pl.when((i + 1 < n_tiles) & ((i + 1) % 2 == slot))
        def _(slot=slot):
            gather(i + 1, slot, True)

    pos = pos_ref[...]
    gate = gate_ref[...]
    for slot in range(2):
        @pl.when(i % 2 == slot)
        def _(slot=slot):
            gather(i, slot, False)
            acc_hi = jnp.zeros((TM, HALF), F32)
            acc_lo = jnp.zeros((TM, HALF), F32)
            rid = lax.broadcasted_iota(jnp.int32, (TM, SORT_ROWS), 1).astype(F32).astype(BF16)
            gate_b = gate.astype(BF16)
            for rc in range(TILE_ROWS // SORT_ROWS):
                rel = (pos - float(rc * SORT_ROWS)).astype(BF16)
                w = jnp.where(rid == rel[:, 0:1], gate_b[:, 0:1], jnp.zeros((), BF16))
                for k in range(1, TOP_K):
                    w = jnp.where(rid == rel[:, k:k + 1], gate_b[:, k:k + 1], w)
                y_hi, y_lo = _unpack_bf16_pairs(y_buf[slot, rc * SORT_ROWS:(rc + 1) * SORT_ROWS, :])
                acc_hi = acc_hi + _dot(w, y_hi)
                acc_lo = acc_lo + _dot(w, y_lo)
            x = x_ref[...] + g2_ref[0] * jnp.concatenate([acc_hi, acc_lo], axis=-1)
            if final_norm:
                ms = jnp.mean(x * x, axis=-1, keepdims=True)
                x = x * lax.rsqrt(ms + EPS) * ng_ref[...]
            o_ref[...] = x


def _combine(ys, to_sorted, pos4, gate4, x_res, mod, norm_g, tiles_per_batch, final_norm):
    n_tok = x_res.shape[0]
    tok = pl.BlockSpec((TM, D_MODEL), lambda i, dst: (i, 0))
    wide = pl.BlockSpec((TM, 128), lambda i, dst: (i, 0))
    tiles, bsz = tiles_per_batch
    grid_spec = pltpu.PrefetchScalarGridSpec(
        num_scalar_prefetch=1,
        grid=(n_tok // TM,),
        in_specs=[pl.BlockSpec(memory_space=pl.ANY), wide, wide, tok,
                  pl.BlockSpec((1, 1, D_MODEL), lambda i, dst: (jnp.minimum(i // tiles, bsz) * 6 + 5, 0, 0)),
                  pl.BlockSpec((1, D_MODEL), lambda i, dst: (0, 0))],
        out_specs=tok,
        scratch_shapes=[pltpu.VMEM((2, TILE_ROWS, HALF), jnp.uint32), pltpu.SemaphoreType.DMA((2,))],
    )
    return pl.pallas_call(
        functools.partial(_combine_kernel, final_norm=final_norm),
        grid_spec=grid_spec,
        out_shape=jax.ShapeDtypeStruct((n_tok, D_MODEL), F32),
        compiler_params=_cparams(("arbitrary",)),
        name="moe_combine",
    )(to_sorted, ys, pos4, gate4, x_res, mod, norm_g.reshape(1, D_MODEL))


def _moe(h2, m_sel, gates, x_res, mod, moe_w, tiles_per_batch, norm_g, final_norm):
    wgu, bgu, wd, bd = moe_w
    n_tiles = h2.shape[0] // TM
    n_blocks = -(-n_tiles * TILE_CHUNKS // BLOCK_CHUNKS) + N_EXPERTS
    hs, pos4, gate4, cnt = _sort(h2, m_sel, gates)
    cnt_i = cnt.reshape(n_tiles, 8, 128)[:, 0, :N_EXPERTS].astype(jnp.int32)
    block_expert, n_used, from_tiles, to_sorted = _chunk_tables(cnt_i, n_blocks)
    ys = _experts(hs, block_expert, n_used, from_tiles, wgu, bgu, wd, bd)
    return _combine(ys, to_sorted, pos4, gate4, x_res, mod, norm_g, tiles_per_batch, final_norm)


def _s5_tables(lam_re, lam_im, log_dt, b_re, b_im, c_re, c_im, d_skip):
    q = S5_Q
    lam = lax.complex(jnp.minimum(lam_re.astype(F32), LAMBDA_RE_MAX), lam_im.astype(F32))
    dt = jnp.exp(log_dt.astype(F32))[..., None]
    a = jnp.exp(lam * dt)
    bbar = ((a - 1) / lam)[..., None] * lax.complex(b_re.astype(F32), b_im.astype(F32))
    cc = lax.complex(c_re.astype(F32), c_im.astype(F32))
    steps = jnp.arange(q + 1, dtype=F32)
    pw = jnp.exp((lam * dt)[:, :, None, :] * steps[None, None, :, None])
    kern = jnp.einsum('dgop,dgkp,dgpc->dgcok', cc, pw[:, :, :q], bbar).real
    eye = jnp.eye(SSM_GROUP, dtype=F32)
    k_fwd = kern[0].at[..., 0].add(kern[1][..., 0] + d_skip.astype(F32)[:, :, None] * eye[None])
    k_bwd = jnp.concatenate([jnp.zeros_like(kern[1][..., :1]), kern[1][..., :0:-1]], axis=-1)
    k_fwd = k_fwd.reshape(SSM_GROUPS, SSM_GROUP * SSM_GROUP, q)
    k_bwd = k_bwd.reshape(SSM_GROUPS, SSM_GROUP * SSM_GROUP, q)
    def both(f, b):
        return jnp.concatenate([f, b], axis=-1)

    e_in = both(pw[0][:, ::-1][:, 1:], pw[1][:, :q])
    b_in = both(jnp.transpose(bbar[0], (0, 2, 1)), jnp.transpose(bbar[1], (0, 2, 1)))
    e_out = jnp.transpose(both(pw[0][:, 1:], pw[1][:, ::-1][:, :q]), (0, 2, 1))
    c_out = jnp.transpose(both(cc[0], cc[1]), (0, 2, 1))
    a_q = both(pw[0][:, q], pw[1][:, q]).reshape(SSM_GROUPS, 1, 2 * SSM_STATE)
    parts = [t for z in (e_in, b_in, e_out, c_out, a_q) for t in (z.real, z.imag)]
    return [k_fwd, k_bwd] + parts


def _s5_kernel(u_ref, kf_ref, kb_ref, er_ref, ei_ref, br_ref, bi_ref, pr_ref, pi_ref, cr_ref, ci_ref,
               ar_ref, ai_ref, o_ref, w_ref, s_ref, h_ref, *, bsz, n_lat_chunks, n_ctx_chunks):
    q = S5_Q
    half = 2 * SSM_STATE
    n_lat = bsz * n_lat_chunks

    causal = (lax.broadcasted_iota(jnp.int32, (q, q), 1) >= lax.broadcasted_iota(jnp.int32, (q, q), 0))

    def build(c, carry):
        for o in range(SSM_GROUP):
            row = pl.ds(c * SSM_GROUP + o, 1)
            kf = pltpu.roll(jnp.broadcast_to(kf_ref[0, row, :], (q, q)), 0, 1, stride=1, stride_axis=0)
            kb = pltpu.roll(jnp.broadcast_to(kb_ref[0, row, :], (q, q)), 0, 1, stride=1, stride_axis=0)
            w_ref[pl.ds(pl.multiple_of(c * q, q), q), o * q:(o + 1) * q] = jnp.where(causal, kf, kb).astype(BF16)
        return carry

    lax.fori_loop(0, SSM_GROUP, build, 0)

    u = u_ref[...]
    er, ei = er_ref[0], ei_ref[0]
    s_re = None
    for c in range(SSM_GROUP):
        br, bi = br_ref[0, c:c + 1, :], bi_ref[0, c:c + 1, :]
        w_re = (er * br - ei * bi).astype(BF16)
        w_im = (er * bi + ei * br).astype(BF16)
        s_re = _dot(u[c], w_re) if s_re is None else s_re + _dot(u[c], w_re)
        s_im = _dot(u[c], w_im) if c == 0 else s_im + _dot(u[c], w_im)
    s_ref[0] = s_re
    s_ref[1] = s_im

    fwd = lax.broadcasted_iota(jnp.int32, (bsz, half), 1) < SSM_STATE
    ar = jnp.broadcast_to(ar_ref[0], (bsz, half))
    ai = jnp.broadcast_to(ai_ref[0], (bsz, half))

    def chunk_rows(base, j, per_sample):
        return pl.ds(base + j, bsz, stride=per_sample)

    def load(base, j, n):
        rows_f = chunk_rows(base, j, n)
        rows_b = chunk_rows(base, n - 1 - j, n)
        return (jnp.where(fwd, s_ref[0, rows_f, :], s_ref[0, rows_b, :]),
                jnp.where(fwd, s_ref[1, rows_f, :], s_ref[1, rows_b, :]))

    def step(hr, hi, sr, si):
        return ar * hr - ai * hi + sr, ar * hi + ai * hr + si

    def ctx_body(j, carry):
        return step(*carry, *load(n_lat, j, n_ctx_chunks))

    def lat_body(j, carry):
        hr, hi = carry
        rows_f = chunk_rows(0, j, n_lat_chunks)
        rows_b = chunk_rows(0, n_lat_chunks - 1 - j, n_lat_chunks)
        h_ref[0, rows_f, :] = jnp.where(fwd, hr, 0.0)
        h_ref[1, rows_f, :] = jnp.where(fwd, hi, 0.0)
        h_ref[2, rows_b, :] = jnp.where(fwd, 0.0, hr)
        h_ref[3, rows_b, :] = jnp.where(fwd, 0.0, hi)
        return step(hr, hi, *load(0, j, n_lat_chunks))

    zero = jnp.zeros((bsz, half), F32)
    carry = lax.fori_loop(0, n_ctx_chunks, ctx_body, (zero, zero))
    lax.fori_loop(0, n_lat_chunks, lat_body, carry)

    ucat = jnp.concatenate([u[c][:n_lat] for c in range(SSM_GROUP)], axis=-1)
    y = _dot(ucat, w_ref[...])
    h_re = (h_ref[0] + h_ref[2]).astype(BF16)
    h_im = (h_ref[1] + h_ref[3]).astype(BF16)
    pr, pi = pr_ref[0], pi_ref[0]
    cr_all, ci_all = cr_ref[0], ci_ref[0]
    for o in range(SSM_GROUP):
        cr, ci = cr_all[:, o:o + 1], ci_all[:, o:o + 1]
        w_re = (pr * cr - pi * ci).astype(BF16)
        w_im = (pr * ci + pi * cr).astype(BF16)
        y_o = y[:, o * q:(o + 1) * q] + _dot(h_re, w_re) - _dot(h_im, w_im)
        o_ref[o] = jax.nn.gelu(y_o).astype(o_ref.dtype)


def _s5(u3, tables, bsz, n_lat_chunks, n_ctx_chunks):
    q = S5_Q
    n_chunks = u3.shape[1]
    n_lat = bsz * n_lat_chunks
    assert n_chunks == n_lat + bsz * n_ctx_chunks

    def gspec(*shape):
        return pl.BlockSpec((1,) + shape, lambda g: (g,) + tuple(0 for _ in shape))

    return pl.pallas_call(
        functools.partial(_s5_kernel, bsz=bsz, n_lat_chunks=n_lat_chunks, n_ctx_chunks=n_ctx_chunks),
        grid=(SSM_GROUPS,),
        in_specs=[pl.BlockSpec((SSM_GROUP, n_chunks, q), lambda g: (g, 0, 0)),
                  gspec(SSM_GROUP * SSM_GROUP, q), gspec(SSM_GROUP * SSM_GROUP, q),
                  gspec(q, 2 * SSM_STATE), gspec(q, 2 * SSM_STATE),
                  gspec(SSM_GROUP, 2 * SSM_STATE), gspec(SSM_GROUP, 2 * SSM_STATE),
                  gspec(2 * SSM_STATE, q), gspec(2 * SSM_STATE, q),
                  gspec(2 * SSM_STATE, SSM_GROUP), gspec(2 * SSM_STATE, SSM_GROUP),
                  gspec(1, 2 * SSM_STATE), gspec(1, 2 * SSM_STATE)],
        out_specs=pl.BlockSpec((SSM_GROUP, n_lat, q), lambda g: (g, 0, 0)),
        out_shape=jax.ShapeDtypeStruct((SSM_GROUPS * SSM_GROUP, n_lat, q), BF16),
        scratch_shapes=[pltpu.VMEM((SSM_GROUP * q, SSM_GROUP * q), BF16),
                        pltpu.VMEM((2, n_chunks, 2 * SSM_STATE), F32),
                        pltpu.VMEM((4, n_lat, 2 * SSM_STATE), F32)],
        compiler_params=_cparams(("arbitrary",)),
        name="s5_scan",
    )(u3, *tables)


def _moe_weights(w_gate_up, b_gate_up, w_down, b_down):
    return w_gate_up, b_gate_up, w_down, b_down


def kernel(x, c, ctx, c_ctx, l0_ada_w, l0_ada_b, l0_norm_mix, l0_w_in, l0_rpb, l0_w_fourier, l0_w_out, l0_norm_ffn, l0_router_w, l0_router_b, l0_w_gate_up, l0_b_gate_up, l0_w_down, l0_b_down, l1_ada_w, l1_ada_b, l1_norm_mix, l1_w_in, l1_lambda_re, l1_lambda_im, l1_log_dt, l1_b_re, l1_b_im, l1_c_re, l1_c_im, l1_d_skip, l1_w_glu, l1_b_glu, l1_w_out, l1_norm_ffn, l1_router_w, l1_router_b, l1_w_gate_up, l1_b_gate_up, l1_w_down, l1_b_down, final_norm):
    bsz, seq, _ = x.shape
    ctx_len = ctx.shape[1]
    n_lat = bsz * seq
    n_ctx = bsz * ctx_len
    n_tok = n_lat + n_ctx
    assert seq % TM == 0 and n_ctx % TM == 0
    x2 = x.reshape(n_lat, D_MODEL)
    ctx2 = ctx.reshape(n_ctx, D_MODEL)

    n_mod = bsz + 1
    c_pad = jnp.zeros((-(-n_mod // 8) * 8, D_MODEL), F32).at[:bsz].set(c).at[bsz].set(c_ctx)
    mod0 = _ada(c_pad, l0_ada_w, l0_ada_b)[:n_mod].reshape(n_mod * 6, 1, D_MODEL)
    mod1 = _ada(c_pad, l1_ada_w, l1_ada_b)[:n_mod].reshape(n_mod * 6, 1, D_MODEL)
    tpb = (seq // TM, bsz)

    q, k, v, f = _modproj([x2, ctx2], mod0, l0_norm_mix, l0_w_in.astype(BF16),
                          (NA_WIDTH, NA_WIDTH, NA_WIDTH, FOURIER_WIDTH), n_lat, n_tok, tpb,
                          q_scale=HEAD_DIM ** -0.5)
    a_lat, a_ctx = _attention(q, k, v, l0_rpb, bsz, seq, ctx_len)
    bdw = (jnp.eye(FOURIER_GROUPS, dtype=F32)[:, None, :, None] * l0_w_fourier[:, :, None, :]).reshape(
        FOURIER_WIDTH, FOURIER_WIDTH).astype(BF16)
    f_lat = _fourier(f, bdw, bsz, seq, 0, TM)
    f_ctx = _fourier(f, bdw, bsz, ctx_len, n_lat // ctx_len, ctx_len)
    x_all, h2, m_sel, gates = _post_l0(a_lat, a_ctx, f_lat, f_ctx, x2, ctx2, mod0, l0_norm_ffn,
                                       l0_w_out.astype(BF16), _router_args(l0_router_w, l0_router_b),
                                       n_lat, n_tok, tpb)
    x_all = _moe(h2, m_sel, gates, x_all, mod0, _moe_weights(l0_w_gate_up, l0_b_gate_up, l0_w_down, l0_b_down),
                 tpb, final_norm, False)

    assert seq % S5_Q == 0 and ctx_len % S5_Q == 0
    ut = _modproj_t(x_all, mod1, l1_norm_mix, l1_w_in.T.astype(BF16), n_tok, tpb)
    tables = _s5_tables(l1_lambda_re, l1_lambda_im, l1_log_dt, l1_b_re, l1_b_im, l1_c_re, l1_c_im, l1_d_skip)
    yt = _s5(ut.reshape(D_MODEL, n_tok // S5_Q, S5_Q), tables, bsz, seq // S5_Q, ctx_len // S5_Q)
    gy = yt.reshape(D_MODEL, n_lat)
    x1, h2, m_sel, gates = _post_l1(gy, x_all, mod1, l1_norm_ffn, l1_w_glu.T.astype(BF16), l1_b_glu,
                                    l1_w_out.T.astype(BF16), _router_args(l1_router_w, l1_router_b), n_lat, tpb)
    out = _moe(h2, m_sel, gates, x1, mod1, _moe_weights(l1_w_gate_up, l1_b_gate_up, l1_w_down, l1_b_down),
               tpb, final_norm, True)
    return out.reshape(bsz, seq, D_MODEL)
```

```python
import functools
import math

import numpy as np
import jax
import jax.numpy as jnp
from jax import lax
from jax.experimental import pallas as pl
from jax.experimental.pallas import tpu as pltpu

F32 = jnp.float32
BF16 = jnp.bfloat16

D_MODEL = 1024
GRID_W = 64
HEAD_DIM = 64
NA_WIDTH = 512
NA_HEADS = 8
MAX_KR = 8
KC = 16
FOURIER_DIM = 64
FOURIER_GROUPS = 8
FOURIER_WIDTH = 512
SSM_GROUP = 16
SSM_GROUPS = 64
SSM_STATE = 64
LAMBDA_RE_MAX = -1e-4
N_EXPERTS = 32
TOP_K = 4
SWIGLU_ALPHA = 1.702
SWIGLU_LIMIT = 7.0
EPS = 1e-6

TM = 512
EXPERT_ROWS = 512
S5_Q = 128
VMEM_LIMIT = 48 * 1024 * 1024
EXPERT_VMEM_LIMIT = 58 * 1024 * 1024


def _cparams(sem):
    return pltpu.CompilerParams(dimension_semantics=sem, vmem_limit_bytes=VMEM_LIMIT)


def _split_bf16(a):
    hi = a.astype(BF16)
    lo = (a - hi.astype(F32)).astype(BF16)
    return hi, lo


def _dot(a, b):
    return jnp.dot(a, b, preferred_element_type=F32)


def _dot_nt(a, b):
    return lax.dot_general(a, b, (((1,), (1,)), ((), ())), preferred_element_type=F32)


def _ada_kernel(c_ref, w_ref, b_ref, o_ref):
    c = c_ref[...]
    s = c * jax.nn.sigmoid(c)
    s_hi, s_lo = _split_bf16(s)
    w_hi, w_lo = _split_bf16(w_ref[...])
    o_ref[...] = _dot(s_hi, w_hi) + _dot(s_lo, w_hi) + _dot(s_hi, w_lo) + b_ref[...]


def _ada(c_pad, ada_w, ada_b):
    n = ada_w.shape[1]
    tn = 1024
    return pl.pallas_call(
        _ada_kernel,
        grid=(n // tn,),
        in_specs=[pl.BlockSpec((c_pad.shape[0], D_MODEL), lambda j: (0, 0)),
                  pl.BlockSpec((D_MODEL, tn), lambda j: (0, j)),
                  pl.BlockSpec((1, tn), lambda j: (0, j))],
        out_specs=pl.BlockSpec((c_pad.shape[0], tn), lambda j: (0, j)),
        out_shape=jax.ShapeDtypeStruct((c_pad.shape[0], n), F32),
        compiler_params=_cparams(("arbitrary",)),
        name="ada_mod",
    )(c_pad, ada_w, ada_b.reshape(1, n))


def _mod_spec(which, tiles_per_batch):
    tiles, bsz = tiles_per_batch
    return pl.BlockSpec((1, 1, D_MODEL), lambda i: (jnp.minimum(i // tiles, bsz) * 6 + which, 0, 0))


def _modulate(x, g, sh, sc):
    ms = jnp.mean(x * x, axis=-1, keepdims=True)
    y = x * lax.rsqrt(ms + EPS) * g
    return y * (1.0 + sc) + sh


def _modproj_kernel(*refs, n_lat_tiles, two_src, widths, q_scale):
    if two_src:
        x_ref, c_ref, sh_ref, sc_ref, g_ref, w_ref = refs[:6]
        o_refs = refs[6:]
        i = pl.program_id(0)
        x = jnp.where(i < n_lat_tiles, x_ref[...], c_ref[...])
    else:
        x_ref, sh_ref, sc_ref, g_ref, w_ref = refs[:5]
        o_refs = refs[5:]
        x = x_ref[...]
    h = _modulate(x, g_ref[...], sh_ref[0], sc_ref[0]).astype(BF16)
    off = 0
    for j, (o_ref, wd) in enumerate(zip(o_refs, widths)):
        y = _dot(h, w_ref[:, off:off + wd])
        if j == 0 and q_scale != 1.0:
            y = y * q_scale
        o_ref[...] = y.astype(o_ref.dtype)
        off += wd


def _modproj(x_srcs, mod, norm_g, w_bf, widths, n_lat, n_tok, tiles_per_batch, q_scale=1.0):
    n_tiles = n_tok // TM
    n_lat_tiles = n_lat // TM
    two_src = len(x_srcs) == 2
    n_out = w_bf.shape[1]
    if two_src:
        x_specs = [pl.BlockSpec((TM, D_MODEL), lambda i: (jnp.minimum(i, n_lat_tiles - 1), 0)),
                   pl.BlockSpec((TM, D_MODEL), lambda i: (jnp.maximum(i - n_lat_tiles, 0), 0))]
    else:
        x_specs = [pl.BlockSpec((TM, D_MODEL), lambda i: (i, 0))]
    return pl.pallas_call(
        functools.partial(_modproj_kernel, n_lat_tiles=n_lat_tiles, two_src=two_src,
                          widths=tuple(widths), q_scale=q_scale),
        grid=(n_tiles,),
        in_specs=x_specs + [_mod_spec(0, tiles_per_batch), _mod_spec(1, tiles_per_batch),
                            pl.BlockSpec((1, D_MODEL), lambda i: (0, 0)),
                            pl.BlockSpec((D_MODEL, n_out), lambda i: (0, 0))],
        out_specs=[pl.BlockSpec((TM, wd), lambda i: (i, 0)) for wd in widths],
        out_shape=[jax.ShapeDtypeStruct((n_tok, wd), BF16) for wd in widths],
        compiler_params=_cparams(("arbitrary",)),
        name="modulate_in_proj",
    )(*x_srcs, mod, mod, norm_g.reshape(1, D_MODEL), w_bf)


def _modproj_t_kernel(x_ref, sh_ref, sc_ref, g_ref, wt_ref, o_ref):
    h = _modulate(x_ref[...], g_ref[...], sh_ref[0], sc_ref[0]).astype(BF16)
    o_ref[...] = _dot_nt(wt_ref[...], h).astype(o_ref.dtype)


def _modproj_t(x_all, mod, norm_g, wt_bf, n_tok, tiles_per_batch):
    n_out = wt_bf.shape[0]
    return pl.pallas_call(
        _modproj_t_kernel,
        grid=(n_tok // TM,),
        in_specs=[pl.BlockSpec((TM, D_MODEL), lambda i: (i, 0)),
                  _mod_spec(0, tiles_per_batch), _mod_spec(1, tiles_per_batch),
                  pl.BlockSpec((1, D_MODEL), lambda i: (0, 0)),
                  pl.BlockSpec((n_out, D_MODEL), lambda i: (0, 0))],
        out_specs=pl.BlockSpec((n_out, TM), lambda i: (0, i)),
        out_shape=jax.ShapeDtypeStruct((n_out, n_tok), BF16),
        compiler_params=_cparams(("arbitrary",)),
        name="modulate_in_proj_t",
    )(x_all, mod, mod, norm_g.reshape(1, D_MODEL), wt_bf)


Q_ROWS = 4
BAND_ROWS = Q_ROWS + MAX_KR


def _na_bias_table(rpb, rows):
    qc = np.arange(GRID_W)[:, None]
    kc = np.arange(GRID_W)[None, :]
    win0 = np.clip(qc - KC // 2, 0, GRID_W - KC)
    col_valid = (kc >= win0) & (kc < win0 + KC)
    col_off = np.clip(kc - qc, -(KC - 1), KC - 1) + KC - 1
    cb = rpb.astype(F32)[:, :, col_off]
    cb = jnp.where(jnp.asarray(col_valid)[None, None], cb, -jnp.inf)
    n_blk = rows // Q_ROWS
    idx = np.zeros((3, Q_ROWS, BAND_ROWS), np.int64)
    ok = np.zeros((3, Q_ROWS, BAND_ROWS), bool)
    for cls, blk in enumerate((0, 1, n_blk - 1)):
        band0 = int(np.clip(blk * Q_ROWS - MAX_KR // 2, 0, rows - BAND_ROWS))
        for rho in range(Q_ROWS):
            r = blk * Q_ROWS + rho
            r0 = int(np.clip(r - MAX_KR // 2, 0, rows - MAX_KR))
            for kap in range(BAND_ROWS):
                kr = band0 + kap
                ok[cls, rho, kap] = r0 <= kr < r0 + MAX_KR
                idx[cls, rho, kap] = np.clip(kr - r + MAX_KR - 1, 0, 2 * MAX_KR - 2)
    t = cb[:, idx]
    t = jnp.where(jnp.asarray(ok)[None, :, :, :, None, None], t, -jnp.inf)
    t = jnp.transpose(t, (1, 0, 2, 4, 3, 5))
    return t.reshape(3, NA_HEADS, Q_ROWS * GRID_W, BAND_ROWS * GRID_W)


def _attn_kernel(q_ref, k_ref, v_ref, qc_ref, kc_ref, vc_ref, bias_ref, o_ref, oc_ref, *, rows):
    lane = lax.broadcasted_iota(jnp.int32, (1, 2 * HEAD_DIM), 1)
    first = lane < HEAD_DIM
    kc = kc_ref[...]
    vc = vc_ref[...]
    zero = jnp.zeros((), BF16)

    def softmax_pv(s_list, v_list):
        m = s_list[0].max(axis=-1, keepdims=True)
        for s in s_list[1:]:
            m = jnp.maximum(m, s.max(axis=-1, keepdims=True))
        den = None
        acc = None
        for s, vv in zip(s_list, v_list):
            p = jnp.exp(s - m)
            ps = p.sum(axis=-1, keepdims=True)
            den = ps if den is None else den + ps
            pv = _dot(p.astype(BF16), vv)
            acc = pv if acc is None else acc + pv
        return acc / den

    n_blk = rows // Q_ROWS
    q_len = Q_ROWS * GRID_W

    def block_body(i, carry):
        band0 = jnp.clip(i * Q_ROWS - MAX_KR // 2, 0, rows - BAND_ROWS)
        cls = jnp.where(i == 0, 0, jnp.where(i == n_blk - 1, 2, 1))
        rows_q = pl.ds(pl.multiple_of(i * q_len, q_len), q_len)
        q_b = q_ref[rows_q, :]
        band = pl.ds(pl.multiple_of(band0 * GRID_W, GRID_W), BAND_ROWS * GRID_W)
        kb = k_ref[band, :]
        vb = v_ref[band, :]
        outs = []
        for hh in range(2):
            q_h = jnp.where(first if hh == 0 else ~first, q_b, zero)
            s_w = _dot_nt(q_h, kb) + bias_ref[cls, hh]
            s_c = _dot_nt(q_h, kc)
            outs.append(softmax_pv([s_w, s_c], [vb, vc]))
        o_ref[rows_q, :] = jnp.where(first, outs[0], outs[1]).astype(o_ref.dtype)
        return carry

    lax.fori_loop(0, n_blk, block_body, 0)

    q_c = qc_ref[...]
    outs = []
    for hh in range(2):
        q_h = jnp.where(first if hh == 0 else ~first, q_c, zero)
        outs.append(softmax_pv([_dot_nt(q_h, kc)], [vc]))
    oc_ref[...] = jnp.where(first, outs[0], outs[1]).astype(oc_ref.dtype)


def _attention(q, k, v, rpb, bsz, seq, ctx_len):
    rows = seq // GRID_W
    assert rows % Q_ROWS == 0 and rows >= 3 * Q_ROWS and seq % ctx_len == 0
    bias = _na_bias_table(rpb, rows)
    cb0 = bsz * seq // ctx_len
    lat = pl.BlockSpec((seq, 2 * HEAD_DIM), lambda hp, b: (b, hp))
    ctx = pl.BlockSpec((ctx_len, 2 * HEAD_DIM), lambda hp, b: (cb0 + b, hp))
    return pl.pallas_call(
        functools.partial(_attn_kernel, rows=rows),
        grid=(NA_HEADS // 2, bsz),
        in_specs=[lat, lat, lat, ctx, ctx, ctx,
                  pl.BlockSpec((3, 2, Q_ROWS * GRID_W, BAND_ROWS * GRID_W), lambda hp, b: (0, hp, 0, 0))],
        out_specs=[pl.BlockSpec((seq, 2 * HEAD_DIM), lambda hp, b: (b, hp)),
                   pl.BlockSpec((ctx_len, 2 * HEAD_DIM), lambda hp, b: (b, hp))],
        out_shape=[jax.ShapeDtypeStruct((bsz * seq, NA_WIDTH), BF16),
                   jax.ShapeDtypeStruct((bsz * ctx_len, NA_WIDTH), BF16)],
        compiler_params=_cparams(("arbitrary", "arbitrary")),
        name="na_attention",
    )(q, k, v, q, k, v, bias)


@functools.lru_cache(maxsize=None)
def _dft_tables(n):
    j = np.arange(n, dtype=np.int64)
    ang = 2.0 * np.pi * ((j[:, None] * j[None, :]) % n).astype(np.float64) / n
    return np.cos(ang).astype(BF16), np.sin(ang).astype(BF16)


def _fourier_kernel(cos_ref, sin_ref, f_ref, bdc_ref, bds_ref, bdw_ref, o_ref, *, scale):
    fb = f_ref[...]
    zr = _dot(cos_ref[...], fb)
    zs = _dot(sin_ref[...], fb)
    fr = (_dot(zr.astype(BF16), bdc_ref[...]) - _dot(zs.astype(BF16), bds_ref[...])) * scale
    o_ref[...] = _dot(fr.astype(BF16), bdw_ref[...]).astype(o_ref.dtype)


def _fourier(f_all, bdw, bsz, n, row_block0, tm):
    cos_n, sin_n = _dft_tables(n)
    cos64, sin64 = _dft_tables(FOURIER_DIM)
    eye = np.eye(FOURIER_GROUPS)
    bdc = jnp.asarray(np.kron(eye, cos64.astype(np.float32)), BF16)
    bds = jnp.asarray(np.kron(eye, sin64.astype(np.float32)), BF16)
    mt = n // tm
    small = pl.BlockSpec((FOURIER_WIDTH, FOURIER_WIDTH), lambda b, m: (0, 0))
    return pl.pallas_call(
        functools.partial(_fourier_kernel, scale=1.0 / math.sqrt(n * FOURIER_DIM)),
        grid=(bsz, mt),
        in_specs=[pl.BlockSpec((tm, n), lambda b, m: (m, 0)),
                  pl.BlockSpec((tm, n), lambda b, m: (m, 0)),
                  pl.BlockSpec((n, FOURIER_WIDTH), lambda b, m: (row_block0 + b, 0)),
                  small, small, small],
        out_specs=pl.BlockSpec((tm, FOURIER_WIDTH), lambda b, m: (b * mt + m, 0)),
        out_shape=jax.ShapeDtypeStruct((bsz * n, FOURIER_WIDTH), BF16),
        compiler_params=_cparams(("arbitrary", "arbitrary")),
        name="fourier_mix",
    )(jnp.asarray(cos_n), jnp.asarray(sin_n), f_all, bdc, bds, bdw)


def _route(h2, rw_hi_ref, rw_lo_ref, rb_ref):
    h_hi, h_lo = _split_bf16(h2)
    logits = (_dot(h_hi, rw_hi_ref[...]) + _dot(h_lo, rw_hi_ref[...]) + _dot(h_hi, rw_lo_ref[...])
              + rb_ref[...])
    lane = lax.broadcasted_iota(jnp.int32, logits.shape, 1).astype(F32)
    work = logits
    sel_any = jnp.zeros(logits.shape, jnp.bool_)
    top = None
    for _ in range(TOP_K):
        m = work.max(axis=-1, keepdims=True)
        if top is None:
            top = m
        idx = jnp.where(work == m, lane, float(N_EXPERTS)).min(axis=-1, keepdims=True)
        sel = lane == idx
        sel_any = sel_any | sel
        work = jnp.where(sel, -jnp.inf, work)
    e = jnp.where(sel_any, jnp.exp(logits - top), 0.0)
    return sel_any.astype(F32), e / e.sum(axis=-1, keepdims=True)


def _post_l0_kernel(a_ref, ac_ref, f_ref, fc_ref, x_ref, c_ref, g1_ref, sh_ref, sc_ref, ng_ref, w_ref,
                    rwh_ref, rwl_ref, rb_ref, xo_ref, hs_ref, pos_ref, gate_ref, cnt_ref, *, n_lat_tiles):
    lat = pl.program_id(0) < n_lat_tiles
    a = jnp.where(lat, a_ref[...], ac_ref[...])
    f = jnp.where(lat, f_ref[...], fc_ref[...])
    x = jnp.where(lat, x_ref[...], c_ref[...])
    y = _dot(a, w_ref[:NA_WIDTH, :]) + _dot(f, w_ref[NA_WIDTH:, :])
    x = x + g1_ref[0] * y
    xo_ref[...] = x
    h2 = _modulate(x, ng_ref[...], sh_ref[0], sc_ref[0])
    _sort_tile(h2, *_route(h2, rwh_ref, rwl_ref, rb_ref), hs_ref, pos_ref, gate_ref, cnt_ref)


def _post_l1_kernel(yt_ref, x_ref, g1_ref, sh_ref, sc_ref, ng_ref, wgt_ref, bg_ref, wot_ref,
                    rwh_ref, rwl_ref, rb_ref, xo_ref, hs_ref, pos_ref, gate_ref, cnt_ref):
    gyt = yt_ref[...]
    zt = _dot(wgt_ref[...], gyt) + bg_ref[...]
    vt = (gyt.astype(F32) * jax.nn.sigmoid(zt)).astype(BF16)
    x = x_ref[...] + g1_ref[0] * _dot(wot_ref[...], vt).T
    xo_ref[...] = x
    h2 = _modulate(x, ng_ref[...], sh_ref[0], sc_ref[0])
    _sort_tile(h2, *_route(h2, rwh_ref, rwl_ref, rb_ref), hs_ref, pos_ref, gate_ref, cnt_ref)


def _router_args(router_w, router_b):
    rw_hi = router_w.astype(BF16)
    rw_lo = (router_w - rw_hi.astype(F32)).astype(BF16)
    return rw_hi, rw_lo, router_b.reshape(1, N_EXPERTS).astype(F32)


def _post_out(n_tok):
    n_tiles = n_tok // TM
    wide = pl.BlockSpec((TM, 128), lambda i: (i, 0))
    specs = [pl.BlockSpec((TM, D_MODEL), lambda i: (i, 0)),
             pl.BlockSpec((TILE_ROWS, HALF), lambda i: (i, 0)), wide, wide,
             pl.BlockSpec((8, 128), lambda i: (i, 0))]
    shapes = [jax.ShapeDtypeStruct((n_tok, D_MODEL), F32),
              jax.ShapeDtypeStruct((n_tiles * TILE_ROWS, HALF), jnp.uint32),
              jax.ShapeDtypeStruct((n_tok, 128), F32),
              jax.ShapeDtypeStruct((n_tok, 128), F32),
              jax.ShapeDtypeStruct((n_tiles * 8, 128), F32)]
    return specs, shapes


def _const_spec(shape):
    return pl.BlockSpec(shape, lambda i: tuple(0 for _ in shape))


def _post_l0(a_lat, a_ctx, f_lat, f_ctx, x, ctx, mod, norm_g, w_out_bf, router, n_lat, n_tok, tiles_per_batch):
    n_lat_tiles = n_lat // TM

    def lat(wd):
        return pl.BlockSpec((TM, wd), lambda i: (jnp.minimum(i, n_lat_tiles - 1), 0))

    def cx(wd):
        return pl.BlockSpec((TM, wd), lambda i: (jnp.maximum(i - n_lat_tiles, 0), 0))

    out_specs, out_shapes = _post_out(n_tok)
    return pl.pallas_call(
        functools.partial(_post_l0_kernel, n_lat_tiles=n_lat_tiles),
        grid=(n_tok // TM,),
        in_specs=[lat(NA_WIDTH), cx(NA_WIDTH), lat(FOURIER_WIDTH), cx(FOURIER_WIDTH),
                  lat(D_MODEL), cx(D_MODEL),
                  _mod_spec(2, tiles_per_batch), _mod_spec(3, tiles_per_batch), _mod_spec(4, tiles_per_batch),
                  _const_spec((1, D_MODEL)), _const_spec((D_MODEL, D_MODEL)),
                  _const_spec((D_MODEL, N_EXPERTS)), _const_spec((D_MODEL, N_EXPERTS)),
                  _const_spec((1, N_EXPERTS))],
        out_specs=out_specs, out_shape=out_shapes,
        compiler_params=_cparams(("arbitrary",)),
        name="post_mixer_l0",
    )(a_lat, a_ctx, f_lat, f_ctx, x, ctx, mod, mod, mod, norm_g.reshape(1, D_MODEL), w_out_bf, *router)


def _post_l1(gyt, x_all, mod, norm_g, w_glu_t_bf, b_glu, w_out_t_bf, router, n_lat, tiles_per_batch):
    out_specs, out_shapes = _post_out(n_lat)
    return pl.pallas_call(
        _post_l1_kernel,
        grid=(n_lat // TM,),
        in_specs=[pl.BlockSpec((D_MODEL, TM), lambda i: (0, i)),
                  pl.BlockSpec((TM, D_MODEL), lambda i: (i, 0)),
                  _mod_spec(2, tiles_per_batch), _mod_spec(3, tiles_per_batch), _mod_spec(4, tiles_per_batch),
                  _const_spec((1, D_MODEL)), _const_spec((D_MODEL, D_MODEL)), _const_spec((D_MODEL, 1)),
                  _const_spec((D_MODEL, D_MODEL)),
                  _const_spec((D_MODEL, N_EXPERTS)), _const_spec((D_MODEL, N_EXPERTS)),
                  _const_spec((1, N_EXPERTS))],
        out_specs=out_specs, out_shape=out_shapes,
        compiler_params=_cparams(("arbitrary",)),
        name="post_mixer_l1",
    )(gyt, x_all, mod, mod, mod, norm_g.reshape(1, D_MODEL), w_glu_t_bf, b_glu.reshape(D_MODEL, 1),
      w_out_t_bf, *router)


CHUNK = 8
TILE_ROWS = TM * TOP_K + N_EXPERTS * CHUNK
TILE_CHUNKS = TILE_ROWS // CHUNK
BLOCK_CHUNKS = EXPERT_ROWS // CHUNK
SORT_ROWS = 256
ISSUE_UNROLL = 8
FF_CHUNK = 256
HALF = D_MODEL // 2


def _pack_bf16_pairs(y):
    bits = pltpu.bitcast(y, jnp.uint32)
    return (bits[:, :HALF] & jnp.uint32(0xFFFF0000)) | (bits[:, HALF:] >> 16)


def _unpack_bf16_pairs(w):
    hi = pltpu.bitcast(w & jnp.uint32(0xFFFF0000), F32).astype(BF16)
    lo = pltpu.bitcast(w << 16, F32).astype(BF16)
    return hi, lo


def _sort_tile(h2, mf, g, hs_ref, pos_ref, gate_ref, cnt_ref):
    m = mf.astype(BF16)
    r = lax.broadcasted_iota(jnp.int32, (TM, TM), 0)
    c = lax.broadcasted_iota(jnp.int32, (TM, TM), 1)
    rank = _dot((r > c).astype(F32).astype(BF16), m)
    cnt = mf.sum(axis=0, keepdims=True)
    n8 = jnp.floor((cnt + (CHUNK - 1)) * (1.0 / CHUNK))
    er = lax.broadcasted_iota(jnp.int32, (N_EXPERTS, N_EXPERTS), 0)
    ec = lax.broadcasted_iota(jnp.int32, (N_EXPERTS, N_EXPERTS), 1)
    upper = (er < ec).astype(F32).astype(BF16)
    off = CHUNK * _dot(jnp.broadcast_to(n8, (8, N_EXPERTS)).astype(BF16), upper)[0:1]
    pos = rank + off
    slot = _dot(m, upper)
    chosen = mf > 0.5
    lane = lax.broadcasted_iota(jnp.int32, pos_ref.shape, 1)
    p_out = jnp.zeros(pos_ref.shape, F32)
    g_out = jnp.zeros(gate_ref.shape, F32)
    for k in range(TOP_K):
        sel = chosen & (slot == float(k))
        pk = jnp.where(sel, pos, 0.0).sum(axis=-1, keepdims=True)
        gk = jnp.where(sel, g, 0.0).sum(axis=-1, keepdims=True)
        p_out = jnp.where(lane == k, pk, p_out)
        g_out = jnp.where(lane == k, gk, g_out)
    pos_ref[...] = p_out
    gate_ref[...] = g_out
    cnt_ref[...] = jnp.zeros_like(cnt_ref)
    cnt_ref[0:1, 0:N_EXPERTS] = cnt

    pos_t = p_out.T
    hb = h2.astype(BF16)
    rid = lax.broadcasted_iota(jnp.int32, (SORT_ROWS, TM), 0).astype(F32).astype(BF16)
    one, nil = jnp.ones((), BF16), jnp.zeros((), BF16)
    for rc in range(TILE_ROWS // SORT_ROWS):
        rel = (pos_t[0:16, :] - float(rc * SORT_ROWS)).astype(BF16)
        onehot = jnp.where(rid == rel[0:1, :], one, nil)
        for k in range(1, TOP_K):
            onehot = jnp.where(rid == rel[k:k + 1, :], one, onehot)
        hs_ref[rc * SORT_ROWS:(rc + 1) * SORT_ROWS, :] = _pack_bf16_pairs(_dot(onehot, hb))


def _chunk_tables(cnt, n_blocks):
    n_tiles = cnt.shape[0]
    n8 = (cnt + CHUNK - 1) // CHUNK
    lend = jnp.cumsum(n8, axis=1)
    lstart = lend - n8
    cum = jnp.cumsum(n8, axis=0)
    before = cum - n8
    total = cum[-1]
    nblk = (total + BLOCK_CHUNKS - 1) // BLOCK_CHUNKS
    bend = jnp.cumsum(nblk)
    gstart = (bend - nblk) * BLOCK_CHUNKS
    n_used = jnp.maximum(bend[-1], 1)
    blk = jnp.arange(n_blocks, dtype=jnp.int32)
    block_expert = (bend[None, :] <= jnp.minimum(blk, n_used - 1)[:, None]).sum(axis=1).astype(jnp.int32)
    block_expert = jnp.minimum(block_expert, N_EXPERTS - 1)
    experts = jnp.arange(N_EXPERTS, dtype=jnp.int32)
    lc = jnp.arange(TILE_CHUNKS, dtype=jnp.int32)
    e_of = (lend[:, None, :] <= lc[None, :, None]).sum(axis=-1).astype(jnp.int32)
    shift = gstart[None, :] + before - lstart
    pick = e_of[:, :, None] == experts[None, None, :]
    to_sorted = jnp.where(pick, shift[:, None, :], 0).sum(axis=-1) + jnp.where(e_of < N_EXPERTS, lc[None, :], 0)
    of_block = block_expert[:, None] == experts[None, :]
    gstart_b = jnp.where(of_block, gstart[None, :], 0).sum(axis=-1)
    total_b = jnp.where(of_block, total[None, :], 0).sum(axis=-1)

    def per_block(table):
        return jnp.where(of_block[:, :, None], table.T[None, :, :], 0).sum(axis=1)

    cum_b, lstart_b, before_b = per_block(cum), per_block(lstart), per_block(before)
    rel = (blk[:, None] * BLOCK_CHUNKS + jnp.arange(BLOCK_CHUNKS, dtype=jnp.int32)[None, :]) - gstart_b[:, None]
    valid = (blk[:, None] < n_used) & (rel < total_b[:, None])
    tile = jnp.minimum((cum_b[:, None, :] <= rel[:, :, None]).sum(axis=-1), n_tiles - 1).astype(jnp.int32)
    of_tile = tile[:, :, None] == jnp.arange(n_tiles, dtype=jnp.int32)[None, None, :]
    inside = jnp.where(of_tile, (lstart_b - before_b)[:, None, :], 0).sum(axis=-1)
    from_tiles = jnp.where(valid, tile * TILE_CHUNKS + inside + rel, TILE_CHUNKS - 1)
    return (block_expert, n_used.reshape(1).astype(jnp.int32), from_tiles.reshape(-1).astype(jnp.int32),
            to_sorted.reshape(-1).astype(jnp.int32))


def _chunk_rows(chunk):
    start = chunk * CHUNK
    return pl.ds(start if isinstance(start, int) else pl.multiple_of(start, CHUNK), CHUNK)


def _chunk_copy(src_ref, src_chunk, dst_ref, dst_chunk, sem):
    return pltpu.make_async_copy(src_ref.at[_chunk_rows(src_chunk), :], dst_ref.at[_chunk_rows(dst_chunk), :], sem)


def _expert_kernel(be_ref, nu_ref, src_ref, hs_ref, wgu_ref, bgu_ref, wd_ref, bd_ref, o_ref,
                   wgu_bf, wd_bf, x_buf, sems):
    i = pl.program_id(0)
    n_used = nu_ref[0]

    def gather(block, slot, start):
        if not start:
            pltpu.make_async_copy(hs_ref.at[pl.ds(0, EXPERT_ROWS), :], x_buf.at[slot], sems.at[slot]).wait()
            return
        for cidx in range(BLOCK_CHUNKS):
            _chunk_copy(hs_ref, src_ref[block * BLOCK_CHUNKS + cidx], x_buf.at[slot], cidx, sems.at[slot]).start()

    @pl.when(i == 0)
    def _():
        gather(0, 0, True)

    for slot in range(2):
        @pl.when((i + 1 < n_used) & ((i + 1) % 2 == slot))
        def _(slot=slot):
            gather(i + 1, slot, True)

    @pl.when((i == 0) | (be_ref[i] != be_ref[jnp.maximum(i - 1, 0)]))
    def _():
        wgu_bf[...] = wgu_ref[0].astype(BF16)
        wd_bf[...] = wd_ref[0].astype(BF16)

    for slot in range(2):
        @pl.when((i < n_used) & (i % 2 == slot))
        def _(slot=slot):
            gather(i, slot, False)
            x_hi, x_lo = _unpack_bf16_pairs(x_buf[slot])
            y = None
            for j in range(D_MODEL // FF_CHUNK):
                def proj(col0):
                    cols = slice(col0 + j * FF_CHUNK, col0 + (j + 1) * FF_CHUNK)
                    return (_dot(x_hi, wgu_bf[:HALF, cols]) + _dot(x_lo, wgu_bf[HALF:, cols])
                            + bgu_ref[0, :, cols])
                gate = jnp.minimum(proj(0), SWIGLU_LIMIT)
                up = jnp.clip(proj(D_MODEL), -SWIGLU_LIMIT, SWIGLU_LIMIT)
                act = (up + 1.0) * (gate * jax.nn.sigmoid(SWIGLU_ALPHA * gate))
                part = _dot(act.astype(BF16), wd_bf[j * FF_CHUNK:(j + 1) * FF_CHUNK, :])
                y = part if y is None else y + part
            y = y + bd_ref[0]
            o_ref[...] = _pack_bf16_pairs(y.astype(BF16).astype(F32))

    @pl.when(i >= n_used)
    def _():
        o_ref[...] = jnp.zeros_like(o_ref)


def _experts(hs, block_expert, n_used, from_tiles, wgu, bgu, wd, bd):
    n_blocks = block_expert.shape[0]

    def row_map(i, be, nu, src):
        return (i, 0)

    def exp_map(i, be, nu, src):
        return (be[i], 0, 0)

    grid_spec = pltpu.PrefetchScalarGridSpec(
        num_scalar_prefetch=3,
        grid=(n_blocks,),
        in_specs=[pl.BlockSpec(memory_space=pl.ANY),
                  pl.BlockSpec((1, D_MODEL, 2 * D_MODEL), exp_map),
                  pl.BlockSpec((1, 1, 2 * D_MODEL), exp_map),
                  pl.BlockSpec((1, D_MODEL, D_MODEL), exp_map),
                  pl.BlockSpec((1, 1, D_MODEL), exp_map)],
        out_specs=pl.BlockSpec((EXPERT_ROWS, HALF), row_map),
        scratch_shapes=[pltpu.VMEM((D_MODEL, 2 * D_MODEL), BF16), pltpu.VMEM((D_MODEL, D_MODEL), BF16),
                        pltpu.VMEM((2, EXPERT_ROWS, HALF), jnp.uint32), pltpu.SemaphoreType.DMA((2,))],
    )
    return pl.pallas_call(
        _expert_kernel,
        grid_spec=grid_spec,
        out_shape=jax.ShapeDtypeStruct((n_blocks * EXPERT_ROWS, HALF), jnp.uint32),
        compiler_params=pltpu.CompilerParams(dimension_semantics=("arbitrary",),
                                             vmem_limit_bytes=EXPERT_VMEM_LIMIT),
        name="moe_experts",
    )(block_expert, n_used, from_tiles, hs, wgu, bgu.reshape(N_EXPERTS, 1, 2 * D_MODEL), wd,
      bd.reshape(N_EXPERTS, 1, D_MODEL))


def _combine_kernel(dst_ref, ys_ref, pos_ref, gate_ref, x_ref, g2_ref, ng_ref, o_ref, y_buf, sems, *, final_norm):
    i = pl.program_id(0)
    n_tiles = pl.num_programs(0)

    def gather(tile, slot, start):
        if not start:
            pltpu.make_async_copy(ys_ref.at[pl.ds(0, TILE_ROWS), :], y_buf.at[slot], sems.at[slot]).wait()
            return

        def body(g, carry):
            for j in range(ISSUE_UNROLL):
                cidx = g * ISSUE_UNROLL + j
                _chunk_copy(ys_ref, dst_ref[tile * TILE_CHUNKS + cidx], y_buf.at[slot], cidx, sems.at[slot]).start()
            return carry
        lax.fori_loop(0, TILE_CHUNKS // ISSUE_UNROLL, body, 0)

    @pl.when(i == 0)
    def _():
        gather(0, 0, True)

    for slot in range(2):
        @pl.when((i + 1 < n_tiles) & ((i + 1) % 2 == slot))
        def _(slot=slot):
            gather(i + 1, slot, True)

    pos = pos_ref[...]
    gate = gate_ref[...]
    for slot in range(2):
        @pl.when(i % 2 == slot)
        def _(slot=slot):
            gather(i, slot, False)
            acc_hi = jnp.zeros((TM, HALF), F32)
            acc_lo = jnp.zeros((TM, HALF), F32)
            rid = lax.broadcasted_iota(jnp.int32, (TM, SORT_ROWS), 1).astype(F32).astype(BF16)
            gate_b = gate.astype(BF16)
            for rc in range(TILE_ROWS // SORT_ROWS):
                rel = (pos - float(rc * SORT_ROWS)).astype(BF16)
                w = jnp.where(rid == rel[:, 0:1], gate_b[:, 0:1], jnp.zeros((), BF16))
                for k in range(1, TOP_K):
                    w = jnp.where(rid == rel[:, k:k + 1], gate_b[:, k:k + 1], w)
                y_hi, y_lo = _unpack_bf16_pairs(y_buf[slot, rc * SORT_ROWS:(rc + 1) * SORT_ROWS, :])
                acc_hi = acc_hi + _dot(w, y_hi)
                acc_lo = acc_lo + _dot(w, y_lo)
            x = x_ref[...] + g2_ref[0] * jnp.concatenate([acc_hi, acc_lo], axis=-1)
            if final_norm:
                ms = jnp.mean(x * x, axis=-1, keepdims=True)
                x = x * lax.rsqrt(ms + EPS) * ng_ref[...]
            o_ref[...] = x


def _combine(ys, to_sorted, pos4, gate4, x_res, mod, norm_g, tiles_per_batch, final_norm):
    n_tok = x_res.shape[0]
    tok = pl.BlockSpec((TM, D_MODEL), lambda i, dst: (i, 0))
    wide = pl.BlockSpec((TM, 128), lambda i, dst: (i, 0))
    tiles, bsz = tiles_per_batch
    grid_spec = pltpu.PrefetchScalarGridSpec(
        num_scalar_prefetch=1,
        grid=(n_tok // TM,),
        in_specs=[pl.BlockSpec(memory_space=pl.ANY), wide, wide, tok,
                  pl.BlockSpec((1, 1, D_MODEL), lambda i, dst: (jnp.minimum(i // tiles, bsz) * 6 + 5, 0, 0)),
                  pl.BlockSpec((1, D_MODEL), lambda i, dst: (0, 0))],
        out_specs=tok,
        scratch_shapes=[pltpu.VMEM((2, TILE_ROWS, HALF), jnp.uint32), pltpu.SemaphoreType.DMA((2,))],
    )
    return pl.pallas_call(
        functools.partial(_combine_kernel, final_norm=final_norm),
        grid_spec=grid_spec,
        out_shape=jax.ShapeDtypeStruct((n_tok, D_MODEL), F32),
        compiler_params=_cparams(("arbitrary",)),
        name="moe_combine",
    )(to_sorted, ys, pos4, gate4, x_res, mod, norm_g.reshape(1, D_MODEL))


def _moe(hs, pos4, gate4, cnt, x_res, mod, moe_w, tiles_per_batch, norm_g, final_norm):
    wgu, bgu, wd, bd = moe_w
    n_tiles = x_res.shape[0] // TM
    n_blocks = -(-n_tiles * TILE_CHUNKS // BLOCK_CHUNKS) + N_EXPERTS
    cnt_i = cnt.reshape(n_tiles, 8, 128)[:, 0, :N_EXPERTS].astype(jnp.int32)
    block_expert, n_used, from_tiles, to_sorted = _chunk_tables(cnt_i, n_blocks)
    ys = _experts(hs, block_expert, n_used, from_tiles, wgu, bgu, wd, bd)
    return _combine(ys, to_sorted, pos4, gate4, x_res, mod, norm_g, tiles_per_batch, final_norm)


def _s5_tables(lam_re, lam_im, log_dt, b_re, b_im, c_re, c_im, d_skip):
    q = S5_Q
    lam = lax.complex(jnp.minimum(lam_re.astype(F32), LAMBDA_RE_MAX), lam_im.astype(F32))
    dt = jnp.exp(log_dt.astype(F32))[..., None]
    a = jnp.exp(lam * dt)
    bbar = ((a - 1) / lam)[..., None] * lax.complex(b_re.astype(F32), b_im.astype(F32))
    cc = lax.complex(c_re.astype(F32), c_im.astype(F32))
    steps = jnp.arange(q + 1, dtype=F32)
    pw = jnp.exp((lam * dt)[:, :, None, :] * steps[None, None, :, None])
    kern = jnp.einsum('dgop,dgkp,dgpc->dgcok', cc, pw[:, :, :q], bbar).real
    eye = jnp.eye(SSM_GROUP, dtype=F32)
    k_fwd = kern[0].at[..., 0].add(kern[1][..., 0] + d_skip.astype(F32)[:, :, None] * eye[None])
    k_bwd = jnp.concatenate([jnp.zeros_like(kern[1][..., :1]), kern[1][..., :0:-1]], axis=-1)
    k_fwd = k_fwd.reshape(SSM_GROUPS, SSM_GROUP * SSM_GROUP, q)
    k_bwd = k_bwd.reshape(SSM_GROUPS, SSM_GROUP * SSM_GROUP, q)
    def both(f, b):
        return jnp.concatenate([f, b], axis=-1)

    e_in = both(pw[0][:, ::-1][:, 1:], pw[1][:, :q])
    b_in = both(jnp.transpose(bbar[0], (0, 2, 1)), jnp.transpose(bbar[1], (0, 2, 1)))
    e_out = jnp.transpose(both(pw[0][:, 1:], pw[1][:, ::-1][:, :q]), (0, 2, 1))
    c_out = jnp.transpose(both(cc[0], cc[1]), (0, 2, 1))
    a_q = both(pw[0][:, q], pw[1][:, q]).reshape(SSM_GROUPS, 1, 2 * SSM_STATE)
    parts = [t for z in (e_in, b_in, e_out, c_out, a_q) for t in (z.real, z.imag)]
    return [k_fwd, k_bwd] + parts


def _s5_kernel(u_ref, kf_ref, kb_ref, er_ref, ei_ref, br_ref, bi_ref, pr_ref, pi_ref, cr_ref, ci_ref,
               ar_ref, ai_ref, o_ref, w_ref, s_ref, h_ref, *, bsz, n_lat_chunks, n_ctx_chunks):
    q = S5_Q
    half = 2 * SSM_STATE
    n_lat = bsz * n_lat_chunks

    causal = (lax.broadcasted_iota(jnp.int32, (q, q), 1) >= lax.broadcasted_iota(jnp.int32, (q, q), 0))

    def build(c, carry):
        for o in range(SSM_GROUP):
            row = pl.ds(c * SSM_GROUP + o, 1)
            kf = pltpu.roll(jnp.broadcast_to(kf_ref[0, row, :], (q, q)), 0, 1, stride=1, stride_axis=0)
            kb = pltpu.roll(jnp.broadcast_to(kb_ref[0, row, :], (q, q)), 0, 1, stride=1, stride_axis=0)
            w_ref[pl.ds(pl.multiple_of(c * q, q), q), o * q:(o + 1) * q] = jnp.where(causal, kf, kb).astype(BF16)
        return carry

    lax.fori_loop(0, SSM_GROUP, build, 0)

    u = u_ref[...]
    er, ei = er_ref[0], ei_ref[0]
    s_re = None
    for c in range(SSM_GROUP):
        br, bi = br_ref[0, c:c + 1, :], bi_ref[0, c:c + 1, :]
        w_re = (er * br - ei * bi).astype(BF16)
        w_im = (er * bi + ei * br).astype(BF16)
        s_re = _dot(u[c], w_re) if s_re is None else s_re + _dot(u[c], w_re)
        s_im = _dot(u[c], w_im) if c == 0 else s_im + _dot(u[c], w_im)
    s_ref[0] = s_re
    s_ref[1] = s_im

    fwd = lax.broadcasted_iota(jnp.int32, (bsz, half), 1) < SSM_STATE
    ar = jnp.broadcast_to(ar_ref[0], (bsz, half))
    ai = jnp.broadcast_to(ai_ref[0], (bsz, half))

    def chunk_rows(base, j, per_sample):
        return pl.ds(base + j, bsz, stride=per_sample)

    def load(base, j, n):
        rows_f = chunk_rows(base, j, n)
        rows_b = chunk_rows(base, n - 1 - j, n)
        return (jnp.where(fwd, s_ref[0, rows_f, :], s_ref[0, rows_b, :]),
                jnp.where(fwd, s_ref[1, rows_f, :], s_ref[1, rows_b, :]))

    def step(hr, hi, sr, si):
        return ar * hr - ai * hi + sr, ar * hi + ai * hr + si

    def ctx_body(j, carry):
        return step(*carry, *load(n_lat, j, n_ctx_chunks))

    def lat_body(j, carry):
        hr, hi = carry
        rows_f = chunk_rows(0, j, n_lat_chunks)
        rows_b = chunk_rows(0, n_lat_chunks - 1 - j, n_lat_chunks)
        h_ref[0, rows_f, :] = jnp.where(fwd, hr, 0.0)
        h_ref[1, rows_f, :] = jnp.where(fwd, hi, 0.0)
        h_ref[2, rows_b, :] = jnp.where(fwd, 0.0, hr)
        h_ref[3, rows_b, :] = jnp.where(fwd, 0.0, hi)
        return step(hr, hi, *load(0, j, n_lat_chunks))

    zero = jnp.zeros((bsz, half), F32)
    carry = lax.fori_loop(0, n_ctx_chunks, ctx_body, (zero, zero))
    lax.fori_loop(0, n_lat_chunks, lat_body, carry)

    ucat = jnp.concatenate([u[c][:n_lat] for c in range(SSM_GROUP)], axis=-1)
    y = _dot(ucat, w_ref[...])
    h_re = (h_ref[0] + h_ref[2]).astype(BF16)
    h_im = (h_ref[1] + h_ref[3]).astype(BF16)
    pr, pi = pr_ref[0], pi_ref[0]
    cr_all, ci_all = cr_ref[0], ci_ref[0]
    for o in range(SSM_GROUP):
        cr, ci = cr_all[:, o:o + 1], ci_all[:, o:o + 1]
        w_re = (pr * cr - pi * ci).astype(BF16)
        w_im = (pr * ci + pi * cr).astype(BF16)
        y_o = y[:, o * q:(o + 1) * q] + _dot(h_re, w_re) - _dot(h_im, w_im)
        o_ref[o] = jax.nn.gelu(y_o).astype(o_ref.dtype)


def _s5(u3, tables, bsz, n_lat_chunks, n_ctx_chunks):
    q = S5_Q
    n_chunks = u3.shape[1]
    n_lat = bsz * n_lat_chunks
    assert n_chunks == n_lat + bsz * n_ctx_chunks

    def gspec(*shape):
        return pl.BlockSpec((1,) + shape, lambda g: (g,) + tuple(0 for _ in shape))

    return pl.pallas_call(
        functools.partial(_s5_kernel, bsz=bsz, n_lat_chunks=n_lat_chunks, n_ctx_chunks=n_ctx_chunks),
        grid=(SSM_GROUPS,),
        in_specs=[pl.BlockSpec((SSM_GROUP, n_chunks, q), lambda g: (g, 0, 0)),
                  gspec(SSM_GROUP * SSM_GROUP, q), gspec(SSM_GROUP * SSM_GROUP, q),
                  gspec(q, 2 * SSM_STATE), gspec(q, 2 * SSM_STATE),
                  gspec(SSM_GROUP, 2 * SSM_STATE), gspec(SSM_GROUP, 2 * SSM_STATE),
                  gspec(2 * SSM_STATE, q), gspec(2 * SSM_STATE, q),
                  gspec(2 * SSM_STATE, SSM_GROUP), gspec(2 * SSM_STATE, SSM_GROUP),
                  gspec(1, 2 * SSM_STATE), gspec(1, 2 * SSM_STATE)],
        out_specs=pl.BlockSpec((SSM_GROUP, n_lat, q), lambda g: (g, 0, 0)),
        out_shape=jax.ShapeDtypeStruct((SSM_GROUPS * SSM_GROUP, n_lat, q), BF16),
        scratch_shapes=[pltpu.VMEM((SSM_GROUP * q, SSM_GROUP * q), BF16),
                        pltpu.VMEM((2, n_chunks, 2 * SSM_STATE), F32),
                        pltpu.VMEM((4, n_lat, 2 * SSM_STATE), F32)],
        compiler_params=_cparams(("arbitrary",)),
        name="s5_scan",
    )(u3, *tables)


def _moe_weights(w_gate_up, b_gate_up, w_down, b_down):
    return w_gate_up, b_gate_up, w_down, b_down


def kernel(x, c, ctx, c_ctx, l0_ada_w, l0_ada_b, l0_norm_mix, l0_w_in, l0_rpb, l0_w_fourier, l0_w_out, l0_norm_ffn, l0_router_w, l0_router_b, l0_w_gate_up, l0_b_gate_up, l0_w_down, l0_b_down, l1_ada_w, l1_ada_b, l1_norm_mix, l1_w_in, l1_lambda_re, l1_lambda_im, l1_log_dt, l1_b_re, l1_b_im, l1_c_re, l1_c_im, l1_d_skip, l1_w_glu, l1_b_glu, l1_w_out, l1_norm_ffn, l1_router_w, l1_router_b, l1_w_gate_up, l1_b_gate_up, l1_w_down, l1_b_down, final_norm):
    bsz, seq, _ = x.shape
    ctx_len = ctx.shape[1]
    n_lat = bsz * seq
    n_ctx = bsz * ctx_len
    n_tok = n_lat + n_ctx
    assert seq % TM == 0 and n_ctx % TM == 0
    x2 = x.reshape(n_lat, D_MODEL)
    ctx2 = ctx.reshape(n_ctx, D_MODEL)

    n_mod = bsz + 1
    c_pad = jnp.zeros((-(-n_mod // 8) * 8, D_MODEL), F32).at[:bsz].set(c).at[bsz].set(c_ctx)
    mod0 = _ada(c_pad, l0_ada_w, l0_ada_b)[:n_mod].reshape(n_mod * 6, 1, D_MODEL)
    mod1 = _ada(c_pad, l1_ada_w, l1_ada_b)[:n_mod].reshape(n_mod * 6, 1, D_MODEL)
    tpb = (seq // TM, bsz)

    q, k, v, f = _modproj([x2, ctx2], mod0, l0_norm_mix, l0_w_in.astype(BF16),
                          (NA_WIDTH, NA_WIDTH, NA_WIDTH, FOURIER_WIDTH), n_lat, n_tok, tpb,
                          q_scale=HEAD_DIM ** -0.5)
    a_lat, a_ctx = _attention(q, k, v, l0_rpb, bsz, seq, ctx_len)
    bdw = (jnp.eye(FOURIER_GROUPS, dtype=F32)[:, None, :, None] * l0_w_fourier[:, :, None, :]).reshape(
        FOURIER_WIDTH, FOURIER_WIDTH).astype(BF16)
    f_lat = _fourier(f, bdw, bsz, seq, 0, TM)
    f_ctx = _fourier(f, bdw, bsz, ctx_len, n_lat // ctx_len, ctx_len)
    x_all, *routed = _post_l0(a_lat, a_ctx, f_lat, f_ctx, x2, ctx2, mod0, l0_norm_ffn,
                              l0_w_out.astype(BF16), _router_args(l0_router_w, l0_router_b), n_lat, n_tok, tpb)
    x_all = _moe(*routed, x_all, mod0, _moe_weights(l0_w_gate_up, l0_b_gate_up, l0_w_down, l0_b_down),
                 tpb, final_norm, False)

    assert seq % S5_Q == 0 and ctx_len % S5_Q == 0
    ut = _modproj_t(x_all, mod1, l1_norm_mix, l1_w_in.T.astype(BF16), n_tok, tpb)
    tables = _s5_tables(l1_lambda_re, l1_lambda_im, l1_log_dt, l1_b_re, l1_b_im, l1_c_re, l1_c_im, l1_d_skip)
    yt = _s5(ut.reshape(D_MODEL, n_tok // S5_Q, S5_Q), tables, bsz, seq // S5_Q, ctx_len // S5_Q)
    gy = yt.reshape(D_MODEL, n_lat)
    x1, *routed = _post_l1(gy, x_all, mod1, l1_norm_ffn, l1_w_glu.T.astype(BF16), l1_b_glu,
                           l1_w_out.T.astype(BF16), _router_args(l1_router_w, l1_router_b), n_lat, tpb)
    out = _moe(*routed, x1, mod1, _moe_weights(l1_w_gate_up, l1_b_gate_up, l1_w_down, l1_b_down),
               tpb, final_norm, True)
    return out.reshape(bsz, seq, D_MODEL)
```

```python
import functools
import math

import numpy as np
import jax
import jax.numpy as jnp
from jax import lax
from jax.experimental import pallas as pl
from jax.experimental.pallas import tpu as pltpu

F32 = jnp.float32
BF16 = jnp.bfloat16

D_MODEL = 1024
GRID_W = 64
HEAD_DIM = 64
NA_WIDTH = 512
NA_HEADS = 8
MAX_KR = 8
KC = 16
FOURIER_DIM = 64
FOURIER_GROUPS = 8
FOURIER_WIDTH = 512
SSM_GROUP = 16
SSM_GROUPS = 64
SSM_STATE = 64
LAMBDA_RE_MAX = -1e-4
N_EXPERTS = 32
TOP_K = 4
SWIGLU_ALPHA = 1.702
SWIGLU_LIMIT = 7.0
EPS = 1e-6

TM = 512
EXPERT_ROWS = 512
S5_Q = 128
VMEM_LIMIT = 48 * 1024 * 1024
EXPERT_VMEM_LIMIT = 58 * 1024 * 1024


def _cparams(sem):
    return pltpu.CompilerParams(dimension_semantics=sem, vmem_limit_bytes=VMEM_LIMIT)


def _split_bf16(a):
    hi = a.astype(BF16)
    lo = (a - hi.astype(F32)).astype(BF16)
    return hi, lo


def _dot(a, b):
    return jnp.dot(a, b, preferred_element_type=F32)


def _dot_nt(a, b):
    return lax.dot_general(a, b, (((1,), (1,)), ((), ())), preferred_element_type=F32)


def _ada_kernel(c_ref, w_ref, b_ref, o_ref):
    c = c_ref[...]
    s = c * jax.nn.sigmoid(c)
    s_hi, s_lo = _split_bf16(s)
    w_hi, w_lo = _split_bf16(w_ref[...])
    o_ref[...] = _dot(s_hi, w_hi) + _dot(s_lo, w_hi) + _dot(s_hi, w_lo) + b_ref[...]


def _ada(c_pad, ada_w, ada_b):
    n = ada_w.shape[1]
    tn = 1024
    return pl.pallas_call(
        _ada_kernel,
        grid=(n // tn,),
        in_specs=[pl.BlockSpec((c_pad.shape[0], D_MODEL), lambda j: (0, 0)),
                  pl.BlockSpec((D_MODEL, tn), lambda j: (0, j)),
                  pl.BlockSpec((1, tn), lambda j: (0, j))],
        out_specs=pl.BlockSpec((c_pad.shape[0], tn), lambda j: (0, j)),
        out_shape=jax.ShapeDtypeStruct((c_pad.shape[0], n), F32),
        compiler_params=_cparams(("arbitrary",)),
        name="ada_mod",
    )(c_pad, ada_w, ada_b.reshape(1, n))


def _mod_spec(which, tiles_per_batch):
    tiles, bsz = tiles_per_batch
    return pl.BlockSpec((1, 1, D_MODEL), lambda i: (jnp.minimum(i // tiles, bsz) * 6 + which, 0, 0))


def _modulate(x, g, sh, sc):
    ms = jnp.mean(x * x, axis=-1, keepdims=True)
    y = x * lax.rsqrt(ms + EPS) * g
    return y * (1.0 + sc) + sh


def _modproj_kernel(*refs, n_lat_tiles, two_src, widths, q_scale):
    if two_src:
        x_ref, c_ref, sh_ref, sc_ref, g_ref, w_ref = refs[:6]
        o_refs = refs[6:]
        i = pl.program_id(0)
        x = jnp.where(i < n_lat_tiles, x_ref[...], c_ref[...])
    else:
        x_ref, sh_ref, sc_ref, g_ref, w_ref = refs[:5]
        o_refs = refs[5:]
        x = x_ref[...]
    h = _modulate(x, g_ref[...], sh_ref[0], sc_ref[0]).astype(BF16)
    off = 0
    for j, (o_ref, wd) in enumerate(zip(o_refs, widths)):
        y = _dot(h, w_ref[:, off:off + wd])
        if j == 0 and q_scale != 1.0:
            y = y * q_scale
        o_ref[...] = y.astype(o_ref.dtype)
        off += wd


def _modproj(x_srcs, mod, norm_g, w_bf, widths, n_lat, n_tok, tiles_per_batch, q_scale=1.0):
    n_tiles = n_tok // TM
    n_lat_tiles = n_lat // TM
    two_src = len(x_srcs) == 2
    n_out = w_bf.shape[1]
    if two_src:
        x_specs = [pl.BlockSpec((TM, D_MODEL), lambda i: (jnp.minimum(i, n_lat_tiles - 1), 0)),
                   pl.BlockSpec((TM, D_MODEL), lambda i: (jnp.maximum(i - n_lat_tiles, 0), 0))]
    else:
        x_specs = [pl.BlockSpec((TM, D_MODEL), lambda i: (i, 0))]
    return pl.pallas_call(
        functools.partial(_modproj_kernel, n_lat_tiles=n_lat_tiles, two_src=two_src,
                          widths=tuple(widths), q_scale=q_scale),
        grid=(n_tiles,),
        in_specs=x_specs + [_mod_spec(0, tiles_per_batch), _mod_spec(1, tiles_per_batch),
                            pl.BlockSpec((1, D_MODEL), lambda i: (0, 0)),
                            pl.BlockSpec((D_MODEL, n_out), lambda i: (0, 0))],
        out_specs=[pl.BlockSpec((TM, wd), lambda i: (i, 0)) for wd in widths],
        out_shape=[jax.ShapeDtypeStruct((n_tok, wd), BF16) for wd in widths],
        compiler_params=_cparams(("arbitrary",)),
        name="modulate_in_proj",
    )(*x_srcs, mod, mod, norm_g.reshape(1, D_MODEL), w_bf)


def _modproj_t_kernel(x_ref, sh_ref, sc_ref, g_ref, wt_ref, o_ref):
    h = _modulate(x_ref[...], g_ref[...], sh_ref[0], sc_ref[0]).astype(BF16)
    o_ref[...] = _dot_nt(wt_ref[...], h).astype(o_ref.dtype)


def _modproj_t(x_all, mod, norm_g, wt_bf, n_tok, tiles_per_batch):
    n_out = wt_bf.shape[0]
    return pl.pallas_call(
        _modproj_t_kernel,
        grid=(n_tok // TM,),
        in_specs=[pl.BlockSpec((TM, D_MODEL), lambda i: (i, 0)),
                  _mod_spec(0, tiles_per_batch), _mod_spec(1, tiles_per_batch),
                  pl.BlockSpec((1, D_MODEL), lambda i: (0, 0)),
                  pl.BlockSpec((n_out, D_MODEL), lambda i: (0, 0))],
        out_specs=pl.BlockSpec((n_out, TM), lambda i: (0, i)),
        out_shape=jax.ShapeDtypeStruct((n_out, n_tok), BF16),
        compiler_params=_cparams(("arbitrary",)),
        name="modulate_in_proj_t",
    )(x_all, mod, mod, norm_g.reshape(1, D_MODEL), wt_bf)


Q_ROWS = 4
BAND_ROWS = Q_ROWS + MAX_KR
EDGE_BLOCKS = -(-(MAX_KR // 2) // Q_ROWS)
N_BIAS_CLASSES = 2 * EDGE_BLOCKS + 1


def _na_bias_table(rpb, rows):
    qc = np.arange(GRID_W)[:, None]
    kc = np.arange(GRID_W)[None, :]
    win0 = np.clip(qc - KC // 2, 0, GRID_W - KC)
    col_valid = (kc >= win0) & (kc < win0 + KC)
    col_off = np.clip(kc - qc, -(KC - 1), KC - 1) + KC - 1
    cb = rpb.astype(F32)[:, :, col_off]
    cb = jnp.where(jnp.asarray(col_valid)[None, None], cb, -jnp.inf)
    n_blk = rows // Q_ROWS
    idx = np.zeros((N_BIAS_CLASSES, Q_ROWS, BAND_ROWS), np.int64)
    ok = np.zeros((N_BIAS_CLASSES, Q_ROWS, BAND_ROWS), bool)
    representatives = list(range(EDGE_BLOCKS + 1)) + list(range(n_blk - EDGE_BLOCKS, n_blk))
    for cls, blk in enumerate(representatives):
        band0 = int(np.clip(blk * Q_ROWS - MAX_KR // 2, 0, rows - BAND_ROWS))
        for rho in range(Q_ROWS):
            r = blk * Q_ROWS + rho
            r0 = int(np.clip(r - MAX_KR // 2, 0, rows - MAX_KR))
            for kap in range(BAND_ROWS):
                kr = band0 + kap
                ok[cls, rho, kap] = r0 <= kr < r0 + MAX_KR
                idx[cls, rho, kap] = np.clip(kr - r + MAX_KR - 1, 0, 2 * MAX_KR - 2)
    t = cb[:, idx]
    t = jnp.where(jnp.asarray(ok)[None, :, :, :, None, None], t, -jnp.inf)
    t = jnp.transpose(t, (1, 0, 2, 4, 3, 5))
    return t.reshape(N_BIAS_CLASSES, NA_HEADS, Q_ROWS * GRID_W, BAND_ROWS * GRID_W)


def _attn_kernel(q_ref, k_ref, v_ref, qc_ref, kc_ref, vc_ref, bias_ref, o_ref, oc_ref, *, rows):
    lane = lax.broadcasted_iota(jnp.int32, (1, 2 * HEAD_DIM), 1)
    first = lane < HEAD_DIM
    kc = kc_ref[...]
    vc = vc_ref[...]
    zero = jnp.zeros((), BF16)

    def softmax_pv(s_list, v_list):
        m = s_list[0].max(axis=-1, keepdims=True)
        for s in s_list[1:]:
            m = jnp.maximum(m, s.max(axis=-1, keepdims=True))
        den = None
        acc = None
        for s, vv in zip(s_list, v_list):
            p = jnp.exp(s - m)
            ps = p.sum(axis=-1, keepdims=True)
            den = ps if den is None else den + ps
            pv = _dot(p.astype(BF16), vv)
            acc = pv if acc is None else acc + pv
        return acc / den

    n_blk = rows // Q_ROWS
    q_len = Q_ROWS * GRID_W

    def block_body(i, carry):
        band0 = jnp.clip(i * Q_ROWS - MAX_KR // 2, 0, rows - BAND_ROWS)
        cls = jnp.where(i < EDGE_BLOCKS, i,
                        jnp.where(i >= n_blk - EDGE_BLOCKS, i - (n_blk - N_BIAS_CLASSES), EDGE_BLOCKS))
        rows_q = pl.ds(pl.multiple_of(i * q_len, q_len), q_len)
        q_b = q_ref[rows_q, :]
        band = pl.ds(pl.multiple_of(band0 * GRID_W, GRID_W), BAND_ROWS * GRID_W)
        kb = k_ref[band, :]
        vb = v_ref[band, :]
        outs = []
        for hh in range(2):
            q_h = jnp.where(first if hh == 0 else ~first, q_b, zero)
            s_w = _dot_nt(q_h, kb) + bias_ref[cls, hh]
            s_c = _dot_nt(q_h, kc)
            outs.append(softmax_pv([s_w, s_c], [vb, vc]))
        o_ref[rows_q, :] = jnp.where(first, outs[0], outs[1]).astype(o_ref.dtype)
        return carry

    lax.fori_loop(0, n_blk, block_body, 0)

    q_c = qc_ref[...]
    outs = []
    for hh in range(2):
        q_h = jnp.where(first if hh == 0 else ~first, q_c, zero)
        outs.append(softmax_pv([_dot_nt(q_h, kc)], [vc]))
    oc_ref[...] = jnp.where(first, outs[0], outs[1]).astype(oc_ref.dtype)


def _attention(q, k, v, rpb, bsz, seq, ctx_len):
    rows = seq // GRID_W
    assert rows % Q_ROWS == 0 and rows >= max(N_BIAS_CLASSES * Q_ROWS, BAND_ROWS) and seq % ctx_len == 0
    bias = _na_bias_table(rpb, rows)
    cb0 = bsz * seq // ctx_len
    lat = pl.BlockSpec((seq, 2 * HEAD_DIM), lambda hp, b: (b, hp))
    ctx = pl.BlockSpec((ctx_len, 2 * HEAD_DIM), lambda hp, b: (cb0 + b, hp))
    return pl.pallas_call(
        functools.partial(_attn_kernel, rows=rows),
        grid=(NA_HEADS // 2, bsz),
        in_specs=[lat, lat, lat, ctx, ctx, ctx,
                  pl.BlockSpec((N_BIAS_CLASSES, 2, Q_ROWS * GRID_W, BAND_ROWS * GRID_W),
                               lambda hp, b: (0, hp, 0, 0))],
        out_specs=[pl.BlockSpec((seq, 2 * HEAD_DIM), lambda hp, b: (b, hp)),
                   pl.BlockSpec((ctx_len, 2 * HEAD_DIM), lambda hp, b: (b, hp))],
        out_shape=[jax.ShapeDtypeStruct((bsz * seq, NA_WIDTH), BF16),
                   jax.ShapeDtypeStruct((bsz * ctx_len, NA_WIDTH), BF16)],
        compiler_params=_cparams(("arbitrary", "arbitrary")),
        name="na_attention",
    )(q, k, v, q, k, v, bias)


@functools.lru_cache(maxsize=None)
def _dft_tables(n):
    j = np.arange(n, dtype=np.int64)
    ang = 2.0 * np.pi * ((j[:, None] * j[None, :]) % n).astype(np.float64) / n
    return np.cos(ang).astype(BF16), np.sin(ang).astype(BF16)


def _fourier_kernel(cos_ref, sin_ref, f_ref, bdc_ref, bds_ref, bdw_ref, o_ref, *, scale):
    fb = f_ref[...]
    zr = _dot(cos_ref[...], fb)
    zs = _dot(sin_ref[...], fb)
    fr = (_dot(zr.astype(BF16), bdc_ref[...]) - _dot(zs.astype(BF16), bds_ref[...])) * scale
    o_ref[...] = _dot(fr.astype(BF16), bdw_ref[...]).astype(o_ref.dtype)


def _fourier(f_all, bdw, bsz, n, row_block0, tm):
    cos_n, sin_n = _dft_tables(n)
    cos64, sin64 = _dft_tables(FOURIER_DIM)
    eye = np.eye(FOURIER_GROUPS)
    bdc = jnp.asarray(np.kron(eye, cos64.astype(np.float32)), BF16)
    bds = jnp.asarray(np.kron(eye, sin64.astype(np.float32)), BF16)
    mt = n // tm
    small = pl.BlockSpec((FOURIER_WIDTH, FOURIER_WIDTH), lambda b, m: (0, 0))
    return pl.pallas_call(
        functools.partial(_fourier_kernel, scale=1.0 / math.sqrt(n * FOURIER_DIM)),
        grid=(bsz, mt),
        in_specs=[pl.BlockSpec((tm, n), lambda b, m: (m, 0)),
                  pl.BlockSpec((tm, n), lambda b, m: (m, 0)),
                  pl.BlockSpec((n, FOURIER_WIDTH), lambda b, m: (row_block0 + b, 0)),
                  small, small, small],
        out_specs=pl.BlockSpec((tm, FOURIER_WIDTH), lambda b, m: (b * mt + m, 0)),
        out_shape=jax.ShapeDtypeStruct((bsz * n, FOURIER_WIDTH), BF16),
        compiler_params=_cparams(("arbitrary", "arbitrary")),
        name="fourier_mix",
    )(jnp.asarray(cos_n), jnp.asarray(sin_n), f_all, bdc, bds, bdw)


def _route(h2, rw_hi_ref, rw_lo_ref, rb_ref):
    h_hi, h_lo = _split_bf16(h2)
    logits = (_dot(h_hi, rw_hi_ref[...]) + _dot(h_lo, rw_hi_ref[...]) + _dot(h_hi, rw_lo_ref[...])
              + rb_ref[...])
    lane = lax.broadcasted_iota(jnp.int32, logits.shape, 1).astype(F32)
    work = logits
    sel_any = jnp.zeros(logits.shape, jnp.bool_)
    top = None
    for _ in range(TOP_K):
        m = work.max(axis=-1, keepdims=True)
        if top is None:
            top = m
        idx = jnp.where(work == m, lane, float(N_EXPERTS)).min(axis=-1, keepdims=True)
        sel = lane == idx
        sel_any = sel_any | sel
        work = jnp.where(sel, -jnp.inf, work)
    e = jnp.where(sel_any, jnp.exp(logits - top), 0.0)
    return sel_any.astype(F32), e / e.sum(axis=-1, keepdims=True)


def _post_l0_kernel(a_ref, ac_ref, f_ref, fc_ref, x_ref, c_ref, g1_ref, sh_ref, sc_ref, ng_ref, w_ref,
                    rwh_ref, rwl_ref, rb_ref, xo_ref, hs_ref, pos_ref, gate_ref, cnt_ref, tri_ref, *, n_lat_tiles):
    lat = pl.program_id(0) < n_lat_tiles
    a = jnp.where(lat, a_ref[...], ac_ref[...])
    f = jnp.where(lat, f_ref[...], fc_ref[...])
    x = jnp.where(lat, x_ref[...], c_ref[...])
    y = _dot(a, w_ref[:NA_WIDTH, :]) + _dot(f, w_ref[NA_WIDTH:, :])
    x = x + g1_ref[0] * y
    xo_ref[...] = x
    h2 = _modulate(x, ng_ref[...], sh_ref[0], sc_ref[0])
    _sort_tile(h2, *_route(h2, rwh_ref, rwl_ref, rb_ref), hs_ref, pos_ref, gate_ref, cnt_ref, tri_ref)


def _post_l1_kernel(yt_ref, x_ref, g1_ref, sh_ref, sc_ref, ng_ref, wgt_ref, bg_ref, wot_ref,
                    rwh_ref, rwl_ref, rb_ref, xo_ref, hs_ref, pos_ref, gate_ref, cnt_ref, tri_ref):
    gyt = yt_ref[...]
    zt = _dot(wgt_ref[...], gyt) + bg_ref[...]
    vt = (gyt.astype(F32) * jax.nn.sigmoid(zt)).astype(BF16)
    x = x_ref[...] + g1_ref[0] * _dot(wot_ref[...], vt).T
    xo_ref[...] = x
    h2 = _modulate(x, ng_ref[...], sh_ref[0], sc_ref[0])
    _sort_tile(h2, *_route(h2, rwh_ref, rwl_ref, rb_ref), hs_ref, pos_ref, gate_ref, cnt_ref, tri_ref)


def _router_args(router_w, router_b):
    rw_hi = router_w.astype(BF16)
    rw_lo = (router_w - rw_hi.astype(F32)).astype(BF16)
    return rw_hi, rw_lo, router_b.reshape(1, N_EXPERTS).astype(F32)


def _post_out(n_tok):
    n_tiles = n_tok // TM
    wide = pl.BlockSpec((TM, 128), lambda i: (i, 0))
    specs = [pl.BlockSpec((TM, D_MODEL), lambda i: (i, 0)),
             pl.BlockSpec((TILE_ROWS, HALF), lambda i: (i, 0)), wide, wide,
             pl.BlockSpec((8, 128), lambda i: (i, 0))]
    shapes = [jax.ShapeDtypeStruct((n_tok, D_MODEL), F32),
              jax.ShapeDtypeStruct((n_tiles * TILE_ROWS, HALF), jnp.uint32),
              jax.ShapeDtypeStruct((n_tok, 128), F32),
              jax.ShapeDtypeStruct((n_tok, 128), F32),
              jax.ShapeDtypeStruct((n_tiles * 8, 128), F32)]
    return specs, shapes


def _const_spec(shape):
    return pl.BlockSpec(shape, lambda i: tuple(0 for _ in shape))


def _post_l0(a_lat, a_ctx, f_lat, f_ctx, x, ctx, mod, norm_g, w_out_bf, router, n_lat, n_tok, tiles_per_batch):
    n_lat_tiles = n_lat // TM

    def lat(wd):
        return pl.BlockSpec((TM, wd), lambda i: (jnp.minimum(i, n_lat_tiles - 1), 0))

    def cx(wd):
        return pl.BlockSpec((TM, wd), lambda i: (jnp.maximum(i - n_lat_tiles, 0), 0))

    out_specs, out_shapes = _post_out(n_tok)
    return pl.pallas_call(
        functools.partial(_post_l0_kernel, n_lat_tiles=n_lat_tiles),
        grid=(n_tok // TM,),
        in_specs=[lat(NA_WIDTH), cx(NA_WIDTH), lat(FOURIER_WIDTH), cx(FOURIER_WIDTH),
                  lat(D_MODEL), cx(D_MODEL),
                  _mod_spec(2, tiles_per_batch), _mod_spec(3, tiles_per_batch), _mod_spec(4, tiles_per_batch),
                  _const_spec((1, D_MODEL)), _const_spec((D_MODEL, D_MODEL)),
                  _const_spec((D_MODEL, N_EXPERTS)), _const_spec((D_MODEL, N_EXPERTS)),
                  _const_spec((1, N_EXPERTS))],
        out_specs=out_specs, out_shape=out_shapes,
        scratch_shapes=[pltpu.VMEM((TM, TM), BF16)],
        compiler_params=_cparams(("arbitrary",)),
        name="post_mixer_l0",
    )(a_lat, a_ctx, f_lat, f_ctx, x, ctx, mod, mod, mod, norm_g.reshape(1, D_MODEL), w_out_bf, *router)


def _post_l1(gyt, x_all, mod, norm_g, w_glu_t_bf, b_glu, w_out_t_bf, router, n_lat, tiles_per_batch):
    out_specs, out_shapes = _post_out(n_lat)
    return pl.pallas_call(
        _post_l1_kernel,
        grid=(n_lat // TM,),
        in_specs=[pl.BlockSpec((D_MODEL, TM), lambda i: (0, i)),
                  pl.BlockSpec((TM, D_MODEL), lambda i: (i, 0)),
                  _mod_spec(2, tiles_per_batch), _mod_spec(3, tiles_per_batch), _mod_spec(4, tiles_per_batch),
                  _const_spec((1, D_MODEL)), _const_spec((D_MODEL, D_MODEL)), _const_spec((D_MODEL, 1)),
                  _const_spec((D_MODEL, D_MODEL)),
                  _const_spec((D_MODEL, N_EXPERTS)), _const_spec((D_MODEL, N_EXPERTS)),
                  _const_spec((1, N_EXPERTS))],
        out_specs=out_specs, out_shape=out_shapes,
        scratch_shapes=[pltpu.VMEM((TM, TM), BF16)],
        compiler_params=_cparams(("arbitrary",)),
        name="post_mixer_l1",
    )(gyt, x_all, mod, mod, mod, norm_g.reshape(1, D_MODEL), w_glu_t_bf, b_glu.reshape(D_MODEL, 1),
      w_out_t_bf, *router)


CHUNK = 8
TILE_ROWS = TM * TOP_K + N_EXPERTS * CHUNK
TILE_CHUNKS = TILE_ROWS // CHUNK
BLOCK_CHUNKS = EXPERT_ROWS // CHUNK
SORT_ROWS = 256
ISSUE_UNROLL = 8
FF_CHUNK = 256
HALF = D_MODEL // 2


def _pack_bf16_pairs(y):
    bits = pltpu.bitcast(y, jnp.uint32)
    return (bits[:, :HALF] & jnp.uint32(0xFFFF0000)) | (bits[:, HALF:] >> 16)


def _unpack_bf16_pairs(w):
    hi = pltpu.bitcast(w & jnp.uint32(0xFFFF0000), F32).astype(BF16)
    lo = pltpu.bitcast(w << 16, F32).astype(BF16)
    return hi, lo


def _sort_tile(h2, mf, g, hs_ref, pos_ref, gate_ref, cnt_ref, tri_ref):
    @pl.when(pl.program_id(0) == 0)
    def _():
        r = lax.broadcasted_iota(jnp.int32, (TM, TM), 0)
        c = lax.broadcasted_iota(jnp.int32, (TM, TM), 1)
        tri_ref[...] = (r > c).astype(F32).astype(BF16)

    m = mf.astype(BF16)
    rank = _dot(tri_ref[...], m)
    cnt = mf.sum(axis=0, keepdims=True)
    n8 = jnp.floor((cnt + (CHUNK - 1)) * (1.0 / CHUNK))
    er = lax.broadcasted_iota(jnp.int32, (N_EXPERTS, N_EXPERTS), 0)
    ec = lax.broadcasted_iota(jnp.int32, (N_EXPERTS, N_EXPERTS), 1)
    upper = (er < ec).astype(F32).astype(BF16)
    off = CHUNK * _dot(jnp.broadcast_to(n8, (8, N_EXPERTS)).astype(BF16), upper)[0:1]
    pos = rank + off
    slot = _dot(m, upper)
    chosen = mf > 0.5
    lane = lax.broadcasted_iota(jnp.int32, pos_ref.shape, 1)
    p_out = jnp.zeros(pos_ref.shape, F32)
    g_out = jnp.zeros(gate_ref.shape, F32)
    for k in range(TOP_K):
        sel = chosen & (slot == float(k))
        pk = jnp.where(sel, pos, 0.0).sum(axis=-1, keepdims=True)
        gk = jnp.where(sel, g, 0.0).sum(axis=-1, keepdims=True)
        p_out = jnp.where(lane == k, pk, p_out)
        g_out = jnp.where(lane == k, gk, g_out)
    pos_ref[...] = p_out
    gate_ref[...] = g_out
    cnt_ref[...] = jnp.zeros_like(cnt_ref)
    cnt_ref[0:1, 0:N_EXPERTS] = cnt

    pos_t = p_out.T
    hb = h2.astype(BF16)
    rid = lax.broadcasted_iota(jnp.int32, (SORT_ROWS, TM), 0).astype(F32).astype(BF16)
    one, nil = jnp.ones((), BF16), jnp.zeros((), BF16)
    for rc in range(TILE_ROWS // SORT_ROWS):
        rel = (pos_t[0:16, :] - float(rc * SORT_ROWS)).astype(BF16)
        onehot = jnp.where(rid == rel[0:1, :], one, nil)
        for k in range(1, TOP_K):
            onehot = jnp.where(rid == rel[k:k + 1, :], one, onehot)
        hs_ref[rc * SORT_ROWS:(rc + 1) * SORT_ROWS, :] = _pack_bf16_pairs(_dot(onehot, hb))


def _chunk_tables(cnt, n_blocks):
    n_tiles = cnt.shape[0]
    n8 = (cnt + CHUNK - 1) // CHUNK
    lend = jnp.cumsum(n8, axis=1)
    lstart = lend - n8
    cum = jnp.cumsum(n8, axis=0)
    before = cum - n8
    total = cum[-1]
    nblk = (total + BLOCK_CHUNKS - 1) // BLOCK_CHUNKS
    bend = jnp.cumsum(nblk)
    gstart = (bend - nblk) * BLOCK_CHUNKS
    n_used = jnp.maximum(bend[-1], 1)
    blk = jnp.arange(n_blocks, dtype=jnp.int32)
    block_expert = (bend[None, :] <= jnp.minimum(blk, n_used - 1)[:, None]).sum(axis=1).astype(jnp.int32)
    block_expert = jnp.minimum(block_expert, N_EXPERTS - 1)
    experts = jnp.arange(N_EXPERTS, dtype=jnp.int32)
    lc = jnp.arange(TILE_CHUNKS, dtype=jnp.int32)
    e_of = (lend[:, None, :] <= lc[None, :, None]).sum(axis=-1).astype(jnp.int32)
    shift = gstart[None, :] + before - lstart
    pick = e_of[:, :, None] == experts[None, None, :]
    to_sorted = jnp.where(pick, shift[:, None, :], 0).sum(axis=-1) + jnp.where(e_of < N_EXPERTS, lc[None, :], 0)
    of_block = block_expert[:, None] == experts[None, :]
    gstart_b = jnp.where(of_block, gstart[None, :], 0).sum(axis=-1)
    total_b = jnp.where(of_block, total[None, :], 0).sum(axis=-1)

    def per_block(table):
        return jnp.where(of_block[:, :, None], table.T[None, :, :], 0).sum(axis=1)

    cum_b, lstart_b, before_b = per_block(cum), per_block(lstart), per_block(before)
    rel = (blk[:, None] * BLOCK_CHUNKS + jnp.arange(BLOCK_CHUNKS, dtype=jnp.int32)[None, :]) - gstart_b[:, None]
    valid = (blk[:, None] < n_used) & (rel < total_b[:, None])
    tile = jnp.minimum((cum_b[:, None, :] <= rel[:, :, None]).sum(axis=-1), n_tiles - 1).astype(jnp.int32)
    of_tile = tile[:, :, None] == jnp.arange(n_tiles, dtype=jnp.int32)[None, None, :]
    inside = jnp.where(of_tile, (lstart_b - before_b)[:, None, :], 0).sum(axis=-1)
    from_tiles = jnp.where(valid, tile * TILE_CHUNKS + inside + rel, TILE_CHUNKS - 1)
    return (block_expert, n_used.reshape(1).astype(jnp.int32), from_tiles.reshape(-1).astype(jnp.int32),
            to_sorted.reshape(-1).astype(jnp.int32))


def _chunk_rows(chunk):
    start = chunk * CHUNK
    return pl.ds(start if isinstance(start, int) else pl.multiple_of(start, CHUNK), CHUNK)


def _chunk_copy(src_ref, src_chunk, dst_ref, dst_chunk, sem):
    return pltpu.make_async_copy(src_ref.at[_chunk_rows(src_chunk), :], dst_ref.at[_chunk_rows(dst_chunk), :], sem)


def _expert_kernel(be_ref, nu_ref, src_ref, hs_ref, wgu_ref, bgu_ref, wd_ref, bd_ref, o_ref,
                   wgu_bf, wd_bf, x_buf, sems):
    i = pl.program_id(0)
    n_used = nu_ref[0]

    def gather(block, slot, start):
        if not start:
            pltpu.make_async_copy(hs_ref.at[pl.ds(0, EXPERT_ROWS), :], x_buf.at[slot], sems.at[slot]).wait()
            return
        for cidx in range(BLOCK_CHUNKS):
            _chunk_copy(hs_ref, src_ref[block * BLOCK_CHUNKS + cidx], x_buf.at[slot], cidx, sems.at[slot]).start()

    @pl.when(i == 0)
    def _():
        gather(0, 0, True)

    for slot in range(2):
        @pl.when((i + 1 < n_used) & ((i + 1) % 2 == slot))
        def _(slot=slot):
            gather(i + 1, slot, True)

    @pl.when((i == 0) | (be_ref[i] != be_ref[jnp.maximum(i - 1, 0)]))
    def _():
        wgu_bf[...] = wgu_ref[0].astype(BF16)
        wd_bf[...] = wd_ref[0].astype(BF16)

    for slot in range(2):
        @pl.when((i < n_used) & (i % 2 == slot))
        def _(slot=slot):
            gather(i, slot, False)
            x_hi, x_lo = _unpack_bf16_pairs(x_buf[slot])
            y = None
            for j in range(D_MODEL // FF_CHUNK):
                def proj(col0):
                    cols = slice(col0 + j * FF_CHUNK, col0 + (j + 1) * FF_CHUNK)
                    return (_dot(x_hi, wgu_bf[:HALF, cols]) + _dot(x_lo, wgu_bf[HALF:, cols])
                            + bgu_ref[0, :, cols])
                gate = jnp.minimum(proj(0), SWIGLU_LIMIT)
                up = jnp.clip(proj(D_MODEL), -SWIGLU_LIMIT, SWIGLU_LIMIT)
                act = (up + 1.0) * (gate * jax.nn.sigmoid(SWIGLU_ALPHA * gate))
                part = _dot(act.astype(BF16), wd_bf[j * FF_CHUNK:(j + 1) * FF_CHUNK, :])
                y = part if y is None else y + part
            y = y + bd_ref[0]
            o_ref[...] = _pack_bf16_pairs(y.astype(BF16).astype(F32))

    @pl.when(i >= n_used)
    def _():
        o_ref[...] = jnp.zeros_like(o_ref)


def _experts(hs, block_expert, n_used, from_tiles, wgu, bgu, wd, bd):
    n_blocks = block_expert.shape[0]

    def row_map(i, be, nu, src):
        return (i, 0)

    def exp_map(i, be, nu, src):
        return (be[i], 0, 0)

    grid_spec = pltpu.PrefetchScalarGridSpec(
        num_scalar_prefetch=3,
        grid=(n_blocks,),
        in_specs=[pl.BlockSpec(memory_space=pl.ANY),
                  pl.BlockSpec((1, D_MODEL, 2 * D_MODEL), exp_map),
                  pl.BlockSpec((1, 1, 2 * D_MODEL), exp_map),
                  pl.BlockSpec((1, D_MODEL, D_MODEL), exp_map),
                  pl.BlockSpec((1, 1, D_MODEL), exp_map)],
        out_specs=pl.BlockSpec((EXPERT_ROWS, HALF), row_map),
        scratch_shapes=[pltpu.VMEM((D_MODEL, 2 * D_MODEL), BF16), pltpu.VMEM((D_MODEL, D_MODEL), BF16),
                        pltpu.VMEM((2, EXPERT_ROWS, HALF), jnp.uint32), pltpu.SemaphoreType.DMA((2,))],
    )
    return pl.pallas_call(
        _expert_kernel,
        grid_spec=grid_spec,
        out_shape=jax.ShapeDtypeStruct((n_blocks * EXPERT_ROWS, HALF), jnp.uint32),
        compiler_params=pltpu.CompilerParams(dimension_semantics=("arbitrary",),
                                             vmem_limit_bytes=EXPERT_VMEM_LIMIT),
        name="moe_experts",
    )(block_expert, n_used, from_tiles, hs, wgu, bgu.reshape(N_EXPERTS, 1, 2 * D_MODEL), wd,
      bd.reshape(N_EXPERTS, 1, D_MODEL))


def _combine_kernel(dst_ref, ys_ref, pos_ref, gate_ref, x_ref, g2_ref, ng_ref, o_ref, y_buf, sems, *, final_norm):
    i = pl.program_id(0)
    n_tiles = pl.num_programs(0)

    def gather(tile, slot, start):
        if not start:
            pltpu.make_async_copy(ys_ref.at[pl.ds(0, TILE_ROWS), :], y_buf.at[slot], sems.at[slot]).wait()
            return

        def body(g, carry):
            for j in range(ISSUE_UNROLL):
                cidx = g * ISSUE_UNROLL + j
                _chunk_copy(ys_ref, dst_ref[tile * TILE_CHUNKS + cidx], y_buf.at[slot], cidx, sems.at[slot]).start()
            return carry
        lax.fori_loop(0, TILE_CHUNKS // ISSUE_UNROLL, body, 0)

    @pl.when(i == 0)
    def _():
        gather(0, 0, True)

    for slot in range(2):
        @pl.when((i + 1 < n_tiles) & ((i + 1) % 2 == slot))
        def _(slot=slot):
            gather(i + 1, slot, True)

    pos = pos_ref[...]
    gate = gate_ref[...]
    for slot in range(2):
        @pl.when(i % 2 == slot)
        def _(slot=slot):
            gather(i, slot, False)
            acc_hi = jnp.zeros((TM, HALF), F32)
            acc_lo = jnp.zeros((TM, HALF), F32)
            rid = lax.broadcasted_iota(jnp.int32, (TM, SORT_ROWS), 1).astype(F32).astype(BF16)
            gate_b = gate.astype(BF16)
            for rc in range(TILE_ROWS // SORT_ROWS):
                rel = (pos - float(rc * SORT_ROWS)).astype(BF16)
                w = jnp.where(rid == rel[:, 0:1], gate_b[:, 0:1], jnp.zeros((), BF16))
                for k in range(1, TOP_K):
                    w = jnp.where(rid == rel[:, k:k + 1], gate_b[:, k:k + 1], w)
                y_hi, y_lo = _unpack_bf16_pairs(y_buf[slot, rc * SORT_ROWS:(rc + 1) * SORT_ROWS, :])
                acc_hi = acc_hi + _dot(w, y_hi)
                acc_lo = acc_lo + _dot(w, y_lo)
            x = x_ref[...] + g2_ref[0] * jnp.concatenate([acc_hi, acc_lo], axis=-1)
            if final_norm:
                ms = jnp.mean(x * x, axis=-1, keepdims=True)
                x = x * lax.rsqrt(ms + EPS) * ng_ref[...]
            o_ref[...] = x


def _combine(ys, to_sorted, pos4, gate4, x_res, mod, norm_g, tiles_per_batch, final_norm):
    n_tok = x_res.shape[0]
    tok = pl.BlockSpec((TM, D_MODEL), lambda i, dst: (i, 0))
    wide = pl.BlockSpec((TM, 128), lambda i, dst: (i, 0))
    tiles, bsz = tiles_per_batch
    grid_spec = pltpu.PrefetchScalarGridSpec(
        num_scalar_prefetch=1,
        grid=(n_tok // TM,),
        in_specs=[pl.BlockSpec(memory_space=pl.ANY), wide, wide, tok,
                  pl.BlockSpec((1, 1, D_MODEL), lambda i, dst: (jnp.minimum(i // tiles, bsz) * 6 + 5, 0, 0)),
                  pl.BlockSpec((1, D_MODEL), lambda i, dst: (0, 0))],
        out_specs=tok,
        scratch_shapes=[pltpu.VMEM((2, TILE_ROWS, HALF), jnp.uint32), pltpu.SemaphoreType.DMA((2,))],
    )
    return pl.pallas_call(
        functools.partial(_combine_kernel, final_norm=final_norm),
        grid_spec=grid_spec,
        out_shape=jax.ShapeDtypeStruct((n_tok, D_MODEL), F32),
        compiler_params=_cparams(("arbitrary",)),
        name="moe_combine",
    )(to_sorted, ys, pos4, gate4, x_res, mod, norm_g.reshape(1, D_MODEL))


def _moe(hs, pos4, gate4, cnt, x_res, mod, moe_w, tiles_per_batch, norm_g, final_norm):
    wgu, bgu, wd, bd = moe_w
    n_tiles = x_res.shape[0] // TM
    n_blocks = -(-n_tiles * TILE_CHUNKS // BLOCK_CHUNKS) + N_EXPERTS
    cnt_i = cnt.reshape(n_tiles, 8, 128)[:, 0, :N_EXPERTS].astype(jnp.int32)
    block_expert, n_used, from_tiles, to_sorted = _chunk_tables(cnt_i, n_blocks)
    ys = _experts(hs, block_expert, n_used, from_tiles, wgu, bgu, wd, bd)
    return _combine(ys, to_sorted, pos4, gate4, x_res, mod, norm_g, tiles_per_batch, final_norm)


def _s5_tables(lam_re, lam_im, log_dt, b_re, b_im, c_re, c_im, d_skip):
    q = S5_Q
    lam = lax.complex(jnp.minimum(lam_re.astype(F32), LAMBDA_RE_MAX), lam_im.astype(F32))
    dt = jnp.exp(log_dt.astype(F32))[..., None]
    a = jnp.exp(lam * dt)
    bbar = ((a - 1) / lam)[..., None] * lax.complex(b_re.astype(F32), b_im.astype(F32))
    cc = lax.complex(c_re.astype(F32), c_im.astype(F32))
    steps = jnp.arange(q + 1, dtype=F32)
    pw = jnp.exp((lam * dt)[:, :, None, :] * steps[None, None, :, None])
    kern = jnp.einsum('dgop,dgkp,dgpc->dgcok', cc, pw[:, :, :q], bbar).real
    eye = jnp.eye(SSM_GROUP, dtype=F32)
    k_fwd = kern[0].at[..., 0].add(kern[1][..., 0] + d_skip.astype(F32)[:, :, None] * eye[None])
    k_bwd = jnp.concatenate([jnp.zeros_like(kern[1][..., :1]), kern[1][..., :0:-1]], axis=-1)
    k_fwd = k_fwd.reshape(SSM_GROUPS, SSM_GROUP * SSM_GROUP, q)
    k_bwd = k_bwd.reshape(SSM_GROUPS, SSM_GROUP * SSM_GROUP, q)
    def both(f, b):
        return jnp.concatenate([f, b], axis=-1)

    e_in = both(pw[0][:, ::-1][:, 1:], pw[1][:, :q])
    b_in = both(jnp.transpose(bbar[0], (0, 2, 1)), jnp.transpose(bbar[1], (0, 2, 1)))
    e_out = jnp.transpose(both(pw[0][:, 1:], pw[1][:, ::-1][:, :q]), (0, 2, 1))
    c_out = jnp.transpose(both(cc[0], cc[1]), (0, 2, 1))
    a_q = both(pw[0][:, q], pw[1][:, q]).reshape(SSM_GROUPS, 1, 2 * SSM_STATE)
    parts = [t for z in (e_in, b_in, e_out, c_out, a_q) for t in (z.real, z.imag)]
    return [k_fwd, k_bwd] + parts


def _s5_kernel(u_ref, kf_ref, kb_ref, er_ref, ei_ref, br_ref, bi_ref, pr_ref, pi_ref, cr_ref, ci_ref,
               ar_ref, ai_ref, o_ref, w_ref, s_ref, h_ref, *, bsz, n_lat_chunks, n_ctx_chunks):
    q = S5_Q
    half = 2 * SSM_STATE
    n_lat = bsz * n_lat_chunks

    causal = (lax.broadcasted_iota(jnp.int32, (q, q), 1) >= lax.broadcasted_iota(jnp.int32, (q, q), 0))

    def build(c, carry):
        for o in range(SSM_GROUP):
            row = pl.ds(c * SSM_GROUP + o, 1)
            kf = pltpu.roll(jnp.broadcast_to(kf_ref[0, row, :], (q, q)), 0, 1, stride=1, stride_axis=0)
            kb = pltpu.roll(jnp.broadcast_to(kb_ref[0, row, :], (q, q)), 0, 1, stride=1, stride_axis=0)
            w_ref[pl.ds(pl.multiple_of(c * q, q), q), o * q:(o + 1) * q] = jnp.where(causal, kf, kb).astype(BF16)
        return carry

    lax.fori_loop(0, SSM_GROUP, build, 0)

    u = u_ref[...]
    er, ei = er_ref[0], ei_ref[0]
    s_re = None
    for c in range(SSM_GROUP):
        br, bi = br_ref[0, c:c + 1, :], bi_ref[0, c:c + 1, :]
        w_re = (er * br - ei * bi).astype(BF16)
        w_im = (er * bi + ei * br).astype(BF16)
        s_re = _dot(u[c], w_re) if s_re is None else s_re + _dot(u[c], w_re)
        s_im = _dot(u[c], w_im) if c == 0 else s_im + _dot(u[c], w_im)
    s_ref[0] = s_re
    s_ref[1] = s_im

    fwd = lax.broadcasted_iota(jnp.int32, (bsz, half), 1) < SSM_STATE
    ar = jnp.broadcast_to(ar_ref[0], (bsz, half))
    ai = jnp.broadcast_to(ai_ref[0], (bsz, half))

    def chunk_rows(base, j, per_sample):
        return pl.ds(base + j, bsz, stride=per_sample)

    def load(base, j, n):
        rows_f = chunk_rows(base, j, n)
        rows_b = chunk_rows(base, n - 1 - j, n)
        return (jnp.where(fwd, s_ref[0, rows_f, :], s_ref[0, rows_b, :]),
                jnp.where(fwd, s_ref[1, rows_f, :], s_ref[1, rows_b, :]))

    def step(hr, hi, sr, si):
        return ar * hr - ai * hi + sr, ar * hi + ai * hr + si

    def ctx_body(j, carry):
        return step(*carry, *load(n_lat, j, n_ctx_chunks))

    def lat_body(j, carry):
        hr, hi = carry
        rows_f = chunk_rows(0, j, n_lat_chunks)
        rows_b = chunk_rows(0, n_lat_chunks - 1 - j, n_lat_chunks)
        h_ref[0, rows_f, :] = jnp.where(fwd, hr, 0.0)
        h_ref[1, rows_f, :] = jnp.where(fwd, hi, 0.0)
        h_ref[2, rows_b, :] = jnp.where(fwd, 0.0, hr)
        h_ref[3, rows_b, :] = jnp.where(fwd, 0.0, hi)
        return step(hr, hi, *load(0, j, n_lat_chunks))

    zero = jnp.zeros((bsz, half), F32)
    carry = lax.fori_loop(0, n_ctx_chunks, ctx_body, (zero, zero))
    lax.fori_loop(0, n_lat_chunks, lat_body, carry)

    ucat = jnp.concatenate([u[c][:n_lat] for c in range(SSM_GROUP)], axis=-1)
    y = _dot(ucat, w_ref[...])
    h_re = (h_ref[0] + h_ref[2]).astype(BF16)
    h_im = (h_ref[1] + h_ref[3]).astype(BF16)
    pr, pi = pr_ref[0], pi_ref[0]
    cr_all, ci_all = cr_ref[0], ci_ref[0]
    for o in range(SSM_GROUP):
        cr, ci = cr_all[:, o:o + 1], ci_all[:, o:o + 1]
        w_re = (pr * cr - pi * ci).astype(BF16)
        w_im = (pr * ci + pi * cr).astype(BF16)
        y_o = y[:, o * q:(o + 1) * q] + _dot(h_re, w_re) - _dot(h_im, w_im)
        o_ref[o] = jax.nn.gelu(y_o).astype(o_ref.dtype)


def _s5(u3, tables, bsz, n_lat_chunks, n_ctx_chunks):
    q = S5_Q
    n_chunks = u3.shape[1]
    n_lat = bsz * n_lat_chunks
    assert n_chunks == n_lat + bsz * n_ctx_chunks

    def gspec(*shape):
        return pl.BlockSpec((1,) + shape, lambda g: (g,) + tuple(0 for _ in shape))

    return pl.pallas_call(
        functools.partial(_s5_kernel, bsz=bsz, n_lat_chunks=n_lat_chunks, n_ctx_chunks=n_ctx_chunks),
        grid=(SSM_GROUPS,),
        in_specs=[pl.BlockSpec((SSM_GROUP, n_chunks, q), lambda g: (g, 0, 0)),
                  gspec(SSM_GROUP * SSM_GROUP, q), gspec(SSM_GROUP * SSM_GROUP, q),
                  gspec(q, 2 * SSM_STATE), gspec(q, 2 * SSM_STATE),
                  gspec(SSM_GROUP, 2 * SSM_STATE), gspec(SSM_GROUP, 2 * SSM_STATE),
                  gspec(2 * SSM_STATE, q), gspec(2 * SSM_STATE, q),
                  gspec(2 * SSM_STATE, SSM_GROUP), gspec(2 * SSM_STATE, SSM_GROUP),
                  gspec(1, 2 * SSM_STATE), gspec(1, 2 * SSM_STATE)],
        out_specs=pl.BlockSpec((SSM_GROUP, n_lat, q), lambda g: (g, 0, 0)),
        out_shape=jax.ShapeDtypeStruct((SSM_GROUPS * SSM_GROUP, n_lat, q), BF16),
        scratch_shapes=[pltpu.VMEM((SSM_GROUP * q, SSM_GROUP * q), BF16),
                        pltpu.VMEM((2, n_chunks, 2 * SSM_STATE), F32),
                        pltpu.VMEM((4, n_lat, 2 * SSM_STATE), F32)],
        compiler_params=_cparams(("arbitrary",)),
        name="s5_scan",
    )(u3, *tables)


def _moe_weights(w_gate_up, b_gate_up, w_down, b_down):
    return w_gate_up, b_gate_up, w_down, b_down


def kernel(x, c, ctx, c_ctx, l0_ada_w, l0_ada_b, l0_norm_mix, l0_w_in, l0_rpb, l0_w_fourier, l0_w_out, l0_norm_ffn, l0_router_w, l0_router_b, l0_w_gate_up, l0_b_gate_up, l0_w_down, l0_b_down, l1_ada_w, l1_ada_b, l1_norm_mix, l1_w_in, l1_lambda_re, l1_lambda_im, l1_log_dt, l1_b_re, l1_b_im, l1_c_re, l1_c_im, l1_d_skip, l1_w_glu, l1_b_glu, l1_w_out, l1_norm_ffn, l1_router_w, l1_router_b, l1_w_gate_up, l1_b_gate_up, l1_w_down, l1_b_down, final_norm):
    bsz, seq, _ = x.shape
    ctx_len = ctx.shape[1]
    n_lat = bsz * seq
    n_ctx = bsz * ctx_len
    n_tok = n_lat + n_ctx
    assert seq % TM == 0 and n_ctx % TM == 0
    x2 = x.reshape(n_lat, D_MODEL)
    ctx2 = ctx.reshape(n_ctx, D_MODEL)

    n_mod = bsz + 1
    c_pad = jnp.zeros((-(-n_mod // 8) * 8, D_MODEL), F32).at[:bsz].set(c).at[bsz].set(c_ctx)
    mod0 = _ada(c_pad, l0_ada_w, l0_ada_b)[:n_mod].reshape(n_mod * 6, 1, D_MODEL)
    mod1 = _ada(c_pad, l1_ada_w, l1_ada_b)[:n_mod].reshape(n_mod * 6, 1, D_MODEL)
    tpb = (seq // TM, bsz)

    q, k, v, f = _modproj([x2, ctx2], mod0, l0_norm_mix, l0_w_in.astype(BF16),
                          (NA_WIDTH, NA_WIDTH, NA_WIDTH, FOURIER_WIDTH), n_lat, n_tok, tpb,
                          q_scale=HEAD_DIM ** -0.5)
    a_lat, a_ctx = _attention(q, k, v, l0_rpb, bsz, seq, ctx_len)
    bdw = (jnp.eye(FOURIER_GROUPS, dtype=F32)[:, None, :, None] * l0_w_fourier[:, :, None, :]).reshape(
        FOURIER_WIDTH, FOURIER_WIDTH).astype(BF16)
    f_lat = _fourier(f, bdw, bsz, seq, 0, TM)
    f_ctx = _fourier(f, bdw, bsz, ctx_len, n_lat // ctx_len, ctx_len)
    x_all, *routed = _post_l0(a_lat, a_ctx, f_lat, f_ctx, x2, ctx2, mod0, l0_norm_ffn,
                              l0_w_out.astype(BF16), _router_args(l0_router_w, l0_router_b), n_lat, n_tok, tpb)
    x_all = _moe(*routed, x_all, mod0, _moe_weights(l0_w_gate_up, l0_b_gate_up, l0_w_down, l0_b_down),
                 tpb, final_norm, False)

    assert seq % S5_Q == 0 and ctx_len % S5_Q == 0
    ut = _modproj_t(x_all, mod1, l1_norm_mix, l1_w_in.T.astype(BF16), n_tok, tpb)
    tables = _s5_tables(l1_lambda_re, l1_lambda_im, l1_log_dt, l1_b_re, l1_b_im, l1_c_re, l1_c_im, l1_d_skip)
    yt = _s5(ut.reshape(D_MODEL, n_tok // S5_Q, S5_Q), tables, bsz, seq // S5_Q, ctx_len // S5_Q)
    gy = yt.reshape(D_MODEL, n_lat)
    x1, *routed = _post_l1(gy, x_all, mod1, l1_norm_ffn, l1_w_glu.T.astype(BF16), l1_b_glu,
                           l1_w_out.T.astype(BF16), _router_args(l1_router_w, l1_router_b), n_lat, tpb)
    out = _moe(*routed, x1, mod1, _moe_weights(l1_w_gate_up, l1_b_gate_up, l1_w_down, l1_b_down),
               tpb, final_norm, True)
    return out.reshape(bsz, seq, D_MODEL)
```

```python
import functools
import math

import numpy as np
import jax
import jax.numpy as jnp
from jax import lax
from jax.experimental import pallas as pl
from jax.experimental.pallas import tpu as pltpu

F32 = jnp.float32
BF16 = jnp.bfloat16

D_MODEL = 1024
GRID_W = 64
HEAD_DIM = 64
NA_WIDTH = 512
NA_HEADS = 8
MAX_KR = 8
KC = 16
FOURIER_DIM = 64
FOURIER_GROUPS = 8
FOURIER_WIDTH = 512
SSM_GROUP = 16
SSM_GROUPS = 64
SSM_STATE = 64
LAMBDA_RE_MAX = -1e-4
N_EXPERTS = 32
TOP_K = 4
SWIGLU_ALPHA = 1.702
SWIGLU_LIMIT = 7.0
EPS = 1e-6

TM = 512
EXPERT_ROWS = 512
S5_Q = 128
VMEM_LIMIT = 48 * 1024 * 1024
EXPERT_VMEM_LIMIT = 58 * 1024 * 1024


def _cparams(sem):
    return pltpu.CompilerParams(dimension_semantics=sem, vmem_limit_bytes=VMEM_LIMIT)


def _split_bf16(a):
    hi = a.astype(BF16)
    lo = (a - hi.astype(F32)).astype(BF16)
    return hi, lo


def _dot(a, b):
    return jnp.dot(a, b, preferred_element_type=F32)


def _dot_nt(a, b):
    return lax.dot_general(a, b, (((1,), (1,)), ((), ())), preferred_element_type=F32)


def _ada_kernel(c_ref, w_ref, b_ref, o_ref):
    c = c_ref[...]
    s = c * jax.nn.sigmoid(c)
    s_hi, s_lo = _split_bf16(s)
    w_hi, w_lo = _split_bf16(w_ref[...])
    o_ref[...] = _dot(s_hi, w_hi) + _dot(s_lo, w_hi) + _dot(s_hi, w_lo) + b_ref[...]


def _ada(c_pad, ada_w, ada_b):
    n = ada_w.shape[1]
    tn = 1024
    return pl.pallas_call(
        _ada_kernel,
        grid=(n // tn,),
        in_specs=[pl.BlockSpec((c_pad.shape[0], D_MODEL), lambda j: (0, 0)),
                  pl.BlockSpec((D_MODEL, tn), lambda j: (0, j)),
                  pl.BlockSpec((1, tn), lambda j: (0, j))],
        out_specs=pl.BlockSpec((c_pad.shape[0], tn), lambda j: (0, j)),
        out_shape=jax.ShapeDtypeStruct((c_pad.shape[0], n), F32),
        compiler_params=_cparams(("arbitrary",)),
        name="ada_mod",
    )(c_pad, ada_w, ada_b.reshape(1, n))


def _mod_spec(which, tiles_per_batch):
    tiles, bsz = tiles_per_batch
    return pl.BlockSpec((1, 1, D_MODEL), lambda i: (jnp.minimum(i // tiles, bsz) * 6 + which, 0, 0))


def _modulate(x, g, sh, sc):
    ms = jnp.mean(x * x, axis=-1, keepdims=True)
    y = x * lax.rsqrt(ms + EPS) * g
    return y * (1.0 + sc) + sh


def _modproj_kernel(*refs, n_lat_tiles, two_src, widths, q_scale):
    if two_src:
        x_ref, c_ref, sh_ref, sc_ref, g_ref, w_ref = refs[:6]
        o_refs = refs[6:]
        i = pl.program_id(0)
        x = jnp.where(i < n_lat_tiles, x_ref[...], c_ref[...])
    else:
        x_ref, sh_ref, sc_ref, g_ref, w_ref = refs[:5]
        o_refs = refs[5:]
        x = x_ref[...]
    h = _modulate(x, g_ref[...], sh_ref[0], sc_ref[0]).astype(BF16)
    off = 0
    for j, (o_ref, wd) in enumerate(zip(o_refs, widths)):
        y = _dot(h, w_ref[:, off:off + wd])
        if j == 0 and q_scale != 1.0:
            y = y * q_scale
        o_ref[...] = y.astype(o_ref.dtype)
        off += wd


def _modproj(x_srcs, mod, norm_g, w_bf, widths, n_lat, n_tok, tiles_per_batch, q_scale=1.0):
    n_tiles = n_tok // TM
    n_lat_tiles = n_lat // TM
    two_src = len(x_srcs) == 2
    n_out = w_bf.shape[1]
    if two_src:
        x_specs = [pl.BlockSpec((TM, D_MODEL), lambda i: (jnp.minimum(i, n_lat_tiles - 1), 0)),
                   pl.BlockSpec((TM, D_MODEL), lambda i: (jnp.maximum(i - n_lat_tiles, 0), 0))]
    else:
        x_specs = [pl.BlockSpec((TM, D_MODEL), lambda i: (i, 0))]
    return pl.pallas_call(
        functools.partial(_modproj_kernel, n_lat_tiles=n_lat_tiles, two_src=two_src,
                          widths=tuple(widths), q_scale=q_scale),
        grid=(n_tiles,),
        in_specs=x_specs + [_mod_spec(0, tiles_per_batch), _mod_spec(1, tiles_per_batch),
                            pl.BlockSpec((1, D_MODEL), lambda i: (0, 0)),
                            pl.BlockSpec((D_MODEL, n_out), lambda i: (0, 0))],
        out_specs=[pl.BlockSpec((TM, wd), lambda i: (i, 0)) for wd in widths],
        out_shape=[jax.ShapeDtypeStruct((n_tok, wd), BF16) for wd in widths],
        compiler_params=_cparams(("arbitrary",)),
        name="modulate_in_proj",
    )(*x_srcs, mod, mod, norm_g.reshape(1, D_MODEL), w_bf)


def _modproj_t_kernel(x_ref, sh_ref, sc_ref, g_ref, wt_ref, o_ref):
    h = _modulate(x_ref[...], g_ref[...], sh_ref[0], sc_ref[0]).astype(BF16)
    o_ref[...] = _dot_nt(wt_ref[...], h).astype(o_ref.dtype)


def _modproj_t(x_all, mod, norm_g, wt_bf, n_tok, tiles_per_batch):
    n_out = wt_bf.shape[0]
    return pl.pallas_call(
        _modproj_t_kernel,
        grid=(n_tok // TM,),
        in_specs=[pl.BlockSpec((TM, D_MODEL), lambda i: (i, 0)),
                  _mod_spec(0, tiles_per_batch), _mod_spec(1, tiles_per_batch),
                  pl.BlockSpec((1, D_MODEL), lambda i: (0, 0)),
                  pl.BlockSpec((n_out, D_MODEL), lambda i: (0, 0))],
        out_specs=pl.BlockSpec((n_out, TM), lambda i: (0, i)),
        out_shape=jax.ShapeDtypeStruct((n_out, n_tok), BF16),
        compiler_params=_cparams(("arbitrary",)),
        name="modulate_in_proj_t",
    )(x_all, mod, mod, norm_g.reshape(1, D_MODEL), wt_bf)


Q_ROWS = 4
BAND_ROWS = Q_ROWS + MAX_KR
EDGE_BLOCKS = -(-(MAX_KR // 2) // Q_ROWS)
N_BIAS_CLASSES = 2 * EDGE_BLOCKS + 1


def _na_bias_table(rpb, rows):
    qc = np.arange(GRID_W)[:, None]
    kc = np.arange(GRID_W)[None, :]
    win0 = np.clip(qc - KC // 2, 0, GRID_W - KC)
    col_valid = (kc >= win0) & (kc < win0 + KC)
    col_off = np.clip(kc - qc, -(KC - 1), KC - 1) + KC - 1
    cb = rpb.astype(F32)[:, :, col_off]
    cb = jnp.where(jnp.asarray(col_valid)[None, None], cb, -jnp.inf)
    n_blk = rows // Q_ROWS
    idx = np.zeros((N_BIAS_CLASSES, Q_ROWS, BAND_ROWS), np.int64)
    ok = np.zeros((N_BIAS_CLASSES, Q_ROWS, BAND_ROWS), bool)
    representatives = list(range(EDGE_BLOCKS + 1)) + list(range(n_blk - EDGE_BLOCKS, n_blk))
    for cls, blk in enumerate(representatives):
        band0 = int(np.clip(blk * Q_ROWS - MAX_KR // 2, 0, rows - BAND_ROWS))
        for rho in range(Q_ROWS):
            r = blk * Q_ROWS + rho
            r0 = int(np.clip(r - MAX_KR // 2, 0, rows - MAX_KR))
            for kap in range(BAND_ROWS):
                kr = band0 + kap
                ok[cls, rho, kap] = r0 <= kr < r0 + MAX_KR
                idx[cls, rho, kap] = np.clip(kr - r + MAX_KR - 1, 0, 2 * MAX_KR - 2)
    t = cb[:, idx]
    t = jnp.where(jnp.asarray(ok)[None, :, :, :, None, None], t, -jnp.inf)
    t = jnp.transpose(t, (1, 0, 2, 4, 3, 5))
    return t.reshape(N_BIAS_CLASSES, NA_HEADS, Q_ROWS * GRID_W, BAND_ROWS * GRID_W)


def _attn_kernel(q_ref, k_ref, v_ref, qc_ref, kc_ref, vc_ref, bias_ref, o_ref, oc_ref, *, rows):
    lane = lax.broadcasted_iota(jnp.int32, (1, 2 * HEAD_DIM), 1)
    first = lane < HEAD_DIM
    kc = kc_ref[...]
    vc = vc_ref[...]
    zero = jnp.zeros((), BF16)

    def softmax_pv(s_list, v_list):
        m = s_list[0].max(axis=-1, keepdims=True)
        for s in s_list[1:]:
            m = jnp.maximum(m, s.max(axis=-1, keepdims=True))
        den = None
        acc = None
        for s, vv in zip(s_list, v_list):
            p = jnp.exp(s - m)
            ps = p.sum(axis=-1, keepdims=True)
            den = ps if den is None else den + ps
            pv = _dot(p.astype(BF16), vv)
            acc = pv if acc is None else acc + pv
        return acc / den

    n_blk = rows // Q_ROWS
    q_len = Q_ROWS * GRID_W

    def block_body(i, carry):
        band0 = jnp.clip(i * Q_ROWS - MAX_KR // 2, 0, rows - BAND_ROWS)
        cls = jnp.where(i < EDGE_BLOCKS, i,
                        jnp.where(i >= n_blk - EDGE_BLOCKS, i - (n_blk - N_BIAS_CLASSES), EDGE_BLOCKS))
        rows_q = pl.ds(pl.multiple_of(i * q_len, q_len), q_len)
        q_b = q_ref[rows_q, :]
        band = pl.ds(pl.multiple_of(band0 * GRID_W, GRID_W), BAND_ROWS * GRID_W)
        kb = k_ref[band, :]
        vb = v_ref[band, :]
        outs = []
        for hh in range(2):
            q_h = jnp.where(first if hh == 0 else ~first, q_b, zero)
            s_w = _dot_nt(q_h, kb) + bias_ref[cls, hh]
            s_c = _dot_nt(q_h, kc)
            outs.append(softmax_pv([s_w, s_c], [vb, vc]))
        o_ref[rows_q, :] = jnp.where(first, outs[0], outs[1]).astype(o_ref.dtype)
        return carry

    lax.fori_loop(0, n_blk, block_body, 0)

    q_c = qc_ref[...]
    outs = []
    for hh in range(2):
        q_h = jnp.where(first if hh == 0 else ~first, q_c, zero)
        outs.append(softmax_pv([_dot_nt(q_h, kc)], [vc]))
    oc_ref[...] = jnp.where(first, outs[0], outs[1]).astype(oc_ref.dtype)


def _attention(q, k, v, rpb, bsz, seq, ctx_len):
    rows = seq // GRID_W
    assert rows % Q_ROWS == 0 and rows >= max(N_BIAS_CLASSES * Q_ROWS, BAND_ROWS) and seq % ctx_len == 0
    bias = _na_bias_table(rpb, rows)
    cb0 = bsz * seq // ctx_len
    lat = pl.BlockSpec((seq, 2 * HEAD_DIM), lambda hp, b: (b, hp))
    ctx = pl.BlockSpec((ctx_len, 2 * HEAD_DIM), lambda hp, b: (cb0 + b, hp))
    return pl.pallas_call(
        functools.partial(_attn_kernel, rows=rows),
        grid=(NA_HEADS // 2, bsz),
        in_specs=[lat, lat, lat, ctx, ctx, ctx,
                  pl.BlockSpec((N_BIAS_CLASSES, 2, Q_ROWS * GRID_W, BAND_ROWS * GRID_W),
                               lambda hp, b: (0, hp, 0, 0))],
        out_specs=[pl.BlockSpec((seq, 2 * HEAD_DIM), lambda hp, b: (b, hp)),
                   pl.BlockSpec((ctx_len, 2 * HEAD_DIM), lambda hp, b: (b, hp))],
        out_shape=[jax.ShapeDtypeStruct((bsz * seq, NA_WIDTH), BF16),
                   jax.ShapeDtypeStruct((bsz * ctx_len, NA_WIDTH), BF16)],
        compiler_params=_cparams(("arbitrary", "arbitrary")),
        name="na_attention",
    )(q, k, v, q, k, v, bias)


@functools.lru_cache(maxsize=None)
def _dft_tables(n):
    j = np.arange(n, dtype=np.int64)
    ang = 2.0 * np.pi * ((j[:, None] * j[None, :]) % n).astype(np.float64) / n
    return np.cos(ang).astype(BF16), np.sin(ang).astype(BF16)


def _fourier_kernel(cos_ref, sin_ref, f_ref, bdc_ref, bds_ref, bdw_ref, o_ref, *, scale):
    fb = f_ref[...]
    zr = _dot(cos_ref[...], fb)
    zs = _dot(sin_ref[...], fb)
    fr = (_dot(zr.astype(BF16), bdc_ref[...]) - _dot(zs.astype(BF16), bds_ref[...])) * scale
    o_ref[...] = _dot(fr.astype(BF16), bdw_ref[...]).astype(o_ref.dtype)


def _fourier(f_all, bdw, bsz, n, row_block0, tm):
    cos_n, sin_n = _dft_tables(n)
    cos64, sin64 = _dft_tables(FOURIER_DIM)
    eye = np.eye(FOURIER_GROUPS)
    bdc = jnp.asarray(np.kron(eye, cos64.astype(np.float32)), BF16)
    bds = jnp.asarray(np.kron(eye, sin64.astype(np.float32)), BF16)
    mt = n // tm
    small = pl.BlockSpec((FOURIER_WIDTH, FOURIER_WIDTH), lambda b, m: (0, 0))
    return pl.pallas_call(
        functools.partial(_fourier_kernel, scale=1.0 / math.sqrt(n * FOURIER_DIM)),
        grid=(bsz, mt),
        in_specs=[pl.BlockSpec((tm, n), lambda b, m: (m, 0)),
                  pl.BlockSpec((tm, n), lambda b, m: (m, 0)),
                  pl.BlockSpec((n, FOURIER_WIDTH), lambda b, m: (row_block0 + b, 0)),
                  small, small, small],
        out_specs=pl.BlockSpec((tm, FOURIER_WIDTH), lambda b, m: (b * mt + m, 0)),
        out_shape=jax.ShapeDtypeStruct((bsz * n, FOURIER_WIDTH), BF16),
        compiler_params=_cparams(("arbitrary", "arbitrary")),
        name="fourier_mix",
    )(jnp.asarray(cos_n), jnp.asarray(sin_n), f_all, bdc, bds, bdw)


def _route(h2, rw_both_ref, rw_hi_ref, rb_ref):
    h_hi, h_lo = _split_bf16(h2)
    both = _dot(h_hi, rw_both_ref[...])
    logits = both[:, :N_EXPERTS] + both[:, N_EXPERTS:] + _dot(h_lo, rw_hi_ref[...]) + rb_ref[...]
    lane = lax.broadcasted_iota(jnp.int32, logits.shape, 1).astype(F32)
    work = logits
    sel_any = jnp.zeros(logits.shape, jnp.bool_)
    top = None
    for _ in range(TOP_K):
        m = work.max(axis=-1, keepdims=True)
        if top is None:
            top = m
        idx = jnp.where(work == m, lane, float(N_EXPERTS)).min(axis=-1, keepdims=True)
        sel = lane == idx
        sel_any = sel_any | sel
        work = jnp.where(sel, -jnp.inf, work)
    e = jnp.where(sel_any, jnp.exp(logits - top), 0.0)
    return sel_any.astype(F32), e / e.sum(axis=-1, keepdims=True)


def _post_l0_kernel(a_ref, ac_ref, f_ref, fc_ref, x_ref, c_ref, g1_ref, sh_ref, sc_ref, ng_ref, w_ref,
                    rwh_ref, rwl_ref, rb_ref, xo_ref, hs_ref, pos_ref, gate_ref, cnt_ref, *, n_lat_tiles):
    lat = pl.program_id(0) < n_lat_tiles
    a = jnp.where(lat, a_ref[...], ac_ref[...])
    f = jnp.where(lat, f_ref[...], fc_ref[...])
    x = jnp.where(lat, x_ref[...], c_ref[...])
    y = _dot(a, w_ref[:NA_WIDTH, :]) + _dot(f, w_ref[NA_WIDTH:, :])
    x = x + g1_ref[0] * y
    xo_ref[...] = x
    h2 = _modulate(x, ng_ref[...], sh_ref[0], sc_ref[0])
    _sort_tile(h2, *_route(h2, rwh_ref, rwl_ref, rb_ref), hs_ref, pos_ref, gate_ref, cnt_ref)


def _post_l1_kernel(yt_ref, x_ref, g1_ref, sh_ref, sc_ref, ng_ref, wgt_ref, bg_ref, wot_ref,
                    rwh_ref, rwl_ref, rb_ref, xo_ref, hs_ref, pos_ref, gate_ref, cnt_ref):
    gyt = yt_ref[...]
    zt = _dot(wgt_ref[...], gyt) + bg_ref[...]
    vt = (gyt.astype(F32) * jax.nn.sigmoid(zt)).astype(BF16)
    x = x_ref[...] + g1_ref[0] * _dot(wot_ref[...], vt).T
    xo_ref[...] = x
    h2 = _modulate(x, ng_ref[...], sh_ref[0], sc_ref[0])
    _sort_tile(h2, *_route(h2, rwh_ref, rwl_ref, rb_ref), hs_ref, pos_ref, gate_ref, cnt_ref)


def _router_args(router_w, router_b):
    rw_hi = router_w.astype(BF16)
    rw_lo = (router_w - rw_hi.astype(F32)).astype(BF16)
    return jnp.concatenate([rw_hi, rw_lo], axis=1), rw_hi, router_b.reshape(1, N_EXPERTS).astype(F32)


def _post_out(n_tok):
    n_tiles = n_tok // TM
    wide = pl.BlockSpec((TM, 128), lambda i: (i, 0))
    specs = [pl.BlockSpec((TM, D_MODEL), lambda i: (i, 0)),
             pl.BlockSpec((TILE_ROWS, HALF), lambda i: (i, 0)), wide, wide,
             pl.BlockSpec((8, 128), lambda i: (i, 0))]
    shapes = [jax.ShapeDtypeStruct((n_tok, D_MODEL), F32),
              jax.ShapeDtypeStruct((n_tiles * TILE_ROWS, HALF), jnp.uint32),
              jax.ShapeDtypeStruct((n_tok, 128), F32),
              jax.ShapeDtypeStruct((n_tok, 128), F32),
              jax.ShapeDtypeStruct((n_tiles * 8, 128), F32)]
    return specs, shapes


def _const_spec(shape):
    return pl.BlockSpec(shape, lambda i: tuple(0 for _ in shape))


def _post_l0(a_lat, a_ctx, f_lat, f_ctx, x, ctx, mod, norm_g, w_out_bf, router, n_lat, n_tok, tiles_per_batch):
    n_lat_tiles = n_lat // TM

    def lat(wd):
        return pl.BlockSpec((TM, wd), lambda i: (jnp.minimum(i, n_lat_tiles - 1), 0))

    def cx(wd):
        return pl.BlockSpec((TM, wd), lambda i: (jnp.maximum(i - n_lat_tiles, 0), 0))

    out_specs, out_shapes = _post_out(n_tok)
    return pl.pallas_call(
        functools.partial(_post_l0_kernel, n_lat_tiles=n_lat_tiles),
        grid=(n_tok // TM,),
        in_specs=[lat(NA_WIDTH), cx(NA_WIDTH), lat(FOURIER_WIDTH), cx(FOURIER_WIDTH),
                  lat(D_MODEL), cx(D_MODEL),
                  _mod_spec(2, tiles_per_batch), _mod_spec(3, tiles_per_batch), _mod_spec(4, tiles_per_batch),
                  _const_spec((1, D_MODEL)), _const_spec((D_MODEL, D_MODEL)),
                  _const_spec((D_MODEL, 2 * N_EXPERTS)), _const_spec((D_MODEL, N_EXPERTS)),
                  _const_spec((1, N_EXPERTS))],
        out_specs=out_specs, out_shape=out_shapes,
        compiler_params=_cparams(("arbitrary",)),
        name="post_mixer_l0",
    )(a_lat, a_ctx, f_lat, f_ctx, x, ctx, mod, mod, mod, norm_g.reshape(1, D_MODEL), w_out_bf, *router)


def _post_l1(gyt, x_all, mod, norm_g, w_glu_t_bf, b_glu, w_out_t_bf, router, n_lat, tiles_per_batch):
    out_specs, out_shapes = _post_out(n_lat)
    return pl.pallas_call(
        _post_l1_kernel,
        grid=(n_lat // TM,),
        in_specs=[pl.BlockSpec((D_MODEL, TM), lambda i: (0, i)),
                  pl.BlockSpec((TM, D_MODEL), lambda i: (i, 0)),
                  _mod_spec(2, tiles_per_batch), _mod_spec(3, tiles_per_batch), _mod_spec(4, tiles_per_batch),
                  _const_spec((1, D_MODEL)), _const_spec((D_MODEL, D_MODEL)), _const_spec((D_MODEL, 1)),
                  _const_spec((D_MODEL, D_MODEL)),
                  _const_spec((D_MODEL, 2 * N_EXPERTS)), _const_spec((D_MODEL, N_EXPERTS)),
                  _const_spec((1, N_EXPERTS))],
        out_specs=out_specs, out_shape=out_shapes,
        compiler_params=_cparams(("arbitrary",)),
        name="post_mixer_l1",
    )(gyt, x_all, mod, mod, mod, norm_g.reshape(1, D_MODEL), w_glu_t_bf, b_glu.reshape(D_MODEL, 1),
      w_out_t_bf, *router)


CHUNK = 8
TILE_ROWS = TM * TOP_K + N_EXPERTS * CHUNK
TILE_CHUNKS = TILE_ROWS // CHUNK
BLOCK_CHUNKS = EXPERT_ROWS // CHUNK
SORT_ROWS = 256
SORT_GROUP = 3
ISSUE_UNROLL = 8
FF_CHUNK = 256
HALF = D_MODEL // 2


def _pack_bf16_pairs(y):
    bits = pltpu.bitcast(y, jnp.uint32)
    return (bits[:, :HALF] & jnp.uint32(0xFFFF0000)) | (bits[:, HALF:] >> 16)


def _unpack_bf16_pairs(w):
    hi = pltpu.bitcast(w & jnp.uint32(0xFFFF0000), F32).astype(BF16)
    lo = pltpu.bitcast(w << 16, F32).astype(BF16)
    return hi, lo


def _sort_tile(h2, mf, g, hs_ref, pos_ref, gate_ref, cnt_ref):
    m = mf.astype(BF16)
    r = lax.broadcasted_iota(jnp.int32, (TM, TM), 0)
    c = lax.broadcasted_iota(jnp.int32, (TM, TM), 1)
    rank = _dot((r > c).astype(F32).astype(BF16), m)
    cnt = mf.sum(axis=0, keepdims=True)
    n8 = jnp.floor((cnt + (CHUNK - 1)) * (1.0 / CHUNK))
    er = lax.broadcasted_iota(jnp.int32, (N_EXPERTS, N_EXPERTS), 0)
    ec = lax.broadcasted_iota(jnp.int32, (N_EXPERTS, N_EXPERTS), 1)
    upper = (er < ec).astype(F32).astype(BF16)
    off = CHUNK * _dot(jnp.broadcast_to(n8, (8, N_EXPERTS)).astype(BF16), upper)[0:1]
    pos = rank + off
    slot = _dot(m, upper)
    chosen = mf > 0.5
    lane = lax.broadcasted_iota(jnp.int32, pos_ref.shape, 1)
    p_out = jnp.zeros(pos_ref.shape, F32)
    g_out = jnp.zeros(gate_ref.shape, F32)
    for k in range(TOP_K):
        sel = chosen & (slot == float(k))
        pk = jnp.where(sel, pos, 0.0).sum(axis=-1, keepdims=True)
        gk = jnp.where(sel, g, 0.0).sum(axis=-1, keepdims=True)
        p_out = jnp.where(lane == k, pk, p_out)
        g_out = jnp.where(lane == k, gk, g_out)
    pos_ref[...] = p_out
    gate_ref[...] = g_out
    cnt_ref[...] = jnp.zeros_like(cnt_ref)
    cnt_ref[0:1, 0:N_EXPERTS] = cnt

    pos_t = p_out.T
    hb = h2.astype(BF16)
    rid = lax.broadcasted_iota(jnp.int32, (SORT_ROWS, TM), 0).astype(F32).astype(BF16)
    one, nil = jnp.ones((), BF16), jnp.zeros((), BF16)
    def window(rc):
        rel = (pos_t[0:16, :] - float(rc * SORT_ROWS)).astype(BF16)
        onehot = jnp.where(rid == rel[0:1, :], one, nil)
        for k in range(1, TOP_K):
            onehot = jnp.where(rid == rel[k:k + 1, :], one, onehot)
        return onehot

    for grp in range(TILE_ROWS // (SORT_ROWS * SORT_GROUP)):
        onehot = jnp.concatenate([window(grp * SORT_GROUP + j) for j in range(SORT_GROUP)], axis=0)
        rows = slice(grp * SORT_GROUP * SORT_ROWS, (grp + 1) * SORT_GROUP * SORT_ROWS)
        hs_ref[rows, :] = _pack_bf16_pairs(_dot(onehot, hb))


def _chunk_tables(cnt, n_blocks):
    n_tiles = cnt.shape[0]
    n8 = (cnt + CHUNK - 1) // CHUNK
    lend = jnp.cumsum(n8, axis=1)
    lstart = lend - n8
    cum = jnp.cumsum(n8, axis=0)
    before = cum - n8
    total = cum[-1]
    nblk = (total + BLOCK_CHUNKS - 1) // BLOCK_CHUNKS
    bend = jnp.cumsum(nblk)
    gstart = (bend - nblk) * BLOCK_CHUNKS
    n_used = jnp.maximum(bend[-1], 1)
    blk = jnp.arange(n_blocks, dtype=jnp.int32)
    block_expert = (bend[None, :] <= jnp.minimum(blk, n_used - 1)[:, None]).sum(axis=1).astype(jnp.int32)
    block_expert = jnp.minimum(block_expert, N_EXPERTS - 1)
    experts = jnp.arange(N_EXPERTS, dtype=jnp.int32)
    lc = jnp.arange(TILE_CHUNKS, dtype=jnp.int32)
    e_of = (lend[:, None, :] <= lc[None, :, None]).sum(axis=-1).astype(jnp.int32)
    shift = gstart[None, :] + before - lstart
    pick = e_of[:, :, None] == experts[None, None, :]
    to_sorted = jnp.where(pick, shift[:, None, :], 0).sum(axis=-1) + jnp.where(e_of < N_EXPERTS, lc[None, :], 0)
    of_block = block_expert[:, None] == experts[None, :]
    gstart_b = jnp.where(of_block, gstart[None, :], 0).sum(axis=-1)
    total_b = jnp.where(of_block, total[None, :], 0).sum(axis=-1)

    def per_block(table):
        return jnp.where(of_block[:, :, None], table.T[None, :, :], 0).sum(axis=1)

    cum_b, lstart_b, before_b = per_block(cum), per_block(lstart), per_block(before)
    rel = (blk[:, None] * BLOCK_CHUNKS + jnp.arange(BLOCK_CHUNKS, dtype=jnp.int32)[None, :]) - gstart_b[:, None]
    valid = (blk[:, None] < n_used) & (rel < total_b[:, None])
    tile = jnp.minimum((cum_b[:, None, :] <= rel[:, :, None]).sum(axis=-1), n_tiles - 1).astype(jnp.int32)
    of_tile = tile[:, :, None] == jnp.arange(n_tiles, dtype=jnp.int32)[None, None, :]
    inside = jnp.where(of_tile, (lstart_b - before_b)[:, None, :], 0).sum(axis=-1)
    from_tiles = jnp.where(valid, tile * TILE_CHUNKS + inside + rel, TILE_CHUNKS - 1)
    return (block_expert, n_used.reshape(1).astype(jnp.int32), from_tiles.reshape(-1).astype(jnp.int32),
            to_sorted.reshape(-1).astype(jnp.int32))


def _chunk_rows(chunk):
    start = chunk * CHUNK
    return pl.ds(start if isinstance(start, int) else pl.multiple_of(start, CHUNK), CHUNK)


def _chunk_copy(src_ref, src_chunk, dst_ref, dst_chunk, sem):
    return pltpu.make_async_copy(src_ref.at[_chunk_rows(src_chunk), :], dst_ref.at[_chunk_rows(dst_chunk), :], sem)


def _expert_kernel(be_ref, nu_ref, src_ref, hs_ref, wgu_ref, bgu_ref, wd_ref, bd_ref, o_ref,
                   wgu_bf, wd_bf, x_buf, sems):
    i = pl.program_id(0)
    n_used = nu_ref[0]

    def gather(block, slot, start):
        if not start:
            pltpu.make_async_copy(hs_ref.at[pl.ds(0, EXPERT_ROWS), :], x_buf.at[slot], sems.at[slot]).wait()
            return
        for cidx in range(BLOCK_CHUNKS):
            _chunk_copy(hs_ref, src_ref[block * BLOCK_CHUNKS + cidx], x_buf.at[slot], cidx, sems.at[slot]).start()

    @pl.when(i == 0)
    def _():
        gather(0, 0, True)

    for slot in range(2):
        @pl.when((i + 1 < n_used) & ((i + 1) % 2 == slot))
        def _(slot=slot):
            gather(i + 1, slot, True)

    @pl.when((i == 0) | (be_ref[i] != be_ref[jnp.maximum(i - 1, 0)]))
    def _():
        wgu_bf[...] = wgu_ref[0].astype(BF16)
        wd_bf[...] = wd_ref[0].astype(BF16)

    for slot in range(2):
        @pl.when((i < n_used) & (i % 2 == slot))
        def _(slot=slot):
            gather(i, slot, False)
            x_hi, x_lo = _unpack_bf16_pairs(x_buf[slot])
            y = None
            for j in range(D_MODEL // FF_CHUNK):
                def proj(col0):
                    cols = slice(col0 + j * FF_CHUNK, col0 + (j + 1) * FF_CHUNK)
                    return (_dot(x_hi, wgu_bf[:HALF, cols]) + _dot(x_lo, wgu_bf[HALF:, cols])
                            + bgu_ref[0, :, cols])
                gate = jnp.minimum(proj(0), SWIGLU_LIMIT)
                up = jnp.clip(proj(D_MODEL), -SWIGLU_LIMIT, SWIGLU_LIMIT)
                act = (up + 1.0) * (gate * jax.nn.sigmoid(SWIGLU_ALPHA * gate))
                part = _dot(act.astype(BF16), wd_bf[j * FF_CHUNK:(j + 1) * FF_CHUNK, :])
                y = part if y is None else y + part
            y = y + bd_ref[0]
            o_ref[...] = _pack_bf16_pairs(y.astype(BF16).astype(F32))

    @pl.when(i >= n_used)
    def _():
        o_ref[...] = jnp.zeros_like(o_ref)


def _experts(hs, block_expert, n_used, from_tiles, wgu, bgu, wd, bd):
    n_blocks = block_expert.shape[0]

    def row_map(i, be, nu, src):
        return (i, 0)

    def exp_map(i, be, nu, src):
        return (be[i], 0, 0)

    grid_spec = pltpu.PrefetchScalarGridSpec(
        num_scalar_prefetch=3,
        grid=(n_blocks,),
        in_specs=[pl.BlockSpec(memory_space=pl.ANY),
                  pl.BlockSpec((1, D_MODEL, 2 * D_MODEL), exp_map),
                  pl.BlockSpec((1, 1, 2 * D_MODEL), exp_map),
                  pl.BlockSpec((1, D_MODEL, D_MODEL), exp_map),
                  pl.BlockSpec((1, 1, D_MODEL), exp_map)],
        out_specs=pl.BlockSpec((EXPERT_ROWS, HALF), row_map),
        scratch_shapes=[pltpu.VMEM((D_MODEL, 2 * D_MODEL), BF16), pltpu.VMEM((D_MODEL, D_MODEL), BF16),
                        pltpu.VMEM((2, EXPERT_ROWS, HALF), jnp.uint32), pltpu.SemaphoreType.DMA((2,))],
    )
    return pl.pallas_call(
        _expert_kernel,
        grid_spec=grid_spec,
        out_shape=jax.ShapeDtypeStruct((n_blocks * EXPERT_ROWS, HALF), jnp.uint32),
        compiler_params=pltpu.CompilerParams(dimension_semantics=("arbitrary",),
                                             vmem_limit_bytes=EXPERT_VMEM_LIMIT),
        name="moe_experts",
    )(block_expert, n_used, from_tiles, hs, wgu, bgu.reshape(N_EXPERTS, 1, 2 * D_MODEL), wd,
      bd.reshape(N_EXPERTS, 1, D_MODEL))


def _combine_kernel(dst_ref, ys_ref, pos_ref, gate_ref, x_ref, g2_ref, ng_ref, o_ref, y_buf, sems, *, final_norm):
    i = pl.program_id(0)
    n_tiles = pl.num_programs(0)

    def gather(tile, slot, start):
        if not start:
            pltpu.make_async_copy(ys_ref.at[pl.ds(0, TILE_ROWS), :], y_buf.at[slot], sems.at[slot]).wait()
            return

        def body(g, carry):
            for j in range(ISSUE_UNROLL):
                cidx = g * ISSUE_UNROLL + j
                _chunk_copy(ys_ref, dst_ref[tile * TILE_CHUNKS + cidx], y_buf.at[slot], cidx, sems.at[slot]).start()
            return carry
        lax.fori_loop(0, TILE_CHUNKS // ISSUE_UNROLL, body, 0)

    @pl.when(i == 0)
    def _():
        gather(0, 0, True)

    for slot in range(2):
        @pl.when((i + 1 < n_tiles) & ((i + 1) % 2 == slot))
        def _(slot=slot):
            gather(i + 1, slot, True)

    pos = pos_ref[...]
    gate = gate_ref[...]
    for slot in range(2):
        @pl.when(i % 2 == slot)
        def _(slot=slot):
            gather(i, slot, False)
            acc_hi = jnp.zeros((TM, HALF), F32)
            acc_lo = jnp.zeros((TM, HALF), F32)
            rid = lax.broadcasted_iota(jnp.int32, (TM, SORT_ROWS), 1).astype(F32).astype(BF16)
            gate_b = gate.astype(BF16)
            for rc in range(TILE_ROWS // SORT_ROWS):
                rel = (pos - float(rc * SORT_ROWS)).astype(BF16)
                w = jnp.where(rid == rel[:, 0:1], gate_b[:, 0:1], jnp.zeros((), BF16))
                for k in range(1, TOP_K):
                    w = jnp.where(rid == rel[:, k:k + 1], gate_b[:, k:k + 1], w)
                y_hi, y_lo = _unpack_bf16_pairs(y_buf[slot, rc * SORT_ROWS:(rc + 1) * SORT_ROWS, :])
                acc_hi = acc_hi + _dot(w, y_hi)
                acc_lo = acc_lo + _dot(w, y_lo)
            x = x_ref[...] + g2_ref[0] * jnp.concatenate([acc_hi, acc_lo], axis=-1)
            if final_norm:
                ms = jnp.mean(x * x, axis=-1, keepdims=True)
                x = x * lax.rsqrt(ms + EPS) * ng_ref[...]
            o_ref[...] = x


def _combine(ys, to_sorted, pos4, gate4, x_res, mod, norm_g, tiles_per_batch, final_norm):
    n_tok = x_res.shape[0]
    tok = pl.BlockSpec((TM, D_MODEL), lambda i, dst: (i, 0))
    wide = pl.BlockSpec((TM, 128), lambda i, dst: (i, 0))
    tiles, bsz = tiles_per_batch
    grid_spec = pltpu.PrefetchScalarGridSpec(
        num_scalar_prefetch=1,
        grid=(n_tok // TM,),
        in_specs=[pl.BlockSpec(memory_space=pl.ANY), wide, wide, tok,
                  pl.BlockSpec((1, 1, D_MODEL), lambda i, dst: (jnp.minimum(i // tiles, bsz) * 6 + 5, 0, 0)),
                  pl.BlockSpec((1, D_MODEL), lambda i, dst: (0, 0))],
        out_specs=tok,
        scratch_shapes=[pltpu.VMEM((2, TILE_ROWS, HALF), jnp.uint32), pltpu.SemaphoreType.DMA((2,))],
    )
    return pl.pallas_call(
        functools.partial(_combine_kernel, final_norm=final_norm),
        grid_spec=grid_spec,
        out_shape=jax.ShapeDtypeStruct((n_tok, D_MODEL), F32),
        compiler_params=_cparams(("arbitrary",)),
        name="moe_combine",
    )(to_sorted, ys, pos4, gate4, x_res, mod, norm_g.reshape(1, D_MODEL))


def _moe(hs, pos4, gate4, cnt, x_res, mod, moe_w, tiles_per_batch, norm_g, final_norm):
    wgu, bgu, wd, bd = moe_w
    n_tiles = x_res.shape[0] // TM
    n_blocks = -(-n_tiles * TILE_CHUNKS // BLOCK_CHUNKS) + N_EXPERTS
    cnt_i = cnt.reshape(n_tiles, 8, 128)[:, 0, :N_EXPERTS].astype(jnp.int32)
    block_expert, n_used, from_tiles, to_sorted = _chunk_tables(cnt_i, n_blocks)
    ys = _experts(hs, block_expert, n_used, from_tiles, wgu, bgu, wd, bd)
    return _combine(ys, to_sorted, pos4, gate4, x_res, mod, norm_g, tiles_per_batch, final_norm)


def _s5_tables(lam_re, lam_im, log_dt, b_re, b_im, c_re, c_im, d_skip):
    q = S5_Q
    lam = lax.complex(jnp.minimum(lam_re.astype(F32), LAMBDA_RE_MAX), lam_im.astype(F32))
    dt = jnp.exp(log_dt.astype(F32))[..., None]
    a = jnp.exp(lam * dt)
    bbar = ((a - 1) / lam)[..., None] * lax.complex(b_re.astype(F32), b_im.astype(F32))
    cc = lax.complex(c_re.astype(F32), c_im.astype(F32))
    steps = jnp.arange(q + 1, dtype=F32)
    pw = jnp.exp((lam * dt)[:, :, None, :] * steps[None, None, :, None])
    kern = jnp.einsum('dgop,dgkp,dgpc->dgcok', cc, pw[:, :, :q], bbar).real
    eye = jnp.eye(SSM_GROUP, dtype=F32)
    k_fwd = kern[0].at[..., 0].add(kern[1][..., 0] + d_skip.astype(F32)[:, :, None] * eye[None])
    k_bwd = jnp.concatenate([jnp.zeros_like(kern[1][..., :1]), kern[1][..., :0:-1]], axis=-1)
    k_fwd = k_fwd.reshape(SSM_GROUPS, SSM_GROUP * SSM_GROUP, q)
    k_bwd = k_bwd.reshape(SSM_GROUPS, SSM_GROUP * SSM_GROUP, q)
    def both(f, b):
        return jnp.concatenate([f, b], axis=-1)

    e_in = both(pw[0][:, ::-1][:, 1:], pw[1][:, :q])
    b_in = both(jnp.transpose(bbar[0], (0, 2, 1)), jnp.transpose(bbar[1], (0, 2, 1)))
    e_out = jnp.transpose(both(pw[0][:, 1:], pw[1][:, ::-1][:, :q]), (0, 2, 1))
    c_out = jnp.transpose(both(cc[0], cc[1]), (0, 2, 1))
    a_q = both(pw[0][:, q], pw[1][:, q]).reshape(SSM_GROUPS, 1, 2 * SSM_STATE)
    parts = [t for z in (e_in, b_in, e_out, c_out, a_q) for t in (z.real, z.imag)]
    return [k_fwd, k_bwd] + parts


def _s5_kernel(u_ref, kf_ref, kb_ref, er_ref, ei_ref, br_ref, bi_ref, pr_ref, pi_ref, cr_ref, ci_ref,
               ar_ref, ai_ref, o_ref, w_ref, s_ref, h_ref, *, bsz, n_lat_chunks, n_ctx_chunks):
    q = S5_Q
    half = 2 * SSM_STATE
    n_lat = bsz * n_lat_chunks

    causal = (lax.broadcasted_iota(jnp.int32, (q, q), 1) >= lax.broadcasted_iota(jnp.int32, (q, q), 0))

    def build(c, carry):
        for o in range(SSM_GROUP):
            row = pl.ds(c * SSM_GROUP + o, 1)
            kf = pltpu.roll(jnp.broadcast_to(kf_ref[0, row, :], (q, q)), 0, 1, stride=1, stride_axis=0)
            kb = pltpu.roll(jnp.broadcast_to(kb_ref[0, row, :], (q, q)), 0, 1, stride=1, stride_axis=0)
            w_ref[pl.ds(pl.multiple_of(c * q, q), q), o * q:(o + 1) * q] = jnp.where(causal, kf, kb).astype(BF16)
        return carry

    lax.fori_loop(0, SSM_GROUP, build, 0)

    u = u_ref[...]
    er, ei = er_ref[0], ei_ref[0]
    s_re = None
    for c in range(SSM_GROUP):
        br, bi = br_ref[0, c:c + 1, :], bi_ref[0, c:c + 1, :]
        w_re = (er * br - ei * bi).astype(BF16)
        w_im = (er * bi + ei * br).astype(BF16)
        s_re = _dot(u[c], w_re) if s_re is None else s_re + _dot(u[c], w_re)
        s_im = _dot(u[c], w_im) if c == 0 else s_im + _dot(u[c], w_im)
    s_ref[0] = s_re
    s_ref[1] = s_im

    fwd = lax.broadcasted_iota(jnp.int32, (bsz, half), 1) < SSM_STATE
    ar = jnp.broadcast_to(ar_ref[0], (bsz, half))
    ai = jnp.broadcast_to(ai_ref[0], (bsz, half))

    def chunk_rows(base, j, per_sample):
        return pl.ds(base + j, bsz, stride=per_sample)

    def load(base, j, n):
        rows_f = chunk_rows(base, j, n)
        rows_b = chunk_rows(base, n - 1 - j, n)
        return (jnp.where(fwd, s_ref[0, rows_f, :], s_ref[0, rows_b, :]),
                jnp.where(fwd, s_ref[1, rows_f, :], s_ref[1, rows_b, :]))

    def step(hr, hi, sr, si):
        return ar * hr - ai * hi + sr, ar * hi + ai * hr + si

    def ctx_body(j, carry):
        return step(*carry, *load(n_lat, j, n_ctx_chunks))

    def lat_body(j, carry):
        hr, hi = carry
        rows_f = chunk_rows(0, j, n_lat_chunks)
        rows_b = chunk_rows(0, n_lat_chunks - 1 - j, n_lat_chunks)
        h_ref[0, rows_f, :] = jnp.where(fwd, hr, 0.0)
        h_ref[1, rows_f, :] = jnp.where(fwd, hi, 0.0)
        h_ref[2, rows_b, :] = jnp.where(fwd, 0.0, hr)
        h_ref[3, rows_b, :] = jnp.where(fwd, 0.0, hi)
        return step(hr, hi, *load(0, j, n_lat_chunks))

    zero = jnp.zeros((bsz, half), F32)
    carry = lax.fori_loop(0, n_ctx_chunks, ctx_body, (zero, zero))
    lax.fori_loop(0, n_lat_chunks, lat_body, carry)

    ucat = jnp.concatenate([u[c][:n_lat] for c in range(SSM_GROUP)], axis=-1)
    y = _dot(ucat, w_ref[...])
    h_re = (h_ref[0] + h_ref[2]).astype(BF16)
    h_im = (h_ref[1] + h_ref[3]).astype(BF16)
    pr, pi = pr_ref[0], pi_ref[0]
    cr_all, ci_all = cr_ref[0], ci_ref[0]
    for o in range(SSM_GROUP):
        cr, ci = cr_all[:, o:o + 1], ci_all[:, o:o + 1]
        w_re = (pr * cr - pi * ci).astype(BF16)
        w_im = (pr * ci + pi * cr).astype(BF16)
        y_o = y[:, o * q:(o + 1) * q] + _dot(h_re, w_re) - _dot(h_im, w_im)
        o_ref[o] = jax.nn.gelu(y_o).astype(o_ref.dtype)


def _s5(u3, tables, bsz, n_lat_chunks, n_ctx_chunks):
    q = S5_Q
    n_chunks = u3.shape[1]
    n_lat = bsz * n_lat_chunks
    assert n_chunks == n_lat + bsz * n_ctx_chunks

    def gspec(*shape):
        return pl.BlockSpec((1,) + shape, lambda g: (g,) + tuple(0 for _ in shape))

    return pl.pallas_call(
        functools.partial(_s5_kernel, bsz=bsz, n_lat_chunks=n_lat_chunks, n_ctx_chunks=n_ctx_chunks),
        grid=(SSM_GROUPS,),
        in_specs=[pl.BlockSpec((SSM_GROUP, n_chunks, q), lambda g: (g, 0, 0)),
                  gspec(SSM_GROUP * SSM_GROUP, q), gspec(SSM_GROUP * SSM_GROUP, q),
                  gspec(q, 2 * SSM_STATE), gspec(q, 2 * SSM_STATE),
                  gspec(SSM_GROUP, 2 * SSM_STATE), gspec(SSM_GROUP, 2 * SSM_STATE),
                  gspec(2 * SSM_STATE, q), gspec(2 * SSM_STATE, q),
                  gspec(2 * SSM_STATE, SSM_GROUP), gspec(2 * SSM_STATE, SSM_GROUP),
                  gspec(1, 2 * SSM_STATE), gspec(1, 2 * SSM_STATE)],
        out_specs=pl.BlockSpec((SSM_GROUP, n_lat, q), lambda g: (g, 0, 0)),
        out_shape=jax.ShapeDtypeStruct((SSM_GROUPS * SSM_GROUP, n_lat, q), BF16),
        scratch_shapes=[pltpu.VMEM((SSM_GROUP * q, SSM_GROUP * q), BF16),
                        pltpu.VMEM((2, n_chunks, 2 * SSM_STATE), F32),
                        pltpu.VMEM((4, n_lat, 2 * SSM_STATE), F32)],
        compiler_params=_cparams(("arbitrary",)),
        name="s5_scan",
    )(u3, *tables)


def _moe_weights(w_gate_up, b_gate_up, w_down, b_down):
    return w_gate_up, b_gate_up, w_down, b_down


def kernel(x, c, ctx, c_ctx, l0_ada_w, l0_ada_b, l0_norm_mix, l0_w_in, l0_rpb, l0_w_fourier, l0_w_out, l0_norm_ffn, l0_router_w, l0_router_b, l0_w_gate_up, l0_b_gate_up, l0_w_down, l0_b_down, l1_ada_w, l1_ada_b, l1_norm_mix, l1_w_in, l1_lambda_re, l1_lambda_im, l1_log_dt, l1_b_re, l1_b_im, l1_c_re, l1_c_im, l1_d_skip, l1_w_glu, l1_b_glu, l1_w_out, l1_norm_ffn, l1_router_w, l1_router_b, l1_w_gate_up, l1_b_gate_up, l1_w_down, l1_b_down, final_norm):
    bsz, seq, _ = x.shape
    ctx_len = ctx.shape[1]
    n_lat = bsz * seq
    n_ctx = bsz * ctx_len
    n_tok = n_lat + n_ctx
    assert seq % TM == 0 and n_ctx % TM == 0
    x2 = x.reshape(n_lat, D_MODEL)
    ctx2 = ctx.reshape(n_ctx, D_MODEL)

    n_mod = bsz + 1
    c_pad = jnp.zeros((-(-n_mod // 8) * 8, D_MODEL), F32).at[:bsz].set(c).at[bsz].set(c_ctx)
    mod0 = _ada(c_pad, l0_ada_w, l0_ada_b)[:n_mod].reshape(n_mod * 6, 1, D_MODEL)
    mod1 = _ada(c_pad, l1_ada_w, l1_ada_b)[:n_mod].reshape(n_mod * 6, 1, D_MODEL)
    tpb = (seq // TM, bsz)

    q, k, v, f = _modproj([x2, ctx2], mod0, l0_norm_mix, l0_w_in.astype(BF16),
                          (NA_WIDTH, NA_WIDTH, NA_WIDTH, FOURIER_WIDTH), n_lat, n_tok, tpb,
                          q_scale=HEAD_DIM ** -0.5)
    a_lat, a_ctx = _attention(q, k, v, l0_rpb, bsz, seq, ctx_len)
    bdw = (jnp.eye(FOURIER_GROUPS, dtype=F32)[:, None, :, None] * l0_w_fourier[:, :, None, :]).reshape(
        FOURIER_WIDTH, FOURIER_WIDTH).astype(BF16)
    f_lat = _fourier(f, bdw, bsz, seq, 0, TM)
    f_ctx = _fourier(f, bdw, bsz, ctx_len, n_lat // ctx_len, ctx_len)
    x_all, *routed = _post_l0(a_lat, a_ctx, f_lat, f_ctx, x2, ctx2, mod0, l0_norm_ffn,
                              l0_w_out.astype(BF16), _router_args(l0_router_w, l0_router_b), n_lat, n_tok, tpb)
    x_all = _moe(*routed, x_all, mod0, _moe_weights(l0_w_gate_up, l0_b_gate_up, l0_w_down, l0_b_down),
                 tpb, final_norm, False)

    assert seq % S5_Q == 0 and ctx_len % S5_Q == 0
    ut = _modproj_t(x_all, mod1, l1_norm_mix, l1_w_in.T.astype(BF16), n_tok, tpb)
    tables = _s5_tables(l1_lambda_re, l1_lambda_im, l1_log_dt, l1_b_re, l1_b_im, l1_c_re, l1_c_im, l1_d_skip)
    yt = _s5(ut.reshape(D_MODEL, n_tok // S5_Q, S5_Q), tables, bsz, seq // S5_Q, ctx_len // S5_Q)
    gy = yt.reshape(D_MODEL, n_lat)
    x1, *routed = _post_l1(gy, x_all, mod1, l1_norm_ffn, l1_w_glu.T.astype(BF16), l1_b_glu,
                           l1_w_out.T.astype(BF16), _router_args(l1_router_w, l1_router_b), n_lat, tpb)
    out = _moe(*routed, x1, mod1, _moe_weights(l1_w_gate_up, l1_b_gate_up, l1_w_down, l1_b_down),
               tpb, final_norm, True)
    return out.reshape(bsz, seq, D_MODEL)
```

```python
import functools
import math

import numpy as np
import jax
import jax.numpy as jnp
from jax import lax
from jax.experimental import pallas as pl
from jax.experimental.pallas import tpu as pltpu

F32 = jnp.float32
BF16 = jnp.bfloat16

D_MODEL = 1024
GRID_W = 64
HEAD_DIM = 64
NA_WIDTH = 512
NA_HEADS = 8
MAX_KR = 8
KC = 16
FOURIER_DIM = 64
FOURIER_GROUPS = 8
FOURIER_WIDTH = 512
SSM_GROUP = 16
SSM_GROUPS = 64
SSM_STATE = 64
LAMBDA_RE_MAX = -1e-4
N_EXPERTS = 32
TOP_K = 4
SWIGLU_ALPHA = 1.702
SWIGLU_LIMIT = 7.0
EPS = 1e-6

TM = 512
EXPERT_ROWS = 512
S5_Q = 128
VMEM_LIMIT = 48 * 1024 * 1024
EXPERT_VMEM_LIMIT = 58 * 1024 * 1024


def _cparams(sem):
    return pltpu.CompilerParams(dimension_semantics=sem, vmem_limit_bytes=VMEM_LIMIT)


def _split_bf16(a):
    hi = a.astype(BF16)
    lo = (a - hi.astype(F32)).astype(BF16)
    return hi, lo


def _dot(a, b):
    return jnp.dot(a, b, preferred_element_type=F32)


def _dot_nt(a, b):
    return lax.dot_general(a, b, (((1,), (1,)), ((), ())), preferred_element_type=F32)


def _ada_kernel(c_ref, w_ref, b_ref, o_ref):
    c = c_ref[...]
    s = c * jax.nn.sigmoid(c)
    s_hi, s_lo = _split_bf16(s)
    w_hi, w_lo = _split_bf16(w_ref[...])
    o_ref[...] = _dot(s_hi, w_hi) + _dot(s_lo, w_hi) + _dot(s_hi, w_lo) + b_ref[...]


def _ada(c_pad, ada_w, ada_b):
    n = ada_w.shape[1]
    tn = 1024
    return pl.pallas_call(
        _ada_kernel,
        grid=(n // tn,),
        in_specs=[pl.BlockSpec((c_pad.shape[0], D_MODEL), lambda j: (0, 0)),
                  pl.BlockSpec((D_MODEL, tn), lambda j: (0, j)),
                  pl.BlockSpec((1, tn), lambda j: (0, j))],
        out_specs=pl.BlockSpec((c_pad.shape[0], tn), lambda j: (0, j)),
        out_shape=jax.ShapeDtypeStruct((c_pad.shape[0], n), F32),
        compiler_params=_cparams(("arbitrary",)),
        name="ada_mod",
    )(c_pad, ada_w, ada_b.reshape(1, n))


def _mod_spec(which, tiles_per_batch):
    tiles, bsz = tiles_per_batch
    return pl.BlockSpec((1, 1, D_MODEL), lambda i: (jnp.minimum(i // tiles, bsz) * 6 + which, 0, 0))


def _modulate(x, g, sh, sc):
    ms = jnp.mean(x * x, axis=-1, keepdims=True)
    y = x * lax.rsqrt(ms + EPS) * g
    return y * (1.0 + sc) + sh


def _modproj_kernel(*refs, n_lat_tiles, two_src, widths, q_scale):
    if two_src:
        x_ref, c_ref, sh_ref, sc_ref, g_ref, w_ref = refs[:6]
        o_refs = refs[6:]
        i = pl.program_id(0)
        x = jnp.where(i < n_lat_tiles, x_ref[...], c_ref[...])
    else:
        x_ref, sh_ref, sc_ref, g_ref, w_ref = refs[:5]
        o_refs = refs[5:]
        x = x_ref[...]
    h = _modulate(x, g_ref[...], sh_ref[0], sc_ref[0]).astype(BF16)
    off = 0
    for j, (o_ref, wd) in enumerate(zip(o_refs, widths)):
        y = _dot(h, w_ref[:, off:off + wd])
        if j == 0 and q_scale != 1.0:
            y = y * q_scale
        o_ref[...] = y.astype(o_ref.dtype)
        off += wd


def _modproj(x_srcs, mod, norm_g, w_bf, widths, n_lat, n_tok, tiles_per_batch, q_scale=1.0):
    n_tiles = n_tok // TM
    n_lat_tiles = n_lat // TM
    two_src = len(x_srcs) == 2
    n_out = w_bf.shape[1]
    if two_src:
        x_specs = [pl.BlockSpec((TM, D_MODEL), lambda i: (jnp.minimum(i, n_lat_tiles - 1), 0)),
                   pl.BlockSpec((TM, D_MODEL), lambda i: (jnp.maximum(i - n_lat_tiles, 0), 0))]
    else:
        x_specs = [pl.BlockSpec((TM, D_MODEL), lambda i: (i, 0))]
    return pl.pallas_call(
        functools.partial(_modproj_kernel, n_lat_tiles=n_lat_tiles, two_src=two_src,
                          widths=tuple(widths), q_scale=q_scale),
        grid=(n_tiles,),
        in_specs=x_specs + [_mod_spec(0, tiles_per_batch), _mod_spec(1, tiles_per_batch),
                            pl.BlockSpec((1, D_MODEL), lambda i: (0, 0)),
                            pl.BlockSpec((D_MODEL, n_out), lambda i: (0, 0))],
        out_specs=[pl.BlockSpec((TM, wd), lambda i: (i, 0)) for wd in widths],
        out_shape=[jax.ShapeDtypeStruct((n_tok, wd), BF16) for wd in widths],
        compiler_params=_cparams(("arbitrary",)),
        name="modulate_in_proj",
    )(*x_srcs, mod, mod, norm_g.reshape(1, D_MODEL), w_bf)


def _modproj_t_kernel(x_ref, sh_ref, sc_ref, g_ref, wt_ref, o_ref):
    h = _modulate(x_ref[...], g_ref[...], sh_ref[0], sc_ref[0]).astype(BF16)
    o_ref[...] = _dot_nt(wt_ref[...], h).astype(o_ref.dtype)


def _modproj_t(x_all, mod, norm_g, wt_bf, n_tok, tiles_per_batch):
    n_out = wt_bf.shape[0]
    return pl.pallas_call(
        _modproj_t_kernel,
        grid=(n_tok // TM,),
        in_specs=[pl.BlockSpec((TM, D_MODEL), lambda i: (i, 0)),
                  _mod_spec(0, tiles_per_batch), _mod_spec(1, tiles_per_batch),
                  pl.BlockSpec((1, D_MODEL), lambda i: (0, 0)),
                  pl.BlockSpec((n_out, D_MODEL), lambda i: (0, 0))],
        out_specs=pl.BlockSpec((n_out, TM), lambda i: (0, i)),
        out_shape=jax.ShapeDtypeStruct((n_out, n_tok), BF16),
        compiler_params=_cparams(("arbitrary",)),
        name="modulate_in_proj_t",
    )(x_all, mod, mod, norm_g.reshape(1, D_MODEL), wt_bf)


Q_ROWS = 4
BAND_ROWS = Q_ROWS + MAX_KR


def _na_bias_table(rpb, rows):
    qc = np.arange(GRID_W)[:, None]
    kc = np.arange(GRID_W)[None, :]
    win0 = np.clip(qc - KC // 2, 0, GRID_W - KC)
    col_valid = (kc >= win0) & (kc < win0 + KC)
    col_off = np.clip(kc - qc, -(KC - 1), KC - 1) + KC - 1
    cb = rpb.astype(F32)[:, :, col_off]
    cb = jnp.where(jnp.asarray(col_valid)[None, None], cb, -jnp.inf)
    n_blk = rows // Q_ROWS
    idx = np.zeros((3, Q_ROWS, BAND_ROWS), np.int64)
    ok = np.zeros((3, Q_ROWS, BAND_ROWS), bool)
    for cls, blk in enumerate((0, 1, n_blk - 1)):
        band0 = int(np.clip(blk * Q_ROWS - MAX_KR // 2, 0, rows - BAND_ROWS))
        for rho in range(Q_ROWS):
            r = blk * Q_ROWS + rho
            r0 = int(np.clip(r - MAX_KR // 2, 0, rows - MAX_KR))
            for kap in range(BAND_ROWS):
                kr = band0 + kap
                ok[cls, rho, kap] = r0 <= kr < r0 + MAX_KR
                idx[cls, rho, kap] = np.clip(kr - r + MAX_KR - 1, 0, 2 * MAX_KR - 2)
    t = cb[:, idx]
    t = jnp.where(jnp.asarray(ok)[None, :, :, :, None, None], t, -jnp.inf)
    t = jnp.transpose(t, (1, 0, 2, 4, 3, 5))
    return t.reshape(3, NA_HEADS, Q_ROWS * GRID_W, BAND_ROWS * GRID_W)


def _attn_kernel(q_ref, k_ref, v_ref, qc_ref, kc_ref, vc_ref, bias_ref, o_ref, oc_ref, *, rows):
    lane = lax.broadcasted_iota(jnp.int32, (1, 2 * HEAD_DIM), 1)
    first = lane < HEAD_DIM
    kc = kc_ref[...]
    vc = vc_ref[...]
    zero = jnp.zeros((), BF16)

    def softmax_pv(s_list, v_list):
        m = s_list[0].max(axis=-1, keepdims=True)
        for s in s_list[1:]:
            m = jnp.maximum(m, s.max(axis=-1, keepdims=True))
        den = None
        acc = None
        for s, vv in zip(s_list, v_list):
            p = jnp.exp(s - m)
            ps = p.sum(axis=-1, keepdims=True)
            den = ps if den is None else den + ps
            pv = _dot(p.astype(BF16), vv)
            acc = pv if acc is None else acc + pv
        return acc / den

    n_blk = rows // Q_ROWS
    q_len = Q_ROWS * GRID_W

    def block_body(i, carry):
        band0 = jnp.clip(i * Q_ROWS - MAX_KR // 2, 0, rows - BAND_ROWS)
        cls = jnp.where(i == 0, 0, jnp.where(i == n_blk - 1, 2, 1))
        rows_q = pl.ds(pl.multiple_of(i * q_len, q_len), q_len)
        q_b = q_ref[rows_q, :]
        band = pl.ds(pl.multiple_of(band0 * GRID_W, GRID_W), BAND_ROWS * GRID_W)
        kb = k_ref[band, :]
        vb = v_ref[band, :]
        outs = []
        for hh in range(2):
            q_h = jnp.where(first if hh == 0 else ~first, q_b, zero)
            s_w = _dot_nt(q_h, kb) + bias_ref[cls, hh]
            s_c = _dot_nt(q_h, kc)
            outs.append(softmax_pv([s_w, s_c], [vb, vc]))
        o_ref[rows_q, :] = jnp.where(first, outs[0], outs[1]).astype(o_ref.dtype)
        return carry

    lax.fori_loop(0, n_blk, block_body, 0)

    q_c = qc_ref[...]
    outs = []
    for hh in range(2):
        q_h = jnp.where(first if hh == 0 else ~first, q_c, zero)
        outs.append(softmax_pv([_dot_nt(q_h, kc)], [vc]))
    oc_ref[...] = jnp.where(first, outs[0], outs[1]).astype(oc_ref.dtype)


def _attention(q, k, v, rpb, bsz, seq, ctx_len):
    rows = seq // GRID_W
    assert rows % Q_ROWS == 0 and rows >= 3 * Q_ROWS and seq % ctx_len == 0
    bias = _na_bias_table(rpb, rows)
    cb0 = bsz * seq // ctx_len
    lat = pl.BlockSpec((seq, 2 * HEAD_DIM), lambda hp, b: (b, hp))
    ctx = pl.BlockSpec((ctx_len, 2 * HEAD_DIM), lambda hp, b: (cb0 + b, hp))
    return pl.pallas_call(
        functools.partial(_attn_kernel, rows=rows),
        grid=(NA_HEADS // 2, bsz),
        in_specs=[lat, lat, lat, ctx, ctx, ctx,
                  pl.BlockSpec((3, 2, Q_ROWS * GRID_W, BAND_ROWS * GRID_W), lambda hp, b: (0, hp, 0, 0))],
        out_specs=[pl.BlockSpec((seq, 2 * HEAD_DIM), lambda hp, b: (b, hp)),
                   pl.BlockSpec((ctx_len, 2 * HEAD_DIM), lambda hp, b: (b, hp))],
        out_shape=[jax.ShapeDtypeStruct((bsz * seq, NA_WIDTH), BF16),
                   jax.ShapeDtypeStruct((bsz * ctx_len, NA_WIDTH), BF16)],
        compiler_params=_cparams(("arbitrary", "arbitrary")),
        name="na_attention",
    )(q, k, v, q, k, v, bias)


@functools.lru_cache(maxsize=None)
def _dft_tables(n):
    j = np.arange(n, dtype=np.int64)
    ang = 2.0 * np.pi * ((j[:, None] * j[None, :]) % n).astype(np.float64) / n
    return np.cos(ang).astype(BF16), np.sin(ang).astype(BF16)


def _fourier_kernel(cos_ref, sin_ref, f_ref, bdc_ref, bds_ref, bdw_ref, o_ref, *, scale):
    fb = f_ref[...]
    zr = _dot(cos_ref[...], fb)
    zs = _dot(sin_ref[...], fb)
    fr = (_dot(zr.astype(BF16), bdc_ref[...]) - _dot(zs.astype(BF16), bds_ref[...])) * scale
    o_ref[...] = _dot(fr.astype(BF16), bdw_ref[...]).astype(o_ref.dtype)


def _fourier(f_all, bdw, bsz, n, row_block0, tm):
    cos_n, sin_n = _dft_tables(n)
    cos64, sin64 = _dft_tables(FOURIER_DIM)
    eye = np.eye(FOURIER_GROUPS)
    bdc = jnp.asarray(np.kron(eye, cos64.astype(np.float32)), BF16)
    bds = jnp.asarray(np.kron(eye, sin64.astype(np.float32)), BF16)
    mt = n // tm
    small = pl.BlockSpec((FOURIER_WIDTH, FOURIER_WIDTH), lambda m, b: (0, 0))
    return pl.pallas_call(
        functools.partial(_fourier_kernel, scale=1.0 / math.sqrt(n * FOURIER_DIM)),
        grid=(mt, bsz),
        in_specs=[pl.BlockSpec((tm, n), lambda m, b: (m, 0)),
                  pl.BlockSpec((tm, n), lambda m, b: (m, 0)),
                  pl.BlockSpec((n, FOURIER_WIDTH), lambda m, b: (row_block0 + b, 0)),
                  small, small, small],
        out_specs=pl.BlockSpec((tm, FOURIER_WIDTH), lambda m, b: (b * mt + m, 0)),
        out_shape=jax.ShapeDtypeStruct((bsz * n, FOURIER_WIDTH), BF16),
        compiler_params=_cparams(("arbitrary", "arbitrary")),
        name="fourier_mix",
    )(jnp.asarray(cos_n), jnp.asarray(sin_n), f_all, bdc, bds, bdw)


def _route(h2, rw_hi_ref, rw_lo_ref, rb_ref):
    h_hi, h_lo = _split_bf16(h2)
    logits = (_dot(h_hi, rw_hi_ref[...]) + _dot(h_lo, rw_hi_ref[...]) + _dot(h_hi, rw_lo_ref[...])
              + rb_ref[...])
    lane = lax.broadcasted_iota(jnp.int32, logits.shape, 1).astype(F32)
    work = logits
    sel_any = jnp.zeros(logits.shape, jnp.bool_)
    top = None
    for _ in range(TOP_K):
        m = work.max(axis=-1, keepdims=True)
        if top is None:
            top = m
        idx = jnp.where(work == m, lane, float(N_EXPERTS)).min(axis=-1, keepdims=True)
        sel = lane == idx
        sel_any = sel_any | sel
        work = jnp.where(sel, -jnp.inf, work)
    e = jnp.where(sel_any, jnp.exp(logits - top), 0.0)
    return sel_any.astype(F32), e / e.sum(axis=-1, keepdims=True)


def _post_l0_kernel(a_ref, ac_ref, f_ref, fc_ref, x_ref, c_ref, g1_ref, sh_ref, sc_ref, ng_ref, w_ref,
                    rwh_ref, rwl_ref, rb_ref, xo_ref, hs_ref, pos_ref, gate_ref, cnt_ref, *, n_lat_tiles):
    lat = pl.program_id(0) < n_lat_tiles
    a = jnp.where(lat, a_ref[...], ac_ref[...])
    f = jnp.where(lat, f_ref[...], fc_ref[...])
    x = jnp.where(lat, x_ref[...], c_ref[...])
    y = _dot(a, w_ref[:NA_WIDTH, :]) + _dot(f, w_ref[NA_WIDTH:, :])
    x = x + g1_ref[0] * y
    xo_ref[...] = x
    h2 = _modulate(x, ng_ref[...], sh_ref[0], sc_ref[0])
    _sort_tile(h2, *_route(h2, rwh_ref, rwl_ref, rb_ref), hs_ref, pos_ref, gate_ref, cnt_ref)


def _post_l1_kernel(yt_ref, x_ref, g1_ref, sh_ref, sc_ref, ng_ref, wgt_ref, bg_ref, wot_ref,
                    rwh_ref, rwl_ref, rb_ref, xo_ref, hs_ref, pos_ref, gate_ref, cnt_ref):
    gyt = yt_ref[...]
    zt = _dot(wgt_ref[...], gyt) + bg_ref[...]
    vt = (gyt.astype(F32) * jax.nn.sigmoid(zt)).astype(BF16)
    x = x_ref[...] + g1_ref[0] * lax.dot_general(vt, wot_ref[...], (((0,), (1,)), ((), ())),
                                                  preferred_element_type=F32)
    xo_ref[...] = x
    h2 = _modulate(x, ng_ref[...], sh_ref[0], sc_ref[0])
    _sort_tile(h2, *_route(h2, rwh_ref, rwl_ref, rb_ref), hs_ref, pos_ref, gate_ref, cnt_ref)


def _router_args(router_w, router_b):
    rw_hi = router_w.astype(BF16)
    rw_lo = (router_w - rw_hi.astype(F32)).astype(BF16)
    return rw_hi, rw_lo, router_b.reshape(1, N_EXPERTS).astype(F32)


def _post_out(n_tok):
    n_tiles = n_tok // TM
    wide = pl.BlockSpec((TM, 128), lambda i: (i, 0))
    specs = [pl.BlockSpec((TM, D_MODEL), lambda i: (i, 0)),
             pl.BlockSpec((TILE_ROWS, HALF), lambda i: (i, 0)), wide, wide,
             pl.BlockSpec((8, 128), lambda i: (i, 0))]
    shapes = [jax.ShapeDtypeStruct((n_tok, D_MODEL), F32),
              jax.ShapeDtypeStruct((n_tiles * TILE_ROWS, HALF), jnp.uint32),
              jax.ShapeDtypeStruct((n_tok, 128), F32),
              jax.ShapeDtypeStruct((n_tok, 128), F32),
              jax.ShapeDtypeStruct((n_tiles * 8, 128), F32)]
    return specs, shapes


def _const_spec(shape):
    return pl.BlockSpec(shape, lambda i: tuple(0 for _ in shape))


def _post_l0(a_lat, a_ctx, f_lat, f_ctx, x, ctx, mod, norm_g, w_out_bf, router, n_lat, n_tok, tiles_per_batch):
    n_lat_tiles = n_lat // TM

    def lat(wd):
        return pl.BlockSpec((TM, wd), lambda i: (jnp.minimum(i, n_lat_tiles - 1), 0))

    def cx(wd):
        return pl.BlockSpec((TM, wd), lambda i: (jnp.maximum(i - n_lat_tiles, 0), 0))

    out_specs, out_shapes = _post_out(n_tok)
    return pl.pallas_call(
        functools.partial(_post_l0_kernel, n_lat_tiles=n_lat_tiles),
        grid=(n_tok // TM,),
        in_specs=[lat(NA_WIDTH), cx(NA_WIDTH), lat(FOURIER_WIDTH), cx(FOURIER_WIDTH),
                  lat(D_MODEL), cx(D_MODEL),
                  _mod_spec(2, tiles_per_batch), _mod_spec(3, tiles_per_batch), _mod_spec(4, tiles_per_batch),
                  _const_spec((1, D_MODEL)), _const_spec((D_MODEL, D_MODEL)),
                  _const_spec((D_MODEL, N_EXPERTS)), _const_spec((D_MODEL, N_EXPERTS)),
                  _const_spec((1, N_EXPERTS))],
        out_specs=out_specs, out_shape=out_shapes,
        compiler_params=_cparams(("arbitrary",)),
        name="post_mixer_l0",
    )(a_lat, a_ctx, f_lat, f_ctx, x, ctx, mod, mod, mod, norm_g.reshape(1, D_MODEL), w_out_bf, *router)


def _post_l1(gyt, x_all, mod, norm_g, w_glu_t_bf, b_glu, w_out_t_bf, router, n_lat, tiles_per_batch):
    out_specs, out_shapes = _post_out(n_lat)
    return pl.pallas_call(
        _post_l1_kernel,
        grid=(n_lat // TM,),
        in_specs=[pl.BlockSpec((D_MODEL, TM), lambda i: (0, i)),
                  pl.BlockSpec((TM, D_MODEL), lambda i: (i, 0)),
                  _mod_spec(2, tiles_per_batch), _mod_spec(3, tiles_per_batch), _mod_spec(4, tiles_per_batch),
                  _const_spec((1, D_MODEL)), _const_spec((D_MODEL, D_MODEL)), _const_spec((D_MODEL, 1)),
                  _const_spec((D_MODEL, D_MODEL)),
                  _const_spec((D_MODEL, N_EXPERTS)), _const_spec((D_MODEL, N_EXPERTS)),
                  _const_spec((1, N_EXPERTS))],
        out_specs=out_specs, out_shape=out_shapes,
        compiler_params=_cparams(("arbitrary",)),
        name="post_mixer_l1",
    )(gyt, x_all, mod, mod, mod, norm_g.reshape(1, D_MODEL), w_glu_t_bf, b_glu.reshape(D_MODEL, 1),
      w_out_t_bf, *router)


CHUNK = 8
TILE_ROWS = TM * TOP_K + N_EXPERTS * CHUNK
TILE_CHUNKS = TILE_ROWS // CHUNK
BLOCK_CHUNKS = EXPERT_ROWS // CHUNK
SORT_ROWS = 256
ISSUE_UNROLL = 8
FF_CHUNK = 256
HALF = D_MODEL // 2


def _pack_bf16_pairs(y):
    bits = pltpu.bitcast(y, jnp.uint32)
    return (bits[:, :HALF] & jnp.uint32(0xFFFF0000)) | (bits[:, HALF:] >> 16)


def _unpack_bf16_pairs(w):
    hi = pltpu.bitcast(w & jnp.uint32(0xFFFF0000), F32).astype(BF16)
    lo = pltpu.bitcast(w << 16, F32).astype(BF16)
    return hi, lo


def _sort_tile(h2, mf, g, hs_ref, pos_ref, gate_ref, cnt_ref):
    m = mf.astype(BF16)
    r = lax.broadcasted_iota(jnp.int32, (TM, TM), 0)
    c = lax.broadcasted_iota(jnp.int32, (TM, TM), 1)
    rank = _dot((r > c).astype(F32).astype(BF16), m)
    cnt = mf.sum(axis=0, keepdims=True)
    n8 = jnp.floor((cnt + (CHUNK - 1)) * (1.0 / CHUNK))
    er = lax.broadcasted_iota(jnp.int32, (N_EXPERTS, N_EXPERTS), 0)
    ec = lax.broadcasted_iota(jnp.int32, (N_EXPERTS, N_EXPERTS), 1)
    upper = (er < ec).astype(F32).astype(BF16)
    off = CHUNK * _dot(jnp.broadcast_to(n8, (8, N_EXPERTS)).astype(BF16), upper)[0:1]
    pos = rank + off
    slot = _dot(m, upper)
    chosen = mf > 0.5
    lane = lax.broadcasted_iota(jnp.int32, pos_ref.shape, 1)
    p_out = jnp.zeros(pos_ref.shape, F32)
    g_out = jnp.zeros(gate_ref.shape, F32)
    for k in range(TOP_K):
        sel = chosen & (slot == float(k))
        pk = jnp.where(sel, pos, 0.0).sum(axis=-1, keepdims=True)
        gk = jnp.where(sel, g, 0.0).sum(axis=-1, keepdims=True)
        p_out = jnp.where(lane == k, pk, p_out)
        g_out = jnp.where(lane == k, gk, g_out)
    pos_ref[...] = p_out
    gate_ref[...] = g_out
    cnt_ref[...] = jnp.zeros_like(cnt_ref)
    cnt_ref[0:1, 0:N_EXPERTS] = cnt

    pos_t = p_out.T
    hb = h2.astype(BF16)
    rid = lax.broadcasted_iota(jnp.int32, (SORT_ROWS, TM), 0).astype(F32).astype(BF16)
    one, nil = jnp.ones((), BF16), jnp.zeros((), BF16)
    for rc in range(TILE_ROWS // SORT_ROWS):
        rel = (pos_t[0:16, :] - float(rc * SORT_ROWS)).astype(BF16)
        onehot = jnp.where(rid == rel[0:1, :], one, nil)
        for k in range(1, TOP_K):
            onehot = jnp.where(rid == rel[k:k + 1, :], one, onehot)
        hs_ref[rc * SORT_ROWS:(rc + 1) * SORT_ROWS, :] = _pack_bf16_pairs(_dot(onehot, hb))


def _chunk_tables(cnt, n_blocks):
    n_tiles = cnt.shape[0]
    n8 = (cnt + CHUNK - 1) // CHUNK
    lend = jnp.cumsum(n8, axis=1)
    lstart = lend - n8
    cum = jnp.cumsum(n8, axis=0)
    before = cum - n8
    total = cum[-1]
    nblk = (total + BLOCK_CHUNKS - 1) // BLOCK_CHUNKS
    bend = jnp.cumsum(nblk)
    gstart = (bend - nblk) * BLOCK_CHUNKS
    n_used = jnp.maximum(bend[-1], 1)
    blk = jnp.arange(n_blocks, dtype=jnp.int32)
    block_expert = (bend[None, :] <= jnp.minimum(blk, n_used - 1)[:, None]).sum(axis=1).astype(jnp.int32)
    block_expert = jnp.minimum(block_expert, N_EXPERTS - 1)
    experts = jnp.arange(N_EXPERTS, dtype=jnp.int32)
    lc = jnp.arange(TILE_CHUNKS, dtype=jnp.int32)
    e_of = (lend[:, None, :] <= lc[None, :, None]).sum(axis=-1).astype(jnp.int32)
    shift = gstart[None, :] + before - lstart
    pick = e_of[:, :, None] == experts[None, None, :]
    to_sorted = jnp.where(pick, shift[:, None, :], 0).sum(axis=-1) + jnp.where(e_of < N_EXPERTS, lc[None, :], 0)
    of_block = block_expert[:, None] == experts[None, :]
    gstart_b = jnp.where(of_block, gstart[None, :], 0).sum(axis=-1)
    total_b = jnp.where(of_block, total[None, :], 0).sum(axis=-1)

    def per_block(table):
        return jnp.where(of_block[:, :, None], table.T[None, :, :], 0).sum(axis=1)

    cum_b, lstart_b, before_b = per_block(cum), per_block(lstart), per_block(before)
    rel = (blk[:, None] * BLOCK_CHUNKS + jnp.arange(BLOCK_CHUNKS, dtype=jnp.int32)[None, :]) - gstart_b[:, None]
    valid = (blk[:, None] < n_used) & (rel < total_b[:, None])
    tile = jnp.minimum((cum_b[:, None, :] <= rel[:, :, None]).sum(axis=-1), n_tiles - 1).astype(jnp.int32)
    of_tile = tile[:, :, None] == jnp.arange(n_tiles, dtype=jnp.int32)[None, None, :]
    inside = jnp.where(of_tile, (lstart_b - before_b)[:, None, :], 0).sum(axis=-1)
    from_tiles = jnp.where(valid, tile * TILE_CHUNKS + inside + rel, TILE_CHUNKS - 1)
    return (block_expert, n_used.reshape(1).astype(jnp.int32), from_tiles.reshape(-1).astype(jnp.int32),
            to_sorted.reshape(-1).astype(jnp.int32))


def _chunk_rows(chunk):
    start = chunk * CHUNK
    return pl.ds(start if isinstance(start, int) else pl.multiple_of(start, CHUNK), CHUNK)


def _chunk_copy(src_ref, src_chunk, dst_ref, dst_chunk, sem):
    return pltpu.make_async_copy(src_ref.at[_chunk_rows(src_chunk), :], dst_ref.at[_chunk_rows(dst_chunk), :], sem)


def _expert_kernel(be_ref, nu_ref, src_ref, hs_ref, wgu_ref, bgu_ref, wd_ref, bd_ref, o_ref,
                   wgu_bf, wd_bf, x_buf, sems):
    i = pl.program_id(0)
    n_used = nu_ref[0]

    def gather(block, slot, start):
        if not start:
            pltpu.make_async_copy(hs_ref.at[pl.ds(0, EXPERT_ROWS), :], x_buf.at[slot], sems.at[slot]).wait()
            return
        for cidx in range(BLOCK_CHUNKS):
            _chunk_copy(hs_ref, src_ref[block * BLOCK_CHUNKS + cidx], x_buf.at[slot], cidx, sems.at[slot]).start()

    @pl.when(i == 0)
    def _():
        gather(0, 0, True)

    for slot in range(2):
        @pl.when((i + 1 < n_used) & ((i + 1) % 2 == slot))
        def _(slot=slot):
            gather(i + 1, slot, True)

    @pl.when((i == 0) | (be_ref[i] != be_ref[jnp.maximum(i - 1, 0)]))
    def _():
        wgu_bf[...] = wgu_ref[0].astype(BF16)
        wd_bf[...] = wd_ref[0].astype(BF16)

    for slot in range(2):
        @pl.when((i < n_used) & (i % 2 == slot))
        def _(slot=slot):
            gather(i, slot, False)
            x_hi, x_lo = _unpack_bf16_pairs(x_buf[slot])
            y = None
            for j in range(D_MODEL // FF_CHUNK):
                def proj(col0):
                    cols = slice(col0 + j * FF_CHUNK, col0 + (j + 1) * FF_CHUNK)
                    return (_dot(x_hi, wgu_bf[:HALF, cols]) + _dot(x_lo, wgu_bf[HALF:, cols])
                            + bgu_ref[0, :, cols])
                gate = jnp.minimum(proj(0), SWIGLU_LIMIT)
                up = jnp.clip(proj(D_MODEL), -SWIGLU_LIMIT, SWIGLU_LIMIT)
                act = (up + 1.0) * (gate * jax.nn.sigmoid(SWIGLU_ALPHA * gate))
                part = _dot(act.astype(BF16), wd_bf[j * FF_CHUNK:(j + 1) * FF_CHUNK, :])
                y = part if y is None else y + part
            y = y + bd_ref[0]
            o_ref[...] = _pack_bf16_pairs(y.astype(BF16).astype(F32))

    @pl.when(i >= n_used)
    def _():
        o_ref[...] = jnp.zeros_like(o_ref)


def _experts(hs, block_expert, n_used, from_tiles, wgu, bgu, wd, bd):
    n_blocks = block_expert.shape[0]

    def row_map(i, be, nu, src):
        return (i, 0)

    def exp_map(i, be, nu, src):
        return (be[i], 0, 0)

    grid_spec = pltpu.PrefetchScalarGridSpec(
        num_scalar_prefetch=3,
        grid=(n_blocks,),
        in_specs=[pl.BlockSpec(memory_space=pl.ANY),
                  pl.BlockSpec((1, D_MODEL, 2 * D_MODEL), exp_map),
                  pl.BlockSpec((1, 1, 2 * D_MODEL), exp_map),
                  pl.BlockSpec((1, D_MODEL, D_MODEL), exp_map),
                  pl.BlockSpec((1, 1, D_MODEL), exp_map)],
        out_specs=pl.BlockSpec((EXPERT_ROWS, HALF), row_map),
        scratch_shapes=[pltpu.VMEM((D_MODEL, 2 * D_MODEL), BF16), pltpu.VMEM((D_MODEL, D_MODEL), BF16),
                        pltpu.VMEM((2, EXPERT_ROWS, HALF), jnp.uint32), pltpu.SemaphoreType.DMA((2,))],
    )
    return pl.pallas_call(
        _expert_kernel,
        grid_spec=grid_spec,
        out_shape=jax.ShapeDtypeStruct((n_blocks * EXPERT_ROWS, HALF), jnp.uint32),
        compiler_params=pltpu.CompilerParams(dimension_semantics=("arbitrary",),
                                             vmem_limit_bytes=EXPERT_VMEM_LIMIT),
        name="moe_experts",
    )(block_expert, n_used, from_tiles, hs, wgu, bgu.reshape(N_EXPERTS, 1, 2 * D_MODEL), wd,
      bd.reshape(N_EXPERTS, 1, D_MODEL))


def _combine_kernel(dst_ref, ys_ref, pos_ref, gate_ref, x_ref, g2_ref, ng_ref, o_ref, y_buf, sems, *, final_norm):
    i = pl.program_id(0)
    n_tiles = pl.num_programs(0)

    def gather(tile, slot, start):
        if not start:
            pltpu.make_async_copy(ys_ref.at[pl.ds(0, TILE_ROWS), :], y_buf.at[slot], sems.at[slot]).wait()
            return

        def body(g, carry):
            for j in range(ISSUE_UNROLL):
                cidx = g * ISSUE_UNROLL + j
                _chunk_copy(ys_ref, dst_ref[tile * TILE_CHUNKS + cidx], y_buf.at[slot], cidx, sems.at[slot]).start()
            return carry
        lax.fori_loop(0, TILE_CHUNKS // ISSUE_UNROLL, body, 0)

    @pl.when(i == 0)
    def _():
        gather(0, 0, True)

    for slot in range(2):
        @pl.when((i + 1 < n_tiles) & ((i + 1) % 2 == slot))
        def _(slot=slot):
            gather(i + 1, slot, True)

    pos = pos_ref[...]
    gate = gate_ref[...]
    for slot in range(2):
        @pl.when(i % 2 == slot)
        def _(slot=slot):
            gather(i, slot, False)
            acc_hi = jnp.zeros((TM, HALF), F32)
            acc_lo = jnp.zeros((TM, HALF), F32)
            rid = lax.broadcasted_iota(jnp.int32, (TM, SORT_ROWS), 1).astype(F32).astype(BF16)
            gate_b = gate.astype(BF16)
            for rc in range(TILE_ROWS // SORT_ROWS):
                rel = (pos - float(rc * SORT_ROWS)).astype(BF16)
                w = jnp.where(rid == rel[:, 0:1], gate_b[:, 0:1], jnp.zeros((), BF16))
                for k in range(1, TOP_K):
                    w = jnp.where(rid == rel[:, k:k + 1], gate_b[:, k:k + 1], w)
                y_hi, y_lo = _unpack_bf16_pairs(y_buf[slot, rc * SORT_ROWS:(rc + 1) * SORT_ROWS, :])
                acc_hi = acc_hi + _dot(w, y_hi)
                acc_lo = acc_lo + _dot(w, y_lo)
            x = x_ref[...] + g2_ref[0] * jnp.concatenate([acc_hi, acc_lo], axis=-1)
            if final_norm:
                ms = jnp.mean(x * x, axis=-1, keepdims=True)
                x = x * lax.rsqrt(ms + EPS) * ng_ref[...]
            o_ref[...] = x


def _combine(ys, to_sorted, pos4, gate4, x_res, mod, norm_g, tiles_per_batch, final_norm):
    n_tok = x_res.shape[0]
    tok = pl.BlockSpec((TM, D_MODEL), lambda i, dst: (i, 0))
    wide = pl.BlockSpec((TM, 128), lambda i, dst: (i, 0))
    tiles, bsz = tiles_per_batch
    grid_spec = pltpu.PrefetchScalarGridSpec(
        num_scalar_prefetch=1,
        grid=(n_tok // TM,),
        in_specs=[pl.BlockSpec(memory_space=pl.ANY), wide, wide, tok,
                  pl.BlockSpec((1, 1, D_MODEL), lambda i, dst: (jnp.minimum(i // tiles, bsz) * 6 + 5, 0, 0)),
                  pl.BlockSpec((1, D_MODEL), lambda i, dst: (0, 0))],
        out_specs=tok,
        scratch_shapes=[pltpu.VMEM((2, TILE_ROWS, HALF), jnp.uint32), pltpu.SemaphoreType.DMA((2,))],
    )
    return pl.pallas_call(
        functools.partial(_combine_kernel, final_norm=final_norm),
        grid_spec=grid_spec,
        out_shape=jax.ShapeDtypeStruct((n_tok, D_MODEL), F32),
        compiler_params=_cparams(("arbitrary",)),
        name="moe_combine",
    )(to_sorted, ys, pos4, gate4, x_res, mod, norm_g.reshape(1, D_MODEL))


def _moe(hs, pos4, gate4, cnt, x_res, mod, moe_w, tiles_per_batch, norm_g, final_norm):
    wgu, bgu, wd, bd = moe_w
    n_tiles = x_res.shape[0] // TM
    n_blocks = -(-n_tiles * TILE_CHUNKS // BLOCK_CHUNKS) + N_EXPERTS
    cnt_i = cnt.reshape(n_tiles, 8, 128)[:, 0, :N_EXPERTS].astype(jnp.int32)
    block_expert, n_used, from_tiles, to_sorted = _chunk_tables(cnt_i, n_blocks)
    ys = _experts(hs, block_expert, n_used, from_tiles, wgu, bgu, wd, bd)
    return _combine(ys, to_sorted, pos4, gate4, x_res, mod, norm_g, tiles_per_batch, final_norm)


def _s5_tables(lam_re, lam_im, log_dt, b_re, b_im, c_re, c_im, d_skip):
    q = S5_Q
    lam = lax.complex(jnp.minimum(lam_re.astype(F32), LAMBDA_RE_MAX), lam_im.astype(F32))
    dt = jnp.exp(log_dt.astype(F32))[..., None]
    a = jnp.exp(lam * dt)
    bbar = ((a - 1) / lam)[..., None] * lax.complex(b_re.astype(F32), b_im.astype(F32))
    cc = lax.complex(c_re.astype(F32), c_im.astype(F32))
    steps = jnp.arange(q + 1, dtype=F32)
    pw = jnp.exp((lam * dt)[:, :, None, :] * steps[None, None, :, None])
    kern = jnp.einsum('dgop,dgkp,dgpc->dgcok', cc, pw[:, :, :q], bbar).real
    eye = jnp.eye(SSM_GROUP, dtype=F32)
    k_fwd = kern[0].at[..., 0].add(kern[1][..., 0] + d_skip.astype(F32)[:, :, None] * eye[None])
    k_bwd = jnp.concatenate([jnp.zeros_like(kern[1][..., :1]), kern[1][..., :0:-1]], axis=-1)
    k_fwd = k_fwd.reshape(SSM_GROUPS, SSM_GROUP * SSM_GROUP, q)
    k_bwd = k_bwd.reshape(SSM_GROUPS, SSM_GROUP * SSM_GROUP, q)
    def both(f, b):
        return jnp.concatenate([f, b], axis=-1)

    e_in = both(pw[0][:, ::-1][:, 1:], pw[1][:, :q])
    b_in = both(jnp.transpose(bbar[0], (0, 2, 1)), jnp.transpose(bbar[1], (0, 2, 1)))
    e_out = jnp.transpose(both(pw[0][:, 1:], pw[1][:, ::-1][:, :q]), (0, 2, 1))
    c_out = jnp.transpose(both(cc[0], cc[1]), (0, 2, 1))
    a_q = both(pw[0][:, q], pw[1][:, q]).reshape(SSM_GROUPS, 1, 2 * SSM_STATE)
    parts = [t for z in (e_in, b_in, e_out, c_out, a_q) for t in (z.real, z.imag)]
    return [k_fwd, k_bwd] + parts


def _s5_kernel(u_ref, kf_ref, kb_ref, er_ref, ei_ref, br_ref, bi_ref, pr_ref, pi_ref, cr_ref, ci_ref,
               ar_ref, ai_ref, o_ref, w_ref, s_ref, h_ref, *, bsz, n_lat_chunks, n_ctx_chunks):
    q = S5_Q
    half = 2 * SSM_STATE
    n_lat = bsz * n_lat_chunks

    causal = (lax.broadcasted_iota(jnp.int32, (q, q), 1) >= lax.broadcasted_iota(jnp.int32, (q, q), 0))

    def build(c, carry):
        for o in range(SSM_GROUP):
            row = pl.ds(c * SSM_GROUP + o, 1)
            kf = pltpu.roll(jnp.broadcast_to(kf_ref[0, row, :], (q, q)), 0, 1, stride=1, stride_axis=0)
            kb = pltpu.roll(jnp.broadcast_to(kb_ref[0, row, :], (q, q)), 0, 1, stride=1, stride_axis=0)
            w_ref[pl.ds(pl.multiple_of(c * q, q), q), o * q:(o + 1) * q] = jnp.where(causal, kf, kb).astype(BF16)
        return carry

    lax.fori_loop(0, SSM_GROUP, build, 0)

    u = u_ref[...]
    er, ei = er_ref[0], ei_ref[0]
    s_re = None
    for c in range(SSM_GROUP):
        br, bi = br_ref[0, c:c + 1, :], bi_ref[0, c:c + 1, :]
        w_re = (er * br - ei * bi).astype(BF16)
        w_im = (er * bi + ei * br).astype(BF16)
        s_re = _dot(u[c], w_re) if s_re is None else s_re + _dot(u[c], w_re)
        s_im = _dot(u[c], w_im) if c == 0 else s_im + _dot(u[c], w_im)
    s_ref[0] = s_re
    s_ref[1] = s_im

    fwd = lax.broadcasted_iota(jnp.int32, (bsz, half), 1) < SSM_STATE
    ar = jnp.broadcast_to(ar_ref[0], (bsz, half))
    ai = jnp.broadcast_to(ai_ref[0], (bsz, half))

    def chunk_rows(base, j, per_sample):
        return pl.ds(base + j, bsz, stride=per_sample)

    def load(base, j, n):
        rows_f = chunk_rows(base, j, n)
        rows_b = chunk_rows(base, n - 1 - j, n)
        return (jnp.where(fwd, s_ref[0, rows_f, :], s_ref[0, rows_b, :]),
                jnp.where(fwd, s_ref[1, rows_f, :], s_ref[1, rows_b, :]))

    def step(hr, hi, sr, si):
        return ar * hr - ai * hi + sr, ar * hi + ai * hr + si

    def ctx_body(j, carry):
        return step(*carry, *load(n_lat, j, n_ctx_chunks))

    def lat_body(j, carry):
        hr, hi = carry
        rows_f = chunk_rows(0, j, n_lat_chunks)
        rows_b = chunk_rows(0, n_lat_chunks - 1 - j, n_lat_chunks)
        h_ref[0, rows_f, :] = jnp.where(fwd, hr, 0.0)
        h_ref[1, rows_f, :] = jnp.where(fwd, hi, 0.0)
        h_ref[2, rows_b, :] = jnp.where(fwd, 0.0, hr)
        h_ref[3, rows_b, :] = jnp.where(fwd, 0.0, hi)
        return step(hr, hi, *load(0, j, n_lat_chunks))

    zero = jnp.zeros((bsz, half), F32)
    carry = lax.fori_loop(0, n_ctx_chunks, ctx_body, (zero, zero))
    lax.fori_loop(0, n_lat_chunks, lat_body, carry)

    ucat = jnp.concatenate([u[c][:n_lat] for c in range(SSM_GROUP)], axis=-1)
    y = _dot(ucat, w_ref[...])
    h_re = (h_ref[0] + h_ref[2]).astype(BF16)
    h_im = (h_ref[1] + h_ref[3]).astype(BF16)
    pr, pi = pr_ref[0], pi_ref[0]
    cr_all, ci_all = cr_ref[0], ci_ref[0]
    for o in range(SSM_GROUP):
        cr, ci = cr_all[:, o:o + 1], ci_all[:, o:o + 1]
        w_re = (pr * cr - pi * ci).astype(BF16)
        w_im = (pr * ci + pi * cr).astype(BF16)
        y_o = y[:, o * q:(o + 1) * q] + _dot(h_re, w_re) - _dot(h_im, w_im)
        o_ref[o] = jax.nn.gelu(y_o).astype(o_ref.dtype)


def _s5(u3, tables, bsz, n_lat_chunks, n_ctx_chunks):
    q = S5_Q
    n_chunks = u3.shape[1]
    n_lat = bsz * n_lat_chunks
    assert n_chunks == n_lat + bsz * n_ctx_chunks

    def gspec(*shape):
        return pl.BlockSpec((1,) + shape, lambda g: (g,) + tuple(0 for _ in shape))

    return pl.pallas_call(
        functools.partial(_s5_kernel, bsz=bsz, n_lat_chunks=n_lat_chunks, n_ctx_chunks=n_ctx_chunks),
        grid=(SSM_GROUPS,),
        in_specs=[pl.BlockSpec((SSM_GROUP, n_chunks, q), lambda g: (g, 0, 0)),
                  gspec(SSM_GROUP * SSM_GROUP, q), gspec(SSM_GROUP * SSM_GROUP, q),
                  gspec(q, 2 * SSM_STATE), gspec(q, 2 * SSM_STATE),
                  gspec(SSM_GROUP, 2 * SSM_STATE), gspec(SSM_GROUP, 2 * SSM_STATE),
                  gspec(2 * SSM_STATE, q), gspec(2 * SSM_STATE, q),
                  gspec(2 * SSM_STATE, SSM_GROUP), gspec(2 * SSM_STATE, SSM_GROUP),
                  gspec(1, 2 * SSM_STATE), gspec(1, 2 * SSM_STATE)],
        out_specs=pl.BlockSpec((SSM_GROUP, n_lat, q), lambda g: (g, 0, 0)),
        out_shape=jax.ShapeDtypeStruct((SSM_GROUPS * SSM_GROUP, n_lat, q), BF16),
        scratch_shapes=[pltpu.VMEM((SSM_GROUP * q, SSM_GROUP * q), BF16),
                        pltpu.VMEM((2, n_chunks, 2 * SSM_STATE), F32),
                        pltpu.VMEM((4, n_lat, 2 * SSM_STATE), F32)],
        compiler_params=_cparams(("arbitrary",)),
        name="s5_scan",
    )(u3, *tables)


def _moe_weights(w_gate_up, b_gate_up, w_down, b_down):
    return w_gate_up, b_gate_up, w_down, b_down


def kernel(x, c, ctx, c_ctx, l0_ada_w, l0_ada_b, l0_norm_mix, l0_w_in, l0_rpb, l0_w_fourier, l0_w_out, l0_norm_ffn, l0_router_w, l0_router_b, l0_w_gate_up, l0_b_gate_up, l0_w_down, l0_b_down, l1_ada_w, l1_ada_b, l1_norm_mix, l1_w_in, l1_lambda_re, l1_lambda_im, l1_log_dt, l1_b_re, l1_b_im, l1_c_re, l1_c_im, l1_d_skip, l1_w_glu, l1_b_glu, l1_w_out, l1_norm_ffn, l1_router_w, l1_router_b, l1_w_gate_up, l1_b_gate_up, l1_w_down, l1_b_down, final_norm):
    bsz, seq, _ = x.shape
    ctx_len = ctx.shape[1]
    n_lat = bsz * seq
    n_ctx = bsz * ctx_len
    n_tok = n_lat + n_ctx
    assert seq % TM == 0 and n_ctx % TM == 0
    x2 = x.reshape(n_lat, D_MODEL)
    ctx2 = ctx.reshape(n_ctx, D_MODEL)

    n_mod = bsz + 1
    c_pad = jnp.zeros((-(-n_mod // 8) * 8, D_MODEL), F32).at[:bsz].set(c).at[bsz].set(c_ctx)
    mod0 = _ada(c_pad, l0_ada_w, l0_ada_b)[:n_mod].reshape(n_mod * 6, 1, D_MODEL)
    mod1 = _ada(c_pad, l1_ada_w, l1_ada_b)[:n_mod].reshape(n_mod * 6, 1, D_MODEL)
    tpb = (seq // TM, bsz)

    q, k, v, f = _modproj([x2, ctx2], mod0, l0_norm_mix, l0_w_in.astype(BF16),
                          (NA_WIDTH, NA_WIDTH, NA_WIDTH, FOURIER_WIDTH), n_lat, n_tok, tpb,
                          q_scale=HEAD_DIM ** -0.5)
    a_lat, a_ctx = _attention(q, k, v, l0_rpb, bsz, seq, ctx_len)
    bdw = (jnp.eye(FOURIER_GROUPS, dtype=F32)[:, None, :, None] * l0_w_fourier[:, :, None, :]).reshape(
        FOURIER_WIDTH, FOURIER_WIDTH).astype(BF16)
    f_lat = _fourier(f, bdw, bsz, seq, 0, TM)
    f_ctx = _fourier(f, bdw, bsz, ctx_len, n_lat // ctx_len, ctx_len)
    x_all, *routed = _post_l0(a_lat, a_ctx, f_lat, f_ctx, x2, ctx2, mod0, l0_norm_ffn,
                              l0_w_out.astype(BF16), _router_args(l0_router_w, l0_router_b), n_lat, n_tok, tpb)
    x_all = _moe(*routed, x_all, mod0, _moe_weights(l0_w_gate_up, l0_b_gate_up, l0_w_down, l0_b_down),
                 tpb, final_norm, False)

    assert seq % S5_Q == 0 and ctx_len % S5_Q == 0
    ut = _modproj_t(x_all, mod1, l1_norm_mix, l1_w_in.T.astype(BF16), n_tok, tpb)
    tables = _s5_tables(l1_lambda_re, l1_lambda_im, l1_log_dt, l1_b_re, l1_b_im, l1_c_re, l1_c_im, l1_d_skip)
    yt = _s5(ut.reshape(D_MODEL, n_tok // S5_Q, S5_Q), tables, bsz, seq // S5_Q, ctx_len // S5_Q)
    gy = yt.reshape(D_MODEL, n_lat)
    x1, *routed = _post_l1(gy, x_all, mod1, l1_norm_ffn, l1_w_glu.T.astype(BF16), l1_b_glu,
                           l1_w_out.T.astype(BF16), _router_args(l1_router_w, l1_router_b), n_lat, tpb)
    out = _moe(*routed, x1, mod1, _moe_weights(l1_w_gate_up, l1_b_gate_up, l1_w_down, l1_b_down),
               tpb, final_norm, True)
    return out.reshape(bsz, seq, D_MODEL)
```

```python
import functools
import math

import numpy as np
import jax
import jax.numpy as jnp
from jax import lax
from jax.experimental import pallas as pl
from jax.experimental.pallas import tpu as pltpu

F32 = jnp.float32
BF16 = jnp.bfloat16

D_MODEL = 1024
GRID_W = 64
HEAD_DIM = 64
NA_WIDTH = 512
NA_HEADS = 8
MAX_KR = 8
KC = 16
FOURIER_DIM = 64
FOURIER_GROUPS = 8
FOURIER_WIDTH = 512
SSM_GROUP = 16
SSM_GROUPS = 64
SSM_STATE = 64
LAMBDA_RE_MAX = -1e-4
N_EXPERTS = 32
TOP_K = 4
SWIGLU_ALPHA = 1.702
SWIGLU_LIMIT = 7.0
EPS = 1e-6

TM = 512
EXPERT_ROWS = 512
S5_Q = 128
VMEM_LIMIT = 48 * 1024 * 1024
EXPERT_VMEM_LIMIT = 58 * 1024 * 1024


def _cparams(sem):
    return pltpu.CompilerParams(dimension_semantics=sem, vmem_limit_bytes=VMEM_LIMIT)


def _split_bf16(a):
    hi = a.astype(BF16)
    lo = (a - hi.astype(F32)).astype(BF16)
    return hi, lo


def _dot(a, b):
    return jnp.dot(a, b, preferred_element_type=F32)


def _dot_nt(a, b):
    return lax.dot_general(a, b, (((1,), (1,)), ((), ())), preferred_element_type=F32)


def _ada_kernel(c_ref, w_ref, b_ref, o_ref):
    c = c_ref[...]
    s = c * jax.nn.sigmoid(c)
    s_hi, s_lo = _split_bf16(s)
    w_hi, w_lo = _split_bf16(w_ref[...])
    o_ref[...] = _dot(s_hi, w_hi) + _dot(s_lo, w_hi) + _dot(s_hi, w_lo) + b_ref[...]


def _ada(c_pad, ada_w, ada_b):
    n = ada_w.shape[1]
    tn = 1024
    return pl.pallas_call(
        _ada_kernel,
        grid=(n // tn,),
        in_specs=[pl.BlockSpec((c_pad.shape[0], D_MODEL), lambda j: (0, 0)),
                  pl.BlockSpec((D_MODEL, tn), lambda j: (0, j)),
                  pl.BlockSpec((1, tn), lambda j: (0, j))],
        out_specs=pl.BlockSpec((c_pad.shape[0], tn), lambda j: (0, j)),
        out_shape=jax.ShapeDtypeStruct((c_pad.shape[0], n), F32),
        compiler_params=_cparams(("arbitrary",)),
        name="ada_mod",
    )(c_pad, ada_w, ada_b.reshape(1, n))


def _mod_spec(which, tiles_per_batch):
    tiles, bsz = tiles_per_batch
    return pl.BlockSpec((1, 1, D_MODEL), lambda i: (jnp.minimum(i // tiles, bsz) * 6 + which, 0, 0))


def _modulate(x, g, sh, sc):
    ms = jnp.mean(x * x, axis=-1, keepdims=True)
    y = x * lax.rsqrt(ms + EPS) * g
    return y * (1.0 + sc) + sh


def _modproj_kernel(*refs, n_lat_tiles, two_src, widths, q_scale):
    if two_src:
        x_ref, c_ref, sh_ref, sc_ref, g_ref, w_ref = refs[:6]
        o_refs = refs[6:]
        i = pl.program_id(0)
        x = jnp.where(i < n_lat_tiles, x_ref[...], c_ref[...])
    else:
        x_ref, sh_ref, sc_ref, g_ref, w_ref = refs[:5]
        o_refs = refs[5:]
        x = x_ref[...]
    h = _modulate(x, g_ref[...], sh_ref[0], sc_ref[0]).astype(BF16)
    off = 0
    for j, (o_ref, wd) in enumerate(zip(o_refs, widths)):
        y = _dot(h, w_ref[:, off:off + wd])
        if j == 0 and q_scale != 1.0:
            y = y * q_scale
        o_ref[...] = y.astype(o_ref.dtype)
        off += wd


def _modproj(x_srcs, mod, norm_g, w_bf, widths, n_lat, n_tok, tiles_per_batch, q_scale=1.0):
    n_tiles = n_tok // TM
    n_lat_tiles = n_lat // TM
    two_src = len(x_srcs) == 2
    n_out = w_bf.shape[1]
    if two_src:
        x_specs = [pl.BlockSpec((TM, D_MODEL), lambda i: (jnp.minimum(i, n_lat_tiles - 1), 0)),
                   pl.BlockSpec((TM, D_MODEL), lambda i: (jnp.maximum(i - n_lat_tiles, 0), 0))]
    else:
        x_specs = [pl.BlockSpec((TM, D_MODEL), lambda i: (i, 0))]
    return pl.pallas_call(
        functools.partial(_modproj_kernel, n_lat_tiles=n_lat_tiles, two_src=two_src,
                          widths=tuple(widths), q_scale=q_scale),
        grid=(n_tiles,),
        in_specs=x_specs + [_mod_spec(0, tiles_per_batch), _mod_spec(1, tiles_per_batch),
                            pl.BlockSpec((1, D_MODEL), lambda i: (0, 0)),
                            pl.BlockSpec((D_MODEL, n_out), lambda i: (0, 0))],
        out_specs=[pl.BlockSpec((TM, wd), lambda i: (i, 0)) for wd in widths],
        out_shape=[jax.ShapeDtypeStruct((n_tok, wd), BF16) for wd in widths],
        compiler_params=_cparams(("arbitrary",)),
        name="modulate_in_proj",
    )(*x_srcs, mod, mod, norm_g.reshape(1, D_MODEL), w_bf)


def _modproj_t_kernel(x_ref, sh_ref, sc_ref, g_ref, wt_ref, o_ref):
    h = _modulate(x_ref[...], g_ref[...], sh_ref[0], sc_ref[0]).astype(BF16)
    o_ref[...] = _dot_nt(wt_ref[...], h).astype(o_ref.dtype)


def _modproj_t(x_all, mod, norm_g, wt_bf, n_tok, tiles_per_batch):
    n_out = wt_bf.shape[0]
    return pl.pallas_call(
        _modproj_t_kernel,
        grid=(n_tok // TM,),
        in_specs=[pl.BlockSpec((TM, D_MODEL), lambda i: (i, 0)),
                  _mod_spec(0, tiles_per_batch), _mod_spec(1, tiles_per_batch),
                  pl.BlockSpec((1, D_MODEL), lambda i: (0, 0)),
                  pl.BlockSpec((n_out, D_MODEL), lambda i: (0, 0))],
        out_specs=pl.BlockSpec((n_out, TM), lambda i: (0, i)),
        out_shape=jax.ShapeDtypeStruct((n_out, n_tok), BF16),
        compiler_params=_cparams(("arbitrary",)),
        name="modulate_in_proj_t",
    )(x_all, mod, mod, norm_g.reshape(1, D_MODEL), wt_bf)


Q_ROWS = 4
BAND_ROWS = Q_ROWS + MAX_KR


def _na_bias_table(rpb, rows):
    qc = np.arange(GRID_W)[:, None]
    kc = np.arange(GRID_W)[None, :]
    win0 = np.clip(qc - KC // 2, 0, GRID_W - KC)
    col_valid = (kc >= win0) & (kc < win0 + KC)
    col_off = np.clip(kc - qc, -(KC - 1), KC - 1) + KC - 1
    cb = rpb.astype(F32)[:, :, col_off]
    cb = jnp.where(jnp.asarray(col_valid)[None, None], cb, -jnp.inf)
    n_blk = rows // Q_ROWS
    idx = np.zeros((3, Q_ROWS, BAND_ROWS), np.int64)
    ok = np.zeros((3, Q_ROWS, BAND_ROWS), bool)
    for cls, blk in enumerate((0, 1, n_blk - 1)):
        band0 = int(np.clip(blk * Q_ROWS - MAX_KR // 2, 0, rows - BAND_ROWS))
        for rho in range(Q_ROWS):
            r = blk * Q_ROWS + rho
            r0 = int(np.clip(r - MAX_KR // 2, 0, rows - MAX_KR))
            for kap in range(BAND_ROWS):
                kr = band0 + kap
                ok[cls, rho, kap] = r0 <= kr < r0 + MAX_KR
                idx[cls, rho, kap] = np.clip(kr - r + MAX_KR - 1, 0, 2 * MAX_KR - 2)
    t = cb[:, idx]
    t = jnp.where(jnp.asarray(ok)[None, :, :, :, None, None], t, -jnp.inf)
    t = jnp.transpose(t, (1, 0, 2, 4, 3, 5))
    return t.reshape(3, NA_HEADS, Q_ROWS * GRID_W, BAND_ROWS * GRID_W)


def _attn_kernel(q_ref, k_ref, v_ref, qc_ref, kc_ref, vc_ref, bias_ref, o_ref, oc_ref, *, rows):
    lane = lax.broadcasted_iota(jnp.int32, (1, 2 * HEAD_DIM), 1)
    first = lane < HEAD_DIM
    kc = kc_ref[...]
    vc = vc_ref[...]
    zero = jnp.zeros((), BF16)

    def softmax_pv(s_list, v_list):
        m = s_list[0].max(axis=-1, keepdims=True)
        for s in s_list[1:]:
            m = jnp.maximum(m, s.max(axis=-1, keepdims=True))
        den = None
        acc = None
        for s, vv in zip(s_list, v_list):
            p = jnp.exp(s - m)
            ps = p.sum(axis=-1, keepdims=True)
            den = ps if den is None else den + ps
            pv = _dot(p.astype(BF16), vv)
            acc = pv if acc is None else acc + pv
        return acc / den

    n_blk = rows // Q_ROWS
    q_len = Q_ROWS * GRID_W

    def block_body(i, carry):
        band0 = jnp.clip(i * Q_ROWS - MAX_KR // 2, 0, rows - BAND_ROWS)
        cls = jnp.where(i == 0, 0, jnp.where(i == n_blk - 1, 2, 1))
        rows_q = pl.ds(pl.multiple_of(i * q_len, q_len), q_len)
        q_b = q_ref[rows_q, :]
        band = pl.ds(pl.multiple_of(band0 * GRID_W, GRID_W), BAND_ROWS * GRID_W)
        kb = k_ref[band, :]
        vb = v_ref[band, :]
        outs = []
        for hh in range(2):
            q_h = jnp.where(first if hh == 0 else ~first, q_b, zero)
            s_w = _dot_nt(q_h, kb) + bias_ref[cls, hh]
            s_c = _dot_nt(q_h, kc)
            outs.append(softmax_pv([s_w, s_c], [vb, vc]))
        o_ref[rows_q, :] = jnp.where(first, outs[0], outs[1]).astype(o_ref.dtype)
        return carry

    lax.fori_loop(0, n_blk, block_body, 0)

    q_c = qc_ref[...]
    outs = []
    for hh in range(2):
        q_h = jnp.where(first if hh == 0 else ~first, q_c, zero)
        outs.append(softmax_pv([_dot_nt(q_h, kc)], [vc]))
    oc_ref[...] = jnp.where(first, outs[0], outs[1]).astype(oc_ref.dtype)


def _attention(q, k, v, rpb, bsz, seq, ctx_len):
    rows = seq // GRID_W
    assert rows % Q_ROWS == 0 and rows >= 3 * Q_ROWS and seq % ctx_len == 0
    bias = _na_bias_table(rpb, rows)
    cb0 = bsz * seq // ctx_len
    lat = pl.BlockSpec((seq, 2 * HEAD_DIM), lambda hp, b: (b, hp))
    ctx = pl.BlockSpec((ctx_len, 2 * HEAD_DIM), lambda hp, b: (cb0 + b, hp))
    return pl.pallas_call(
        functools.partial(_attn_kernel, rows=rows),
        grid=(NA_HEADS // 2, bsz),
        in_specs=[lat, lat, lat, ctx, ctx, ctx,
                  pl.BlockSpec((3, 2, Q_ROWS * GRID_W, BAND_ROWS * GRID_W), lambda hp, b: (0, hp, 0, 0))],
        out_specs=[pl.BlockSpec((seq, 2 * HEAD_DIM), lambda hp, b: (b, hp)),
                   pl.BlockSpec((ctx_len, 2 * HEAD_DIM), lambda hp, b: (b, hp))],
        out_shape=[jax.ShapeDtypeStruct((bsz * seq, NA_WIDTH), BF16),
                   jax.ShapeDtypeStruct((bsz * ctx_len, NA_WIDTH), BF16)],
        compiler_params=_cparams(("arbitrary", "arbitrary")),
        name="na_attention",
    )(q, k, v, q, k, v, bias)


@functools.lru_cache(maxsize=None)
def _dft_tables(n):
    j = np.arange(n, dtype=np.int64)
    ang = 2.0 * np.pi * ((j[:, None] * j[None, :]) % n).astype(np.float64) / n
    return np.cos(ang).astype(BF16), np.sin(ang).astype(BF16)


def _fourier_kernel(cos_ref, sin_ref, f_ref, bdc_ref, bds_ref, bdw_ref, o_ref, *, scale):
    fb = f_ref[...]
    zr = _dot(cos_ref[...], fb)
    zs = _dot(sin_ref[...], fb)
    fr = (_dot(zr.astype(BF16), bdc_ref[...]) - _dot(zs.astype(BF16), bds_ref[...])) * scale
    o_ref[...] = _dot(fr.astype(BF16), bdw_ref[...]).astype(o_ref.dtype)


def _fourier(f_all, bdw, bsz, n, row_block0, tm):
    cos_n, sin_n = _dft_tables(n)
    cos64, sin64 = _dft_tables(FOURIER_DIM)
    eye = np.eye(FOURIER_GROUPS)
    bdc = jnp.asarray(np.kron(eye, cos64.astype(np.float32)), BF16)
    bds = jnp.asarray(np.kron(eye, sin64.astype(np.float32)), BF16)
    mt = n // tm
    small = pl.BlockSpec((FOURIER_WIDTH, FOURIER_WIDTH), lambda m, b: (0, 0))
    return pl.pallas_call(
        functools.partial(_fourier_kernel, scale=1.0 / math.sqrt(n * FOURIER_DIM)),
        grid=(mt, bsz),
        in_specs=[pl.BlockSpec((tm, n), lambda m, b: (m, 0)),
                  pl.BlockSpec((tm, n), lambda m, b: (m, 0)),
                  pl.BlockSpec((n, FOURIER_WIDTH), lambda m, b: (row_block0 + b, 0)),
                  small, small, small],
        out_specs=pl.BlockSpec((tm, FOURIER_WIDTH), lambda m, b: (b * mt + m, 0)),
        out_shape=jax.ShapeDtypeStruct((bsz * n, FOURIER_WIDTH), BF16),
        compiler_params=_cparams(("arbitrary", "arbitrary")),
        name="fourier_mix",
    )(jnp.asarray(cos_n), jnp.asarray(sin_n), f_all, bdc, bds, bdw)


def _route(h2, rw_hi_ref, rw_lo_ref, rb_ref):
    h_hi, h_lo = _split_bf16(h2)
    logits = (_dot(h_hi, rw_hi_ref[...]) + _dot(h_lo, rw_hi_ref[...]) + _dot(h_hi, rw_lo_ref[...])
              + rb_ref[...])
    lane = lax.broadcasted_iota(jnp.int32, logits.shape, 1).astype(F32)
    work = logits
    sel_any = jnp.zeros(logits.shape, jnp.bool_)
    top = None
    for _ in range(TOP_K):
        m = work.max(axis=-1, keepdims=True)
        if top is None:
            top = m
        idx = jnp.where(work == m, lane, float(N_EXPERTS)).min(axis=-1, keepdims=True)
        sel = lane == idx
        sel_any = sel_any | sel
        work = jnp.where(sel, -jnp.inf, work)
    e = jnp.where(sel_any, jnp.exp(logits - top), 0.0)
    return sel_any.astype(F32), e / e.sum(axis=-1, keepdims=True)


def _post_l0_kernel(a_ref, ac_ref, f_ref, fc_ref, x_ref, c_ref, g1_ref, sh_ref, sc_ref, ng_ref, w_ref,
                    rwh_ref, rwl_ref, rb_ref, xo_ref, hs_ref, pos_ref, gate_ref, cnt_ref, *, n_lat_tiles):
    lat = pl.program_id(0) < n_lat_tiles
    a = jnp.where(lat, a_ref[...], ac_ref[...])
    f = jnp.where(lat, f_ref[...], fc_ref[...])
    x = jnp.where(lat, x_ref[...], c_ref[...])
    y = _dot(a, w_ref[:NA_WIDTH, :]) + _dot(f, w_ref[NA_WIDTH:, :])
    x = x + g1_ref[0] * y
    xo_ref[...] = x
    h2 = _modulate(x, ng_ref[...], sh_ref[0], sc_ref[0])
    _sort_tile(h2, *_route(h2, rwh_ref, rwl_ref, rb_ref), hs_ref, pos_ref, gate_ref, cnt_ref)


def _post_l1_kernel(yt_ref, x_ref, g1_ref, sh_ref, sc_ref, ng_ref, wgt_ref, bg_ref, wot_ref,
                    rwh_ref, rwl_ref, rb_ref, xo_ref, hs_ref, pos_ref, gate_ref, cnt_ref):
    gyt = yt_ref[...]
    zt = _dot(wgt_ref[...], gyt) + bg_ref[...]
    vt = (gyt.astype(F32) * jax.nn.sigmoid(zt)).astype(BF16)
    x = x_ref[...] + g1_ref[0] * lax.dot_general(vt, wot_ref[...], (((0,), (1,)), ((), ())),
                                                  preferred_element_type=F32)
    xo_ref[...] = x
    h2 = _modulate(x, ng_ref[...], sh_ref[0], sc_ref[0])
    _sort_tile(h2, *_route(h2, rwh_ref, rwl_ref, rb_ref), hs_ref, pos_ref, gate_ref, cnt_ref)


def _router_args(router_w, router_b):
    rw_hi = router_w.astype(BF16)
    rw_lo = (router_w - rw_hi.astype(F32)).astype(BF16)
    return rw_hi, rw_lo, router_b.reshape(1, N_EXPERTS).astype(F32)


def _post_out(n_tok):
    n_tiles = n_tok // TM
    wide = pl.BlockSpec((TM, 128), lambda i: (i, 0))
    specs = [pl.BlockSpec((TM, D_MODEL), lambda i: (i, 0)),
             pl.BlockSpec((TILE_ROWS, HALF), lambda i: (i, 0)), wide, wide,
             pl.BlockSpec((8, 128), lambda i: (i, 0))]
    shapes = [jax.ShapeDtypeStruct((n_tok, D_MODEL), F32),
              jax.ShapeDtypeStruct((n_tiles * TILE_ROWS, HALF), jnp.uint32),
              jax.ShapeDtypeStruct((n_tok, 128), F32),
              jax.ShapeDtypeStruct((n_tok, 128), F32),
              jax.ShapeDtypeStruct((n_tiles * 8, 128), F32)]
    return specs, shapes


def _const_spec(shape):
    return pl.BlockSpec(shape, lambda i: tuple(0 for _ in shape))


def _post_l0(a_lat, a_ctx, f_lat, f_ctx, x, ctx, mod, norm_g, w_out_bf, router, n_lat, n_tok, tiles_per_batch):
    n_lat_tiles = n_lat // TM

    def lat(wd):
        return pl.BlockSpec((TM, wd), lambda i: (jnp.minimum(i, n_lat_tiles - 1), 0))

    def cx(wd):
        return pl.BlockSpec((TM, wd), lambda i: (jnp.maximum(i - n_lat_tiles, 0), 0))

    out_specs, out_shapes = _post_out(n_tok)
    return pl.pallas_call(
        functools.partial(_post_l0_kernel, n_lat_tiles=n_lat_tiles),
        grid=(n_tok // TM,),
        in_specs=[lat(NA_WIDTH), cx(NA_WIDTH), lat(FOURIER_WIDTH), cx(FOURIER_WIDTH),
                  lat(D_MODEL), cx(D_MODEL),
                  _mod_spec(2, tiles_per_batch), _mod_spec(3, tiles_per_batch), _mod_spec(4, tiles_per_batch),
                  _const_spec((1, D_MODEL)), _const_spec((D_MODEL, D_MODEL)),
                  _const_spec((D_MODEL, N_EXPERTS)), _const_spec((D_MODEL, N_EXPERTS)),
                  _const_spec((1, N_EXPERTS))],
        out_specs=out_specs, out_shape=out_shapes,
        compiler_params=_cparams(("arbitrary",)),
        name="post_mixer_l0",
    )(a_lat, a_ctx, f_lat, f_ctx, x, ctx, mod, mod, mod, norm_g.reshape(1, D_MODEL), w_out_bf, *router)


def _post_l1(gyt, x_all, mod, norm_g, w_glu_t_bf, b_glu, w_out_t_bf, router, n_lat, tiles_per_batch):
    out_specs, out_shapes = _post_out(n_lat)
    return pl.pallas_call(
        _post_l1_kernel,
        grid=(n_lat // TM,),
        in_specs=[pl.BlockSpec((D_MODEL, TM), lambda i: (0, i)),
                  pl.BlockSpec((TM, D_MODEL), lambda i: (i, 0)),
                  _mod_spec(2, tiles_per_batch), _mod_spec(3, tiles_per_batch), _mod_spec(4, tiles_per_batch),
                  _const_spec((1, D_MODEL)), _const_spec((D_MODEL, D_MODEL)), _const_spec((D_MODEL, 1)),
                  _const_spec((D_MODEL, D_MODEL)),
                  _const_spec((D_MODEL, N_EXPERTS)), _const_spec((D_MODEL, N_EXPERTS)),
                  _const_spec((1, N_EXPERTS))],
        out_specs=out_specs, out_shape=out_shapes,
        compiler_params=_cparams(("arbitrary",)),
        name="post_mixer_l1",
    )(gyt, x_all, mod, mod, mod, norm_g.reshape(1, D_MODEL), w_glu_t_bf, b_glu.reshape(D_MODEL, 1),
      w_out_t_bf, *router)


CHUNK = 8
TILE_ROWS = TM * TOP_K + N_EXPERTS * CHUNK
TILE_CHUNKS = TILE_ROWS // CHUNK
BLOCK_CHUNKS = EXPERT_ROWS // CHUNK
SORT_ROWS = 256
ISSUE_UNROLL = 8
FF_CHUNK = 512
HALF = D_MODEL // 2


def _pack_bf16_pairs(y):
    bits = pltpu.bitcast(y, jnp.uint32)
    return (bits[:, :HALF] & jnp.uint32(0xFFFF0000)) | (bits[:, HALF:] >> 16)


def _unpack_bf16_pairs(w):
    hi = pltpu.bitcast(w & jnp.uint32(0xFFFF0000), F32).astype(BF16)
    lo = pltpu.bitcast(w << 16, F32).astype(BF16)
    return hi, lo


def _sort_tile(h2, mf, g, hs_ref, pos_ref, gate_ref, cnt_ref):
    m = mf.astype(BF16)
    r = lax.broadcasted_iota(jnp.int32, (TM, TM), 0)
    c = lax.broadcasted_iota(jnp.int32, (TM, TM), 1)
    rank = _dot((r > c).astype(F32).astype(BF16), m)
    cnt = mf.sum(axis=0, keepdims=True)
    n8 = jnp.floor((cnt + (CHUNK - 1)) * (1.0 / CHUNK))
    er = lax.broadcasted_iota(jnp.int32, (N_EXPERTS, N_EXPERTS), 0)
    ec = lax.broadcasted_iota(jnp.int32, (N_EXPERTS, N_EXPERTS), 1)
    upper = (er < ec).astype(F32).astype(BF16)
    off = CHUNK * _dot(jnp.broadcast_to(n8, (8, N_EXPERTS)).astype(BF16), upper)[0:1]
    pos = rank + off
    slot = _dot(m, upper)
    chosen = mf > 0.5
    lane = lax.broadcasted_iota(jnp.int32, pos_ref.shape, 1)
    p_out = jnp.zeros(pos_ref.shape, F32)
    g_out = jnp.zeros(gate_ref.shape, F32)
    for k in range(TOP_K):
        sel = chosen & (slot == float(k))
        pk = jnp.where(sel, pos, 0.0).sum(axis=-1, keepdims=True)
        gk = jnp.where(sel, g, 0.0).sum(axis=-1, keepdims=True)
        p_out = jnp.where(lane == k, pk, p_out)
        g_out = jnp.where(lane == k, gk, g_out)
    pos_ref[...] = p_out
    gate_ref[...] = g_out
    cnt_ref[...] = jnp.zeros_like(cnt_ref)
    cnt_ref[0:1, 0:N_EXPERTS] = cnt

    pos_t = p_out.T
    hb = h2.astype(BF16)
    rid = lax.broadcasted_iota(jnp.int32, (SORT_ROWS, TM), 0).astype(F32).astype(BF16)
    one, nil = jnp.ones((), BF16), jnp.zeros((), BF16)
    for rc in range(TILE_ROWS // SORT_ROWS):
        rel = (pos_t[0:16, :] - float(rc * SORT_ROWS)).astype(BF16)
        onehot = jnp.where(rid == rel[0:1, :], one, nil)
        for k in range(1, TOP_K):
            onehot = jnp.where(rid == rel[k:k + 1, :], one, onehot)
        hs_ref[rc * SORT_ROWS:(rc + 1) * SORT_ROWS, :] = _pack_bf16_pairs(_dot(onehot, hb))


def _chunk_tables(cnt, n_blocks):
    n_tiles = cnt.shape[0]
    n8 = (cnt + CHUNK - 1) // CHUNK
    lend = jnp.cumsum(n8, axis=1)
    lstart = lend - n8
    cum = jnp.cumsum(n8, axis=0)
    before = cum - n8
    total = cum[-1]
    nblk = (total + BLOCK_CHUNKS - 1) // BLOCK_CHUNKS
    bend = jnp.cumsum(nblk)
    gstart = (bend - nblk) * BLOCK_CHUNKS
    n_used = jnp.maximum(bend[-1], 1)
    blk = jnp.arange(n_blocks, dtype=jnp.int32)
    block_expert = (bend[None, :] <= jnp.minimum(blk, n_used - 1)[:, None]).sum(axis=1).astype(jnp.int32)
    block_expert = jnp.minimum(block_expert, N_EXPERTS - 1)
    experts = jnp.arange(N_EXPERTS, dtype=jnp.int32)
    lc = jnp.arange(TILE_CHUNKS, dtype=jnp.int32)
    e_of = (lend[:, None, :] <= lc[None, :, None]).sum(axis=-1).astype(jnp.int32)
    shift = gstart[None, :] + before - lstart
    pick = e_of[:, :, None] == experts[None, None, :]
    to_sorted = jnp.where(pick, shift[:, None, :], 0).sum(axis=-1) + jnp.where(e_of < N_EXPERTS, lc[None, :], 0)
    of_block = block_expert[:, None] == experts[None, :]
    gstart_b = jnp.where(of_block, gstart[None, :], 0).sum(axis=-1)
    total_b = jnp.where(of_block, total[None, :], 0).sum(axis=-1)

    def per_block(table):
        return jnp.where(of_block[:, :, None], table.T[None, :, :], 0).sum(axis=1)

    cum_b, lstart_b, before_b = per_block(cum), per_block(lstart), per_block(before)
    rel = (blk[:, None] * BLOCK_CHUNKS + jnp.arange(BLOCK_CHUNKS, dtype=jnp.int32)[None, :]) - gstart_b[:, None]
    valid = (blk[:, None] < n_used) & (rel < total_b[:, None])
    tile = jnp.minimum((cum_b[:, None, :] <= rel[:, :, None]).sum(axis=-1), n_tiles - 1).astype(jnp.int32)
    of_tile = tile[:, :, None] == jnp.arange(n_tiles, dtype=jnp.int32)[None, None, :]
    inside = jnp.where(of_tile, (lstart_b - before_b)[:, None, :], 0).sum(axis=-1)
    from_tiles = jnp.where(valid, tile * TILE_CHUNKS + inside + rel, TILE_CHUNKS - 1)
    return (block_expert, n_used.reshape(1).astype(jnp.int32), from_tiles.reshape(-1).astype(jnp.int32),
            to_sorted.reshape(-1).astype(jnp.int32))


def _chunk_rows(chunk):
    start = chunk * CHUNK
    return pl.ds(start if isinstance(start, int) else pl.multiple_of(start, CHUNK), CHUNK)


def _chunk_copy(src_ref, src_chunk, dst_ref, dst_chunk, sem):
    return pltpu.make_async_copy(src_ref.at[_chunk_rows(src_chunk), :], dst_ref.at[_chunk_rows(dst_chunk), :], sem)


def _expert_kernel(be_ref, nu_ref, src_ref, hs_ref, wgu_ref, bgu_ref, wd_ref, bd_ref, o_ref,
                   wgu_bf, wd_bf, x_buf, sems):
    i = pl.program_id(0)
    n_used = nu_ref[0]

    def gather(block, slot, start):
        if not start:
            pltpu.make_async_copy(hs_ref.at[pl.ds(0, EXPERT_ROWS), :], x_buf.at[slot], sems.at[slot]).wait()
            return
        for cidx in range(BLOCK_CHUNKS):
            _chunk_copy(hs_ref, src_ref[block * BLOCK_CHUNKS + cidx], x_buf.at[slot], cidx, sems.at[slot]).start()

    @pl.when(i == 0)
    def _():
        gather(0, 0, True)

    for slot in range(2):
        @pl.when((i + 1 < n_used) & ((i + 1) % 2 == slot))
        def _(slot=slot):
            gather(i + 1, slot, True)

    @pl.when((i == 0) | (be_ref[i] != be_ref[jnp.maximum(i - 1, 0)]))
    def _():
        wgu_bf[...] = wgu_ref[0].astype(BF16)
        wd_bf[...] = wd_ref[0].astype(BF16)

    for slot in range(2):
        @pl.when((i < n_used) & (i % 2 == slot))
        def _(slot=slot):
            gather(i, slot, False)
            x_hi, x_lo = _unpack_bf16_pairs(x_buf[slot])
            y = None
            for j in range(D_MODEL // FF_CHUNK):
                def proj(col0):
                    cols = slice(col0 + j * FF_CHUNK, col0 + (j + 1) * FF_CHUNK)
                    return (_dot(x_hi, wgu_bf[:HALF, cols]) + _dot(x_lo, wgu_bf[HALF:, cols])
                            + bgu_ref[0, :, cols])
                gate = jnp.minimum(proj(0), SWIGLU_LIMIT)
                up = jnp.clip(proj(D_MODEL), -SWIGLU_LIMIT, SWIGLU_LIMIT)
                act = (up + 1.0) * (gate * jax.nn.sigmoid(SWIGLU_ALPHA * gate))
                part = _dot(act.astype(BF16), wd_bf[j * FF_CHUNK:(j + 1) * FF_CHUNK, :])
                y = part if y is None else y + part
            y = y + bd_ref[0]
            o_ref[...] = _pack_bf16_pairs(y.astype(BF16).astype(F32))

    @pl.when(i >= n_used)
    def _():
        o_ref[...] = jnp.zeros_like(o_ref)


def _experts(hs, block_expert, n_used, from_tiles, wgu, bgu, wd, bd):
    n_blocks = block_expert.shape[0]

    def row_map(i, be, nu, src):
        return (i, 0)

    def exp_map(i, be, nu, src):
        return (be[i], 0, 0)

    grid_spec = pltpu.PrefetchScalarGridSpec(
        num_scalar_prefetch=3,
        grid=(n_blocks,),
        in_specs=[pl.BlockSpec(memory_space=pl.ANY),
                  pl.BlockSpec((1, D_MODEL, 2 * D_MODEL), exp_map),
                  pl.BlockSpec((1, 1, 2 * D_MODEL), exp_map),
                  pl.BlockSpec((1, D_MODEL, D_MODEL), exp_map),
                  pl.BlockSpec((1, 1, D_MODEL), exp_map)],
        out_specs=pl.BlockSpec((EXPERT_ROWS, HALF), row_map),
        scratch_shapes=[pltpu.VMEM((D_MODEL, 2 * D_MODEL), BF16), pltpu.VMEM((D_MODEL, D_MODEL), BF16),
                        pltpu.VMEM((2, EXPERT_ROWS, HALF), jnp.uint32), pltpu.SemaphoreType.DMA((2,))],
    )
    return pl.pallas_call(
        _expert_kernel,
        grid_spec=grid_spec,
        out_shape=jax.ShapeDtypeStruct((n_blocks * EXPERT_ROWS, HALF), jnp.uint32),
        compiler_params=pltpu.CompilerParams(dimension_semantics=("arbitrary",),
                                             vmem_limit_bytes=EXPERT_VMEM_LIMIT),
        name="moe_experts",
    )(block_expert, n_used, from_tiles, hs, wgu, bgu.reshape(N_EXPERTS, 1, 2 * D_MODEL), wd,
      bd.reshape(N_EXPERTS, 1, D_MODEL))


def _combine_kernel(dst_ref, ys_ref, pos_ref, gate_ref, x_ref, g2_ref, ng_ref, o_ref, y_buf, sems, *, final_norm):
    i = pl.program_id(0)
    n_tiles = pl.num_programs(0)

    def gather(tile, slot, start):
        if not start:
            pltpu.make_async_copy(ys_ref.at[pl.ds(0, TILE_ROWS), :], y_buf.at[slot], sems.at[slot]).wait()
            return

        def body(g, carry):
            for j in range(ISSUE_UNROLL):
                cidx = g * ISSUE_UNROLL + j
                _chunk_copy(ys_ref, dst_ref[tile * TILE_CHUNKS + cidx], y_buf.at[slot], cidx, sems.at[slot]).start()
            return carry
        lax.fori_loop(0, TILE_CHUNKS // ISSUE_UNROLL, body, 0)

    @pl.when(i == 0)
    def _():
        gather(0, 0, True)

    for slot in range(2):
        @pl.when((i + 1 < n_tiles) & ((i + 1) % 2 == slot))
        def _(slot=slot):
            gather(i + 1, slot, True)

    pos = pos_ref[...]
    gate = gate_ref[...]
    for slot in range(2):
        @pl.when(i % 2 == slot)
        def _(slot=slot):
            gather(i, slot, False)
            acc_hi = jnp.zeros((TM, HALF), F32)
            acc_lo = jnp.zeros((TM, HALF), F32)
            rid = lax.broadcasted_iota(jnp.int32, (TM, SORT_ROWS), 1).astype(F32).astype(BF16)
            gate_b = gate.astype(BF16)
            for rc in range(TILE_ROWS // SORT_ROWS):
                rel = (pos - float(rc * SORT_ROWS)).astype(BF16)
                w = jnp.where(rid == rel[:, 0:1], gate_b[:, 0:1], jnp.zeros((), BF16))
                for k in range(1, TOP_K):
                    w = jnp.where(rid == rel[:, k:k + 1], gate_b[:, k:k + 1], w)
                y_hi, y_lo = _unpack_bf16_pairs(y_buf[slot, rc * SORT_ROWS:(rc + 1) * SORT_ROWS, :])
                acc_hi = acc_hi + _dot(w, y_hi)
                acc_lo = acc_lo + _dot(w, y_lo)
            x = x_ref[...] + g2_ref[0] * jnp.concatenate([acc_hi, acc_lo], axis=-1)
            if final_norm:
                ms = jnp.mean(x * x, axis=-1, keepdims=True)
                x = x * lax.rsqrt(ms + EPS) * ng_ref[...]
            o_ref[...] = x


def _combine(ys, to_sorted, pos4, gate4, x_res, mod, norm_g, tiles_per_batch, final_norm):
    n_tok = x_res.shape[0]
    tok = pl.BlockSpec((TM, D_MODEL), lambda i, dst: (i, 0))
    wide = pl.BlockSpec((TM, 128), lambda i, dst: (i, 0))
    tiles, bsz = tiles_per_batch
    grid_spec = pltpu.PrefetchScalarGridSpec(
        num_scalar_prefetch=1,
        grid=(n_tok // TM,),
        in_specs=[pl.BlockSpec(memory_space=pl.ANY), wide, wide, tok,
                  pl.BlockSpec((1, 1, D_MODEL), lambda i, dst: (jnp.minimum(i // tiles, bsz) * 6 + 5, 0, 0)),
                  pl.BlockSpec((1, D_MODEL), lambda i, dst: (0, 0))],
        out_specs=tok,
        scratch_shapes=[pltpu.VMEM((2, TILE_ROWS, HALF), jnp.uint32), pltpu.SemaphoreType.DMA((2,))],
    )
    return pl.pallas_call(
        functools.partial(_combine_kernel, final_norm=final_norm),
        grid_spec=grid_spec,
        out_shape=jax.ShapeDtypeStruct((n_tok, D_MODEL), F32),
        compiler_params=_cparams(("arbitrary",)),
        name="moe_combine",
    )(to_sorted, ys, pos4, gate4, x_res, mod, norm_g.reshape(1, D_MODEL))


def _moe(hs, pos4, gate4, cnt, x_res, mod, moe_w, tiles_per_batch, norm_g, final_norm):
    wgu, bgu, wd, bd = moe_w
    n_tiles = x_res.shape[0] // TM
    n_blocks = -(-n_tiles * TILE_CHUNKS // BLOCK_CHUNKS) + N_EXPERTS
    cnt_i = cnt.reshape(n_tiles, 8, 128)[:, 0, :N_EXPERTS].astype(jnp.int32)
    block_expert, n_used, from_tiles, to_sorted = _chunk_tables(cnt_i, n_blocks)
    ys = _experts(hs, block_expert, n_used, from_tiles, wgu, bgu, wd, bd)
    return _combine(ys, to_sorted, pos4, gate4, x_res, mod, norm_g, tiles_per_batch, final_norm)


def _s5_tables(lam_re, lam_im, log_dt, b_re, b_im, c_re, c_im, d_skip):
    q = S5_Q
    lam = lax.complex(jnp.minimum(lam_re.astype(F32), LAMBDA_RE_MAX), lam_im.astype(F32))
    dt = jnp.exp(log_dt.astype(F32))[..., None]
    a = jnp.exp(lam * dt)
    bbar = ((a - 1) / lam)[..., None] * lax.complex(b_re.astype(F32), b_im.astype(F32))
    cc = lax.complex(c_re.astype(F32), c_im.astype(F32))
    steps = jnp.arange(q + 1, dtype=F32)
    pw = jnp.exp((lam * dt)[:, :, None, :] * steps[None, None, :, None])
    kern = jnp.einsum('dgop,dgkp,dgpc->dgcok', cc, pw[:, :, :q], bbar).real
    eye = jnp.eye(SSM_GROUP, dtype=F32)
    k_fwd = kern[0].at[..., 0].add(kern[1][..., 0] + d_skip.astype(F32)[:, :, None] * eye[None])
    k_bwd = jnp.concatenate([jnp.zeros_like(kern[1][..., :1]), kern[1][..., :0:-1]], axis=-1)
    k_fwd = k_fwd.reshape(SSM_GROUPS, SSM_GROUP * SSM_GROUP, q)
    k_bwd = k_bwd.reshape(SSM_GROUPS, SSM_GROUP * SSM_GROUP, q)
    def both(f, b):
        return jnp.concatenate([f, b], axis=-1)

    e_in = both(pw[0][:, ::-1][:, 1:], pw[1][:, :q])
    b_in = both(jnp.transpose(bbar[0], (0, 2, 1)), jnp.transpose(bbar[1], (0, 2, 1)))
    e_out = jnp.transpose(both(pw[0][:, 1:], pw[1][:, ::-1][:, :q]), (0, 2, 1))
    c_out = jnp.transpose(both(cc[0], cc[1]), (0, 2, 1))
    a_q = both(pw[0][:, q], pw[1][:, q]).reshape(SSM_GROUPS, 1, 2 * SSM_STATE)
    parts = [t for z in (e_in, b_in, e_out, c_out, a_q) for t in (z.real, z.imag)]
    return [k_fwd, k_bwd] + parts


def _s5_kernel(u_ref, kf_ref, kb_ref, er_ref, ei_ref, br_ref, bi_ref, pr_ref, pi_ref, cr_ref, ci_ref,
               ar_ref, ai_ref, o_ref, w_ref, s_ref, h_ref, *, bsz, n_lat_chunks, n_ctx_chunks):
    q = S5_Q
    half = 2 * SSM_STATE
    n_lat = bsz * n_lat_chunks

    causal = (lax.broadcasted_iota(jnp.int32, (q, q), 1) >= lax.broadcasted_iota(jnp.int32, (q, q), 0))

    def build(c, carry):
        for o in range(SSM_GROUP):
            row = pl.ds(c * SSM_GROUP + o, 1)
            kf = pltpu.roll(jnp.broadcast_to(kf_ref[0, row, :], (q, q)), 0, 1, stride=1, stride_axis=0)
            kb = pltpu.roll(jnp.broadcast_to(kb_ref[0, row, :], (q, q)), 0, 1, stride=1, stride_axis=0)
            w_ref[pl.ds(pl.multiple_of(c * q, q), q), o * q:(o + 1) * q] = jnp.where(causal, kf, kb).astype(BF16)
        return carry

    lax.fori_loop(0, SSM_GROUP, build, 0)

    u = u_ref[...]
    er, ei = er_ref[0], ei_ref[0]
    s_re = None
    for c in range(SSM_GROUP):
        br, bi = br_ref[0, c:c + 1, :], bi_ref[0, c:c + 1, :]
        w_re = (er * br - ei * bi).astype(BF16)
        w_im = (er * bi + ei * br).astype(BF16)
        s_re = _dot(u[c], w_re) if s_re is None else s_re + _dot(u[c], w_re)
        s_im = _dot(u[c], w_im) if c == 0 else s_im + _dot(u[c], w_im)
    s_ref[0] = s_re
    s_ref[1] = s_im

    fwd = lax.broadcasted_iota(jnp.int32, (bsz, half), 1) < SSM_STATE
    ar = jnp.broadcast_to(ar_ref[0], (bsz, half))
    ai = jnp.broadcast_to(ai_ref[0], (bsz, half))

    def chunk_rows(base, j, per_sample):
        return pl.ds(base + j, bsz, stride=per_sample)

    def load(base, j, n):
        rows_f = chunk_rows(base, j, n)
        rows_b = chunk_rows(base, n - 1 - j, n)
        return (jnp.where(fwd, s_ref[0, rows_f, :], s_ref[0, rows_b, :]),
                jnp.where(fwd, s_ref[1, rows_f, :], s_ref[1, rows_b, :]))

    def step(hr, hi, sr, si):
        return ar * hr - ai * hi + sr, ar * hi + ai * hr + si

    def ctx_body(j, carry):
        return step(*carry, *load(n_lat, j, n_ctx_chunks))

    def lat_body(j, carry):
        hr, hi = carry
        rows_f = chunk_rows(0, j, n_lat_chunks)
        rows_b = chunk_rows(0, n_lat_chunks - 1 - j, n_lat_chunks)
        h_ref[0, rows_f, :] = jnp.where(fwd, hr, 0.0)
        h_ref[1, rows_f, :] = jnp.where(fwd, hi, 0.0)
        h_ref[2, rows_b, :] = jnp.where(fwd, 0.0, hr)
        h_ref[3, rows_b, :] = jnp.where(fwd, 0.0, hi)
        return step(hr, hi, *load(0, j, n_lat_chunks))

    zero = jnp.zeros((bsz, half), F32)
    carry = lax.fori_loop(0, n_ctx_chunks, ctx_body, (zero, zero))
    lax.fori_loop(0, n_lat_chunks, lat_body, carry)

    ucat = jnp.concatenate([u[c][:n_lat] for c in range(SSM_GROUP)], axis=-1)
    y = _dot(ucat, w_ref[...])
    h_re = (h_ref[0] + h_ref[2]).astype(BF16)
    h_im = (h_ref[1] + h_ref[3]).astype(BF16)
    pr, pi = pr_ref[0], pi_ref[0]
    cr_all, ci_all = cr_ref[0], ci_ref[0]
    for o in range(SSM_GROUP):
        cr, ci = cr_all[:, o:o + 1], ci_all[:, o:o + 1]
        w_re = (pr * cr - pi * ci).astype(BF16)
        w_im = (pr * ci + pi * cr).astype(BF16)
        y_o = y[:, o * q:(o + 1) * q] + _dot(h_re, w_re) - _dot(h_im, w_im)
        o_ref[o] = jax.nn.gelu(y_o).astype(o_ref.dtype)


def _s5(u3, tables, bsz, n_lat_chunks, n_ctx_chunks):
    q = S5_Q
    n_chunks = u3.shape[1]
    n_lat = bsz * n_lat_chunks
    assert n_chunks == n_lat + bsz * n_ctx_chunks

    def gspec(*shape):
        return pl.BlockSpec((1,) + shape, lambda g: (g,) + tuple(0 for _ in shape))

    return pl.pallas_call(
        functools.partial(_s5_kernel, bsz=bsz, n_lat_chunks=n_lat_chunks, n_ctx_chunks=n_ctx_chunks),
        grid=(SSM_GROUPS,),
        in_specs=[pl.BlockSpec((SSM_GROUP, n_chunks, q), lambda g: (g, 0, 0)),
                  gspec(SSM_GROUP * SSM_GROUP, q), gspec(SSM_GROUP * SSM_GROUP, q),
                  gspec(q, 2 * SSM_STATE), gspec(q, 2 * SSM_STATE),
                  gspec(SSM_GROUP, 2 * SSM_STATE), gspec(SSM_GROUP, 2 * SSM_STATE),
                  gspec(2 * SSM_STATE, q), gspec(2 * SSM_STATE, q),
                  gspec(2 * SSM_STATE, SSM_GROUP), gspec(2 * SSM_STATE, SSM_GROUP),
                  gspec(1, 2 * SSM_STATE), gspec(1, 2 * SSM_STATE)],
        out_specs=pl.BlockSpec((SSM_GROUP, n_lat, q), lambda g: (g, 0, 0)),
        out_shape=jax.ShapeDtypeStruct((SSM_GROUPS * SSM_GROUP, n_lat, q), BF16),
        scratch_shapes=[pltpu.VMEM((SSM_GROUP * q, SSM_GROUP * q), BF16),
                        pltpu.VMEM((2, n_chunks, 2 * SSM_STATE), F32),
                        pltpu.VMEM((4, n_lat, 2 * SSM_STATE), F32)],
        compiler_params=_cparams(("arbitrary",)),
        name="s5_scan",
    )(u3, *tables)


def _moe_weights(w_gate_up, b_gate_up, w_down, b_down):
    return w_gate_up, b_gate_up, w_down, b_down


def kernel(x, c, ctx, c_ctx, l0_ada_w, l0_ada_b, l0_norm_mix, l0_w_in, l0_rpb, l0_w_fourier, l0_w_out, l0_norm_ffn, l0_router_w, l0_router_b, l0_w_gate_up, l0_b_gate_up, l0_w_down, l0_b_down, l1_ada_w, l1_ada_b, l1_norm_mix, l1_w_in, l1_lambda_re, l1_lambda_im, l1_log_dt, l1_b_re, l1_b_im, l1_c_re, l1_c_im, l1_d_skip, l1_w_glu, l1_b_glu, l1_w_out, l1_norm_ffn, l1_router_w, l1_router_b, l1_w_gate_up, l1_b_gate_up, l1_w_down, l1_b_down, final_norm):
    bsz, seq, _ = x.shape
    ctx_len = ctx.shape[1]
    n_lat = bsz * seq
    n_ctx = bsz * ctx_len
    n_tok = n_lat + n_ctx
    assert seq % TM == 0 and n_ctx % TM == 0
    x2 = x.reshape(n_lat, D_MODEL)
    ctx2 = ctx.reshape(n_ctx, D_MODEL)

    n_mod = bsz + 1
    c_pad = jnp.zeros((-(-n_mod // 8) * 8, D_MODEL), F32).at[:bsz].set(c).at[bsz].set(c_ctx)
    mod0 = _ada(c_pad, l0_ada_w, l0_ada_b)[:n_mod].reshape(n_mod * 6, 1, D_MODEL)
    mod1 = _ada(c_pad, l1_ada_w, l1_ada_b)[:n_mod].reshape(n_mod * 6, 1, D_MODEL)
    tpb = (seq // TM, bsz)

    q, k, v, f = _modproj([x2, ctx2], mod0, l0_norm_mix, l0_w_in.astype(BF16),
                          (NA_WIDTH, NA_WIDTH, NA_WIDTH, FOURIER_WIDTH), n_lat, n_tok, tpb,
                          q_scale=HEAD_DIM ** -0.5)
    a_lat, a_ctx = _attention(q, k, v, l0_rpb, bsz, seq, ctx_len)
    bdw = (jnp.eye(FOURIER_GROUPS, dtype=F32)[:, None, :, None] * l0_w_fourier[:, :, None, :]).reshape(
        FOURIER_WIDTH, FOURIER_WIDTH).astype(BF16)
    f_lat = _fourier(f, bdw, bsz, seq, 0, TM)
    f_ctx = _fourier(f, bdw, bsz, ctx_len, n_lat // ctx_len, ctx_len)
    x_all, *routed = _post_l0(a_lat, a_ctx, f_lat, f_ctx, x2, ctx2, mod0, l0_norm_ffn,
                              l0_w_out.astype(BF16), _router_args(l0_router_w, l0_router_b), n_lat, n_tok, tpb)
    x_all = _moe(*routed, x_all, mod0, _moe_weights(l0_w_gate_up, l0_b_gate_up, l0_w_down, l0_b_down),
                 tpb, final_norm, False)

    assert seq % S5_Q == 0 and ctx_len % S5_Q == 0
    ut = _modproj_t(x_all, mod1, l1_norm_mix, l1_w_in.T.astype(BF16), n_tok, tpb)
    tables = _s5_tables(l1_lambda_re, l1_lambda_im, l1_log_dt, l1_b_re, l1_b_im, l1_c_re, l1_c_im, l1_d_skip)
    yt = _s5(ut.reshape(D_MODEL, n_tok // S5_Q, S5_Q), tables, bsz, seq // S5_Q, ctx_len // S5_Q)
    gy = yt.reshape(D_MODEL, n_lat)
    x1, *routed = _post_l1(gy, x_all, mod1, l1_norm_ffn, l1_w_glu.T.astype(BF16), l1_b_glu,
                           l1_w_out.T.astype(BF16), _router_args(l1_router_w, l1_router_b), n_lat, tpb)
    out = _moe(*routed, x1, mod1, _moe_weights(l1_w_gate_up, l1_b_gate_up, l1_w_down, l1_b_down),
               tpb, final_norm, True)
    return out.reshape(bsz, seq, D_MODEL)
```
